```python
import jax, jax.numpy as jnp
from jax import lax
import numpy as np


D_MODEL = 1024
BATCH = 8
SEQ = 4096
DEPTH = 4

GROUP_W = D_MODEL // 2
HEAD_DIM = 64
N_HEADS = GROUP_W // HEAD_DIM
D_MIX = 3 * GROUP_W
NORM_EPS = 1e-6
DECAY_LORA = 64
ICL_LORA = 64
RWKV_IN = 3 * GROUP_W + DECAY_LORA + ICL_LORA
GN_EPS = 64e-5
CONV_W = 4
LRU_C = 8.0
Q_LORA = D_MODEL // 4
KV_LORA = D_MODEL // 8
QK_NOPE = 64
QK_ROPE = 32
V_DIM = HEAD_DIM
ROPE_BASE = 10000.0
Q_BLOCK = 128
SPLITS = (RWKV_IN, GROUP_W, GROUP_W, GROUP_W, Q_LORA, KV_LORA, QK_ROPE, GROUP_W)
D_IN = RWKV_IN + GROUP_W + GROUP_W + GROUP_W + Q_LORA + KV_LORA + QK_ROPE + GROUP_W

kernel_name = "hybrid_rwkv7_rglru_mla_heads"


def _split_points(sizes):
    pts, acc = [], 0
    for s in sizes[:-1]:
        acc += s
        pts.append(acc)
    return pts


def rms_norm(x, g, eps=NORM_EPS):
    xf = x.astype(jnp.float32)
    y = xf * lax.rsqrt(jnp.mean(xf * xf, axis=-1, keepdims=True) + eps)
    return (y * g.astype(jnp.float32)).astype(x.dtype)


def token_shift(u):
    return jnp.pad(u[:, :-1], ((0, 0), (1, 0), (0, 0)))


def rwkv7_mix(u, mu, w0, w2, a0, a2, k_k, k_a, r_k, lnx_g, lnx_b):
    B, T, _ = u.shape
    u = u.astype(jnp.float32)
    u = u + mu * (token_shift(u) - u)
    r, k, v, wl, al = jnp.split(u, _split_points((GROUP_W, GROUP_W, GROUP_W, DECAY_LORA, ICL_LORA)), axis=-1)
    log_w = -jax.nn.softplus(-(w0 + jnp.tanh(wl) @ w2)) - 0.5
    decay = jnp.exp(-jnp.exp(log_w))
    a = jax.nn.sigmoid(a0 + al @ a2)
    heads = lambda t: t.reshape(B, T, N_HEADS, HEAD_DIM)
    kk = heads(k * k_k)
    kk = kk / jnp.maximum(jnp.sqrt(jnp.sum(kk * kk, axis=-1, keepdims=True)), 1e-12)
    k = k * (1.0 + (a - 1.0) * k_a)
    r_h, w_h, k_h, v_h, a_h = heads(r), heads(decay), heads(k), heads(v), heads(a)
    b_h = kk * a_h

    def step(S, inp):
        r_t, w_t, k_t, v_t, kk_t, b_t = inp
        sa = jnp.einsum('bhij,bhj->bhi', S, kk_t)
        S = S * w_t[:, :, None, :] - sa[..., None] * b_t[:, :, None, :] + v_t[..., None] * k_t[:, :, None, :]
        return S, jnp.einsum('bhij,bhj->bhi', S, r_t)

    xs = tuple(jnp.swapaxes(t, 0, 1) for t in (r_h, w_h, k_h, v_h, kk, b_h))
    S0 = jnp.zeros((B, N_HEADS, HEAD_DIM, HEAD_DIM), jnp.float32)
    _, y = lax.scan(step, S0, xs)
    y = jnp.swapaxes(y, 0, 1)
    mean = jnp.mean(y, axis=-1, keepdims=True)
    var = jnp.mean(jnp.square(y - mean), axis=-1, keepdims=True)
    y = ((y - mean) * lax.rsqrt(var + GN_EPS)).reshape(B, T, GROUP_W) * lnx_g + lnx_b
    bonus = jnp.sum(r_h * k_h * r_k, axis=-1, keepdims=True) * v_h
    return y + bonus.reshape(B, T, GROUP_W)


def rglru_mix(u, conv_w, conv_b, ga_w, ga_b, gx_w, gx_b, lam):
    B, T, _ = u.shape
    uf = u.astype(jnp.float32)
    up = jnp.pad(uf, ((0, 0), (CONV_W - 1, 0), (0, 0)))
    xc = conv_b + sum(up[:, j:j + T] * conv_w[j] for j in range(CONV_W))
    xb = xc.reshape(B, T, N_HEADS, HEAD_DIM)
    gate_r = jax.nn.sigmoid(jnp.einsum('bthi,hij->bthj', xb, ga_w).reshape(B, T, GROUP_W) + ga_b)
    gate_i = jax.nn.sigmoid(jnp.einsum('bthi,hij->bthj', xb, gx_w).reshape(B, T, GROUP_W) + gx_b)
    log_a = -LRU_C * gate_r * jax.nn.softplus(-lam)
    a = jnp.exp(log_a)
    bx = jnp.sqrt(-jnp.expm1(2.0 * log_a)) * (gate_i * xc)

    def combine(lhs, rhs):
        a_l, b_l = lhs
        a_r, b_r = rhs
        return a_l * a_r, a_r * b_l + b_r

    _, h = lax.associative_scan(combine, (a, bx), axis=1)
    return h


def apply_rope(x, cos, sin):
    half = x.shape[-1] // 2
    x1, x2 = x[..., :half], x[..., half:]
    return jnp.concatenate([x1 * cos - x2 * sin, x1 * sin + x2 * cos], axis=-1)


def mla_mix(q_lat, kv_lat, k_rope, q_norm_g, w_uq, kv_norm_g, w_ukv, cos, sin):
    B, T, _ = q_lat.shape
    q = (rms_norm(q_lat, q_norm_g) @ w_uq).reshape(B, T, N_HEADS, QK_NOPE + QK_ROPE)
    q_nope, q_rope = q[..., :QK_NOPE], q[..., QK_NOPE:]
    kv = (rms_norm(kv_lat, kv_norm_g) @ w_ukv).reshape(B, T, N_HEADS, QK_NOPE + V_DIM)
    k_nope, v = kv[..., :QK_NOPE], kv[..., QK_NOPE:]
    q_rope = apply_rope(q_rope, cos[:, None, :], sin[:, None, :])
    k_rope = apply_rope(k_rope, cos, sin)
    scale = (QK_NOPE + QK_ROPE) ** -0.5
    nb = T // Q_BLOCK
    qn_b = jnp.swapaxes(q_nope.reshape(B, nb, Q_BLOCK, N_HEADS, QK_NOPE), 0, 1)
    qr_b = jnp.swapaxes(q_rope.reshape(B, nb, Q_BLOCK, N_HEADS, QK_ROPE), 0, 1)
    k_pos = jnp.arange(T)

    def block(args):
        qn, qr, start = args
        s = jnp.einsum('bqhd,bkhd->bhqk', qn, k_nope) + jnp.einsum('bqhr,bkr->bhqk', qr, k_rope)
        s = s.astype(jnp.float32) * scale
        q_pos = start + jnp.arange(Q_BLOCK)
        s = jnp.where(k_pos[None, :] <= q_pos[:, None], s, -jnp.inf)
        p = jax.nn.softmax(s, axis=-1).astype(v.dtype)
        return jnp.einsum('bhqk,bkhd->bqhd', p, v)

    o = lax.map(block, (qn_b, qr_b, jnp.arange(nb) * Q_BLOCK))
    return jnp.swapaxes(o, 0, 1).reshape(B, T, GROUP_W)


def setup_inputs(seed: int = 0) -> dict:
    key = jax.random.key(seed)
    ks = jax.random.split(key, 32)
    f32 = jnp.float32
    nrm = lambda k, shape, fan_in: jax.random.normal(k, shape, f32) * (fan_in ** -0.5)
    gain = lambda k, shape: 1.0 + 0.02 * jax.random.normal(k, shape, f32)
    small = lambda k, shape: 0.01 * jax.random.normal(k, shape, f32)
    u_lam = jax.random.uniform(ks[21], (DEPTH, GROUP_W), f32, 0.9, 0.999)
    a_lam = u_lam ** (1.0 / LRU_C)
    return {
        'x': jax.random.normal(ks[0], (BATCH, SEQ, D_MODEL), f32),
        'ln_g': gain(ks[1], (DEPTH, D_MODEL)),
        'w_in': nrm(ks[2], (DEPTH, D_MODEL, D_IN), D_MODEL),
        'rwkv_mu': jax.random.uniform(ks[3], (DEPTH, RWKV_IN), f32),
        'rwkv_w0': jax.random.uniform(ks[4], (DEPTH, GROUP_W), f32, -6.0, 1.0),
        'rwkv_w2': nrm(ks[5], (DEPTH, DECAY_LORA, GROUP_W), DECAY_LORA),
        'rwkv_a0': 0.1 * jax.random.normal(ks[6], (DEPTH, GROUP_W), f32),
        'rwkv_a2': nrm(ks[7], (DEPTH, ICL_LORA, GROUP_W), ICL_LORA),
        'rwkv_k_k': 0.85 + 0.05 * jax.random.normal(ks[8], (DEPTH, GROUP_W), f32),
        'rwkv_k_a': 1.0 + 0.05 * jax.random.normal(ks[9], (DEPTH, GROUP_W), f32),
        'rwkv_r_k': 0.1 * jax.random.normal(ks[10], (DEPTH, N_HEADS, HEAD_DIM), f32),
        'rwkv_lnx_g': gain(ks[11], (DEPTH, GROUP_W)),
        'rwkv_lnx_b': small(ks[12], (DEPTH, GROUP_W)),
        'lru_conv_w': nrm(ks[13], (DEPTH, CONV_W, GROUP_W), CONV_W),
        'lru_conv_b': small(ks[14], (DEPTH, GROUP_W)),
        'lru_ga_w': nrm(ks[15], (DEPTH, N_HEADS, HEAD_DIM, HEAD_DIM), HEAD_DIM),
        'lru_ga_b': small(ks[16], (DEPTH, GROUP_W)),
        'lru_gx_w': nrm(ks[17], (DEPTH, N_HEADS, HEAD_DIM, HEAD_DIM), HEAD_DIM),
        'lru_gx_b': small(ks[18], (DEPTH, GROUP_W)),
        'lru_lam': jnp.log(a_lam) - jnp.log1p(-a_lam),
        'lru_out_g': gain(ks[19], (DEPTH, GROUP_W)),
        'mla_q_norm_g': gain(ks[20], (DEPTH, Q_LORA)),
        'mla_w_uq': nrm(ks[22], (DEPTH, Q_LORA, N_HEADS * (QK_NOPE + QK_ROPE)), Q_LORA),
        'mla_kv_norm_g': gain(ks[23], (DEPTH, KV_LORA)),
        'mla_w_ukv': nrm(ks[24], (DEPTH, KV_LORA, N_HEADS * (QK_NOPE + V_DIM)), KV_LORA),
        'mla_out_g': gain(ks[25], (DEPTH, GROUP_W)),
        'w_out': nrm(ks[26], (DEPTH, D_MIX, D_MODEL), D_MIX),
        'final_g': gain(ks[27], (D_MODEL,)),
    }


def reference(x, ln_g, w_in, rwkv_mu, rwkv_w0, rwkv_w2, rwkv_a0, rwkv_a2, rwkv_k_k, rwkv_k_a, rwkv_r_k,
              rwkv_lnx_g, rwkv_lnx_b, lru_conv_w, lru_conv_b, lru_ga_w, lru_ga_b, lru_gx_w, lru_gx_b, lru_lam,
              lru_out_g, mla_q_norm_g, mla_w_uq, mla_kv_norm_g, mla_w_ukv, mla_out_g, w_out, final_g):
    T = x.shape[1]
    half = QK_ROPE // 2
    inv_freq = ROPE_BASE ** (-jnp.arange(half, dtype=jnp.float32) * 2.0 / QK_ROPE)
    ang = jnp.arange(T, dtype=jnp.float32)[:, None] * inv_freq[None, :]
    cos, sin = jnp.cos(ang), jnp.sin(ang)
    pts = _split_points(SPLITS)
    h = x
    for l in range(DEPTH):
        xn = rms_norm(h, ln_g[l])
        proj = xn @ w_in[l]
        u_a, z_a, u_b, z_b, q_lat, kv_lat, k_rope, z_c = jnp.split(proj, pts, axis=-1)
        y_a = rwkv7_mix(u_a, rwkv_mu[l], rwkv_w0[l], rwkv_w2[l], rwkv_a0[l], rwkv_a2[l], rwkv_k_k[l],
                        rwkv_k_a[l], rwkv_r_k[l], rwkv_lnx_g[l], rwkv_lnx_b[l]).astype(h.dtype)
        y_b = rms_norm(rglru_mix(u_b, lru_conv_w[l], lru_conv_b[l], lru_ga_w[l], lru_ga_b[l], lru_gx_w[l],
                                 lru_gx_b[l], lru_lam[l]), lru_out_g[l]).astype(h.dtype)
        y_c = rms_norm(mla_mix(q_lat, kv_lat, k_rope, mla_q_norm_g[l], mla_w_uq[l], mla_kv_norm_g[l],
                               mla_w_ukv[l], cos, sin), mla_out_g[l]).astype(h.dtype)
        y = jnp.concatenate([y_a * jax.nn.silu(z_a), y_b * jax.nn.silu(z_b), y_c * jax.nn.silu(z_c)], axis=-1)
        h = h + y @ w_out[l]
    return rms_norm(h, final_g)
```

```python
import functools

import jax
import jax.numpy as jnp
from jax import lax
from jax.experimental import pallas as pl
from jax.experimental.pallas import tpu as pltpu

F32 = jnp.float32
BF16 = jnp.bfloat16

D_MODEL = 1024
DEPTH = 4
GROUP_W = 512
HEAD_DIM = 64
N_HEADS = 8
NORM_EPS = 1e-6
LORA = 64
GN_EPS = 64e-5
CONV_W = 4
LRU_C = 8.0
Q_LORA = 256
KV_LORA = 128
QK_NOPE = 64
QK_ROPE = 32
V_DIM = 64
ROPE_BASE = 10000.0

LANE = 128
HEAD_PAD = 128
RW_GROUP = 256
RW_CHUNK = 128
COL_R, COL_K, COL_V, COL_ZA, COL_UB, COL_ZB, COL_ZC = (i * GROUP_W for i in range(7))
COL_QLAT = 7 * GROUP_W
COL_KVLAT = COL_QLAT + Q_LORA
COL_KR = COL_KVLAT + KV_LORA
COL_WA = COL_KR + LANE
D_PROJ = COL_WA + LANE

VMEM_LIMIT = 56 * 1024 * 1024


def _cparams(sem):
    return pltpu.CompilerParams(dimension_semantics=sem, vmem_limit_bytes=VMEM_LIMIT)


def _bdot(a, b):
    return jnp.dot(a.astype(BF16), b.astype(BF16), preferred_element_type=F32)


def _bdot_nt(a, b):
    return lax.dot_general(a.astype(BF16), b.astype(BF16), (((1,), (1,)), ((), ())),
                           preferred_element_type=F32)


def _split_dot(x, m, passes):
    acc = None
    rem = x
    for _ in range(passes):
        part = rem.astype(BF16)
        term = jnp.dot(part, m, preferred_element_type=F32)
        acc = term if acc is None else acc + term
        rem = rem - part.astype(F32)
    return acc


def _silu(z):
    return z * jax.nn.sigmoid(z)


def _inproj_kernel(x_ref, g_ref, w_ref, o_ref):
    x = x_ref[...]
    ms = jnp.mean(x * x, axis=-1, keepdims=True)
    xn = x * lax.rsqrt(ms + NORM_EPS) * g_ref[...]
    o_ref[...] = _bdot(xn, w_ref[...])


def _inproj(h2d, g, w, tm):
    m = h2d.shape[0]
    return pl.pallas_call(
        _inproj_kernel,
        grid=(m // tm,),
        in_specs=[pl.BlockSpec((tm, D_MODEL), lambda i: (i, 0)),
                  pl.BlockSpec((1, D_MODEL), lambda i: (0, 0)),
                  pl.BlockSpec((D_MODEL, D_PROJ), lambda i: (0, 0))],
        out_specs=pl.BlockSpec((tm, D_PROJ), lambda i: (i, 0)),
        out_shape=jax.ShapeDtypeStruct((m, D_PROJ), F32),
        compiler_params=_cparams(("parallel",)),
        name="inproj",
    )(h2d, g, w)


def _tri_inverse(a_strict, eye):
    n = -a_strict
    inv = eye + n
    npow = n
    size = a_strict.shape[0]
    k = 2
    while k < size:
        npow = _bdot(npow, npow)
        inv = inv + _bdot(inv, npow)
        k *= 2
    return inv


def _rwkv_kernel(r_ref, k_ref, v_ref, wa_ref, z_ref,
                 mur_ref, muk_ref, muv_ref, muwa_ref,
                 w0_ref, w2_ref, a0_ref, a2_ref, kk_ref, ka_ref, rk_ref, lng_ref, lnb_ref,
                 seg_ref, tri_ref,
                 o_ref,
                 state_ref, pr_ref, pk_ref, pv_ref, pwa_ref):
    c = pl.program_id(2)
    C = r_ref.shape[0]
    W = r_ref.shape[1]

    @pl.when(c == 0)
    def _():
        state_ref[...] = jnp.zeros_like(state_ref)
        pr_ref[...] = jnp.zeros_like(pr_ref)
        pk_ref[...] = jnp.zeros_like(pk_ref)
        pv_ref[...] = jnp.zeros_like(pv_ref)
        pwa_ref[...] = jnp.zeros_like(pwa_ref)

    def lerp_shift(x_ref, prev_ref, mu_ref):
        x = x_ref[...]
        row = lax.broadcasted_iota(jnp.int32, x.shape, 0)
        xs = jnp.where(row == 0, prev_ref[...], pltpu.roll(x, 1, 0))
        prev_ref[...] = x[C - 1:C, :]
        return x + mu_ref[...] * (xs - x)

    r = lerp_shift(r_ref, pr_ref, mur_ref)
    k = lerp_shift(k_ref, pk_ref, muk_ref)
    v = lerp_shift(v_ref, pv_ref, muv_ref)
    wa = lerp_shift(wa_ref, pwa_ref, muwa_ref)

    log_w = -jax.nn.softplus(-(w0_ref[...] + _bdot(jnp.tanh(wa), w2_ref[...]))) - 0.5
    ld = -jnp.exp(log_w)
    a = jax.nn.sigmoid(a0_ref[...] + _bdot(wa, a2_ref[...]))
    seg = seg_ref[...]
    kk = k * kk_ref[...]
    kk = kk / jnp.maximum(jnp.sqrt(_split_dot(kk * kk, seg, 3)), 1e-12)
    k = k * (1.0 + (a - 1.0) * ka_ref[...])
    bb = kk * a

    g = _split_dot_left(tri_ref[...], ld)
    g_last = g[C - 1:C, :]
    r_t = r * jnp.exp(g)
    kk_t = kk * jnp.exp(g - ld)
    inv_g = jnp.exp(-g)
    k_h = k * inv_g
    b_h = bb * inv_g
    to_end = jnp.exp(g_last - g)
    k_e = k * to_end
    b_e = bb * to_end

    s_old = state_ref[...]
    lhs = jnp.concatenate([r_t, kk_t], axis=0).astype(BF16)
    p = lax.dot_general(lhs, s_old.astype(BF16), (((1,), (1,)), ((), ())),
                        preferred_element_type=F32)
    rhs_full = jnp.concatenate([k_h, b_h], axis=0)
    vb = v.astype(BF16)

    lane2 = lax.broadcasted_iota(jnp.int32, (2 * C, W), 1)
    lane1 = lax.broadcasted_iota(jnp.int32, (C, W), 1)
    ti = lax.broadcasted_iota(jnp.int32, (C, C), 0)
    si = lax.broadcasted_iota(jnp.int32, (C, C), 1)
    incl = si <= ti
    strict = si < ti
    eye = (si == ti).astype(F32)

    y = jnp.zeros((C, W), F32)
    u_all = jnp.zeros((C, W), F32)
    for hd in range(W // HEAD_DIM):
        lo, hi = hd * HEAD_DIM, (hd + 1) * HEAD_DIM
        rhs_h = jnp.where((lane2 >= lo) & (lane2 < hi), rhs_full, 0.0)
        amat = _bdot_nt(lhs, rhs_h)
        a_rk = jnp.where(incl, amat[0:C, 0:C], 0.0)
        a_rb = jnp.where(incl, amat[0:C, C:2 * C], 0.0)
        a_ak = jnp.where(strict, amat[C:2 * C, 0:C], 0.0)
        a_ab = jnp.where(strict, amat[C:2 * C, C:2 * C], 0.0)
        x = p[C:2 * C, :] + _bdot(a_ak, vb)
        u_h = _bdot(_tri_inverse(a_ab, eye), x)
        in_head = (lane1 >= lo) & (lane1 < hi)
        u_all = jnp.where(in_head, u_h, u_all)
        vu = jnp.concatenate([vb, u_h.astype(BF16)], axis=0)
        y_h = _bdot(jnp.concatenate([a_rk, -a_rb], axis=1), vu)
        y = jnp.where(in_head, y_h, y)
    y = y + p[0:C, :]

    vu_t = jnp.concatenate([v, u_all], axis=0).T
    kb_e = jnp.concatenate([k_e, -b_e], axis=0)
    s_new = s_old * jnp.exp(g_last) + _bdot(vu_t, kb_e)
    state_ref[...] = jnp.where(seg > 0.5, s_new, 0.0)

    inv_n = 1.0 / HEAD_DIM
    mean = _split_dot(y, seg, 2) * inv_n
    yc = y - mean
    var = _split_dot(yc * yc, seg, 2) * inv_n
    yn = yc * lax.rsqrt(var + GN_EPS) * lng_ref[...] + lnb_ref[...]
    bonus = _split_dot(r * k * rk_ref[...], seg, 2) * v
    o_ref[...] = ((yn + bonus) * _silu(z_ref[...])).astype(o_ref.dtype)


def _split_dot_left(m, x):
    acc = None
    rem = x
    for _ in range(3):
        part = rem.astype(BF16)
        term = jnp.dot(m, part, preferred_element_type=F32)
        acc = term if acc is None else acc + term
        rem = rem - part.astype(F32)
    return acc


def _rwkv(proj, prm, batch, seq):
    C = RW_CHUNK
    W = RW_GROUP
    ng = GROUP_W // W
    nblk = GROUP_W // W

    def col(base):
        return pl.BlockSpec((None, C, W), lambda b, g, c, base=base: (b, c, base // W + g))

    def vec(width=W):
        return pl.BlockSpec((1, width), lambda b, g, c: (0, g))

    def whole(shape):
        return pl.BlockSpec(shape, lambda b, g, c: tuple(0 for _ in shape))

    in_specs = [col(COL_R), col(COL_K), col(COL_V),
                pl.BlockSpec((None, C, LANE), lambda b, g, c: (b, c, COL_WA // LANE)),
                col(COL_ZA),
                vec(), vec(), vec(), whole((1, LANE)),
                vec(), pl.BlockSpec((LANE, W), lambda b, g, c: (0, g)),
                vec(), pl.BlockSpec((LANE, W), lambda b, g, c: (0, g)),
                vec(), vec(), vec(), vec(), vec(),
                whole((W, W)), whole((C, C))]
    del nblk
    return pl.pallas_call(
        _rwkv_kernel,
        grid=(batch, ng, seq // C),
        in_specs=in_specs,
        out_specs=pl.BlockSpec((None, C, W), lambda b, g, c: (b, c, g)),
        out_shape=jax.ShapeDtypeStruct((batch, seq, GROUP_W), BF16),
        scratch_shapes=[pltpu.VMEM((W, W), F32), pltpu.VMEM((1, W), F32), pltpu.VMEM((1, W), F32),
                        pltpu.VMEM((1, W), F32), pltpu.VMEM((1, LANE), F32)],
        compiler_params=_cparams(("parallel", "parallel", "arbitrary")),
        name="rwkv7",
    )(proj, proj, proj, proj, proj,
      prm["mu_r"], prm["mu_k"], prm["mu_v"], prm["mu_wa"],
      prm["w0"], prm["w2p"], prm["a0"], prm["a2p"], prm["k_k"], prm["k_a"], prm["r_k"],
      prm["lnx_g"], prm["lnx_b"], prm["seg"], prm["tri"])


def _shift_rows(x, d, fill):
    row = lax.broadcasted_iota(jnp.int32, x.shape, 0)
    return jnp.where(row >= d, pltpu.roll(x, d, 0), fill)


def _lru_kernel(u_ref, z_ref, cw_ref, cb_ref, gaw_ref, gab_ref, gxw_ref, gxb_ref, lam_ref, og_ref,
                o_ref, tail_ref, hprev_ref):
    c = pl.program_id(1)
    ct = u_ref.shape[0]

    @pl.when(c == 0)
    def _():
        tail_ref[...] = jnp.zeros_like(tail_ref)
        hprev_ref[...] = jnp.zeros_like(hprev_ref)

    u = u_ref[...]
    tail = tail_ref[...]
    row8 = lax.broadcasted_iota(jnp.int32, tail.shape, 0)
    xc = cb_ref[...] + u * cw_ref[CONV_W - 1:CONV_W, :]
    for d in range(1, CONV_W):
        rolled = pltpu.roll(u, d, 0)
        first = jnp.where(row8 < d, pltpu.roll(tail, d, 0), rolled[0:8, :])
        ud = jnp.concatenate([first, rolled[8:, :]], axis=0)
        xc = xc + ud * cw_ref[CONV_W - 1 - d:CONV_W - d, :]
    tail_ref[...] = u[ct - 8:ct, :]

    gate_r = jax.nn.sigmoid(_bdot(xc, gaw_ref[...]) + gab_ref[...])
    gate_i = jax.nn.sigmoid(_bdot(xc, gxw_ref[...]) + gxb_ref[...])
    log_a = -LRU_C * gate_r * jax.nn.softplus(-lam_ref[...])
    acc_a = jnp.exp(log_a)
    th = jnp.tanh(log_a)
    acc_b = jnp.sqrt(-2.0 * th / (1.0 - th)) * (gate_i * xc)
    d = 1
    while d < ct:
        acc_b = acc_a * _shift_rows(acc_b, d, 0.0) + acc_b
        acc_a = acc_a * _shift_rows(acc_a, d, 1.0)
        d *= 2
    hs = acc_a * hprev_ref[...] + acc_b
    hprev_ref[...] = hs[ct - 1:ct, :]
    ms = jnp.mean(hs * hs, axis=-1, keepdims=True)
    yb = hs * lax.rsqrt(ms + NORM_EPS) * og_ref[...]
    o_ref[...] = (yb * _silu(z_ref[...])).astype(o_ref.dtype)


def _lru(proj, prm, batch, seq, ct):
    def vec():
        return pl.BlockSpec((1, GROUP_W), lambda b, c: (0, 0))

    def mat(rows):
        return pl.BlockSpec((rows, GROUP_W), lambda b, c: (0, 0))

    return pl.pallas_call(
        _lru_kernel,
        grid=(batch, seq // ct),
        in_specs=[pl.BlockSpec((None, ct, GROUP_W), lambda b, c: (b, c, COL_UB // GROUP_W)),
                  pl.BlockSpec((None, ct, GROUP_W), lambda b, c: (b, c, COL_ZB // GROUP_W)),
                  mat(CONV_W), vec(), mat(GROUP_W), vec(), mat(GROUP_W), vec(), vec(), vec()],
        out_specs=pl.BlockSpec((None, ct, GROUP_W), lambda b, c: (b, c, 0)),
        out_shape=jax.ShapeDtypeStruct((batch, seq, GROUP_W), BF16),
        scratch_shapes=[pltpu.VMEM((8, GROUP_W), F32), pltpu.VMEM((1, GROUP_W), F32)],
        compiler_params=_cparams(("parallel", "arbitrary")),
        name="rglru",
    )(proj, proj, prm["conv_w"], prm["conv_b"], prm["ga_w"], prm["ga_b"], prm["gx_w"], prm["gx_b"],
      prm["lam"], prm["lru_out_g"])


def _mla_pro_kernel(ql_ref, kvl_ref, kr_ref, qg_ref, kvg_ref, wq_ref, wqs_ref, wk_ref, wv_ref, pk_ref,
                    cq_ref, sq_ref, ck_ref, sk_ref, q_ref, k_ref, v_ref):
    ql = ql_ref[...]
    qn = ql * lax.rsqrt(jnp.mean(ql * ql, axis=-1, keepdims=True) + NORM_EPS) * qg_ref[...]
    qn = qn.astype(BF16)
    cq = jnp.concatenate([cq_ref[...]] * N_HEADS, axis=1)
    sq = jnp.concatenate([sq_ref[...]] * N_HEADS, axis=1)
    q = jnp.dot(qn, wq_ref[...], preferred_element_type=F32) * cq
    q = q + jnp.dot(qn, wqs_ref[...], preferred_element_type=F32) * sq
    q_ref[...] = q.astype(q_ref.dtype)

    kvl = kvl_ref[...]
    kvn = kvl * lax.rsqrt(jnp.mean(kvl * kvl, axis=-1, keepdims=True) + NORM_EPS) * kvg_ref[...]
    kvn = kvn.astype(BF16)
    kr = kr_ref[...]
    roped = kr * ck_ref[...] + pltpu.roll(kr * sk_ref[...], LANE - QK_ROPE, 1)
    kfull = jnp.dot(kvn, wk_ref[...], preferred_element_type=F32) + _bdot(roped, pk_ref[...])
    k_ref[...] = kfull.astype(k_ref.dtype)
    v_ref[...] = jnp.dot(kvn, wv_ref[...], preferred_element_type=F32).astype(v_ref.dtype)


def _mla_pro(proj, prm, rope, batch, seq, tm):
    qk_w = N_HEADS * HEAD_PAD

    def whole(shape):
        return pl.BlockSpec(shape, lambda b, t: tuple(0 for _ in shape))

    def tab():
        return pl.BlockSpec((tm, LANE), lambda b, t: (t, 0))

    return pl.pallas_call(
        _mla_pro_kernel,
        grid=(batch, seq // tm),
        in_specs=[pl.BlockSpec((None, tm, Q_LORA), lambda b, t: (b, t, COL_QLAT // Q_LORA)),
                  pl.BlockSpec((None, tm, KV_LORA), lambda b, t: (b, t, COL_KVLAT // KV_LORA)),
                  pl.BlockSpec((None, tm, LANE), lambda b, t: (b, t, COL_KR // LANE)),
                  whole((1, Q_LORA)), whole((1, KV_LORA)),
                  whole((Q_LORA, qk_w)), whole((Q_LORA, qk_w)), whole((KV_LORA, qk_w)),
                  whole((KV_LORA, GROUP_W)), whole((LANE, qk_w)),
                  tab(), tab(), tab(), tab()],
        out_specs=[pl.BlockSpec((None, tm, qk_w), lambda b, t: (b, t, 0)),
                   pl.BlockSpec((None, tm, qk_w), lambda b, t: (b, t, 0)),
                   pl.BlockSpec((None, tm, GROUP_W), lambda b, t: (b, t, 0))],
        out_shape=[jax.ShapeDtypeStruct((batch, seq, qk_w), BF16),
                   jax.ShapeDtypeStruct((batch, seq, qk_w), BF16),
                   jax.ShapeDtypeStruct((batch, seq, GROUP_W), BF16)],
        compiler_params=_cparams(("parallel", "parallel")),
        name="mla_pro",
    )(proj, proj, proj, prm["q_norm_g"], prm["kv_norm_g"], prm["wq"], prm["wq_sw"], prm["wk"], prm["wv"],
      prm["k_place"], rope["cq"], rope["sq"], rope["ck"], rope["sk"])


def _attn_kernel(q_ref, k_ref, v_ref, o_ref, m_ref, l_ref, acc_ref, *, blk):
    qi = pl.program_id(2)
    ki = pl.program_id(3)
    nk = pl.num_programs(3)

    @pl.when(ki == 0)
    def _():
        m_ref[...] = jnp.full_like(m_ref, -jnp.inf)
        l_ref[...] = jnp.zeros_like(l_ref)
        acc_ref[...] = jnp.zeros_like(acc_ref)

    def step(masked):
        v = v_ref[...]
        lane = lax.broadcasted_iota(jnp.int32, (blk, LANE), 1)
        pvs, alphas = [], []
        for hh in range(2):
            q = q_ref[:, hh * HEAD_PAD:(hh + 1) * HEAD_PAD]
            k = k_ref[:, hh * HEAD_PAD:(hh + 1) * HEAD_PAD]
            s = lax.dot_general(q, k, (((1,), (1,)), ((), ())), preferred_element_type=F32)
            if masked:
                rows = lax.broadcasted_iota(jnp.int32, (blk, blk), 0)
                cols = lax.broadcasted_iota(jnp.int32, (blk, blk), 1)
                s = jnp.where(cols <= rows, s, -jnp.inf)
            m_prev = m_ref[hh]
            m_new = jnp.maximum(m_prev, jnp.max(s, axis=-1, keepdims=True))
            alpha = jnp.exp(m_prev - m_new)
            p = jnp.exp(s - m_new[:, 0:1])
            l_ref[hh] = alpha * l_ref[hh] + jnp.sum(p, axis=-1, keepdims=True)
            m_ref[hh] = m_new
            pvs.append(jnp.dot(p.astype(BF16), v, preferred_element_type=F32))
            alphas.append(alpha)
        first = lane < V_DIM
        acc_ref[...] = (jnp.where(first, alphas[0], alphas[1]) * acc_ref[...]
                        + jnp.where(first, pvs[0], pvs[1]))

    @pl.when(ki < qi)
    def _():
        step(False)

    @pl.when(ki == qi)
    def _():
        step(True)

    @pl.when(ki == nk - 1)
    def _():
        lane = lax.broadcasted_iota(jnp.int32, (blk, LANE), 1)
        l = jnp.where(lane < V_DIM, l_ref[0], l_ref[1])
        o_ref[...] = acc_ref[...] / l


def _attn(q, k, v, batch, seq, blk):
    nb = seq // blk
    pair_w = 2 * HEAD_PAD
    return pl.pallas_call(
        functools.partial(_attn_kernel, blk=blk),
        grid=(batch, N_HEADS // 2, nb, nb),
        in_specs=[pl.BlockSpec((None, blk, pair_w), lambda b, h, i, j: (b, i, h)),
                  pl.BlockSpec((None, blk, pair_w), lambda b, h, i, j: (b, jnp.minimum(j, i), h)),
                  pl.BlockSpec((None, blk, LANE), lambda b, h, i, j: (b, jnp.minimum(j, i), h))],
        out_specs=pl.BlockSpec((None, blk, LANE), lambda b, h, i, j: (b, i, h)),
        out_shape=jax.ShapeDtypeStruct((batch, seq, GROUP_W), F32),
        scratch_shapes=[pltpu.VMEM((2, blk, LANE), F32), pltpu.VMEM((2, blk, LANE), F32),
                        pltpu.VMEM((blk, LANE), F32)],
        compiler_params=_cparams(("parallel", "parallel", "parallel", "arbitrary")),
        name="mla_attn",
    )(q, k, v)


def _outproj_kernel(ya_ref, yb_ref, oc_ref, zc_ref, gc_ref, w_ref, h_ref, fg_ref, o_ref, *, final):
    oc = oc_ref[...]
    yc = oc * lax.rsqrt(jnp.mean(oc * oc, axis=-1, keepdims=True) + NORM_EPS) * gc_ref[...]
    yc = yc * _silu(zc_ref[...])
    acc = h_ref[...]
    acc = acc + jnp.dot(ya_ref[...], w_ref[0:GROUP_W, :], preferred_element_type=F32)
    acc = acc + jnp.dot(yb_ref[...], w_ref[GROUP_W:2 * GROUP_W, :], preferred_element_type=F32)
    acc = acc + _bdot(yc, w_ref[2 * GROUP_W:3 * GROUP_W, :])
    if final:
        acc = acc * lax.rsqrt(jnp.mean(acc * acc, axis=-1, keepdims=True) + NORM_EPS) * fg_ref[...]
    o_ref[...] = acc


def _outproj(ya, yb, oc, proj, h2d, mla_out_g, w_out, final_g, final, tm):
    m = h2d.shape[0]

    def rows(width, blk=0):
        return pl.BlockSpec((tm, width), lambda i, blk=blk: (i, blk))

    return pl.pallas_call(
        functools.partial(_outproj_kernel, final=final),
        grid=(m // tm,),
        in_specs=[rows(GROUP_W), rows(GROUP_W), rows(GROUP_W), rows(GROUP_W, COL_ZC // GROUP_W),
                  pl.BlockSpec((1, GROUP_W), lambda i: (0, 0)),
                  pl.BlockSpec((3 * GROUP_W, D_MODEL), lambda i: (0, 0)),
                  rows(D_MODEL),
                  pl.BlockSpec((1, D_MODEL), lambda i: (0, 0))],
        out_specs=rows(D_MODEL),
        out_shape=jax.ShapeDtypeStruct((m, D_MODEL), F32),
        compiler_params=_cparams(("parallel",)),
        name="outproj_final" if final else "outproj",
    )(ya, yb, oc, proj, mla_out_g, w_out, h2d, final_g)


def _rope_tables(seq):
    half = QK_ROPE // 2
    inv_freq = ROPE_BASE ** (-jnp.arange(half, dtype=F32) * 2.0 / QK_ROPE)
    ang = jnp.arange(seq, dtype=F32)[:, None] * inv_freq[None, :]
    cos2 = jnp.concatenate([jnp.cos(ang)] * 2, axis=1)
    sin2 = jnp.concatenate([jnp.sin(ang)] * 2, axis=1)
    zeros = lambda w: jnp.zeros((seq, w), F32)
    ones = jnp.ones((seq, QK_NOPE), F32)
    return {
        "cq": jnp.concatenate([ones, cos2, zeros(HEAD_PAD - QK_NOPE - QK_ROPE)], axis=1),
        "sq": jnp.concatenate([zeros(QK_NOPE), sin2, zeros(HEAD_PAD - QK_NOPE - QK_ROPE)], axis=1),
        "ck": jnp.concatenate([cos2, zeros(LANE - QK_ROPE)], axis=1),
        "sk": jnp.concatenate([zeros(QK_ROPE), sin2, zeros(LANE - 2 * QK_ROPE)], axis=1),
    }


def _swap_halves(w):
    half = w.shape[-1] // 2
    return jnp.concatenate([-w[..., half:], w[..., :half]], axis=-1)


def _block_diag(w):
    h, n, _ = w.shape
    eye = jnp.eye(h, dtype=w.dtype)
    return (eye[:, None, :, None] * w[:, :, None, :]).reshape(h * n, h * n)


def _layer_params(l, ln_g, w_in, rwkv_mu, rwkv_w0, rwkv_w2, rwkv_a0, rwkv_a2, rwkv_k_k, rwkv_k_a, rwkv_r_k,
                  rwkv_lnx_g, rwkv_lnx_b, lru_conv_w, lru_conv_b, lru_ga_w, lru_ga_b, lru_gx_w, lru_gx_b,
                  lru_lam, lru_out_g, mla_q_norm_g, mla_w_uq, mla_kv_norm_g, mla_w_ukv, mla_out_g, w_out):
    G = GROUP_W
    w = w_in[l]
    o_ua, o_za = 0, 3 * G + 2 * LORA
    o_ub = o_za + G
    o_zb = o_ub + G
    o_ql = o_zb + G
    o_kv = o_ql + Q_LORA
    o_kr = o_kv + KV_LORA
    o_zc = o_kr + QK_ROPE
    sl = lambda a, n: w[:, a:a + n]
    kr_w = sl(o_kr, QK_ROPE)
    w_re = jnp.concatenate([
        sl(o_ua, G), sl(o_ua + G, G), sl(o_ua + 2 * G, G), sl(o_za, G), sl(o_ub, G), sl(o_zb, G), sl(o_zc, G),
        sl(o_ql, Q_LORA), sl(o_kv, KV_LORA),
        kr_w, _swap_halves(kr_w), jnp.zeros((D_MODEL, LANE - 2 * QK_ROPE), F32),
        sl(o_ua + 3 * G, 2 * LORA)], axis=1).astype(BF16)
    row = lambda v: v.reshape(1, -1)
    mu = rwkv_mu[l]
    zl = jnp.zeros((LORA, G), F32)
    head_id = jnp.arange(RW_GROUP) // HEAD_DIM
    t_idx = jnp.arange(RW_CHUNK)
    scale = (QK_NOPE + QK_ROPE) ** -0.5
    wq3 = mla_w_uq[l].reshape(Q_LORA, N_HEADS, QK_NOPE + QK_ROPE) * scale
    zq = lambda n: jnp.zeros((Q_LORA, N_HEADS, n), F32)
    wq = jnp.concatenate([wq3, zq(HEAD_PAD - QK_NOPE - QK_ROPE)], axis=2)
    wq_sw = jnp.concatenate([zq(QK_NOPE), _swap_halves(wq3[:, :, QK_NOPE:]),
                             zq(HEAD_PAD - QK_NOPE - QK_ROPE)], axis=2)
    wkv3 = mla_w_ukv[l].reshape(KV_LORA, N_HEADS, QK_NOPE + V_DIM)
    wk = jnp.concatenate([wkv3[:, :, :QK_NOPE], jnp.zeros((KV_LORA, N_HEADS, HEAD_PAD - QK_NOPE), F32)], axis=2)
    place = jnp.zeros((LANE, N_HEADS, HEAD_PAD), F32)
    place = place.at[jnp.arange(QK_ROPE), :, QK_NOPE + jnp.arange(QK_ROPE)].set(1.0)
    return {
        "ln_g": row(ln_g[l]), "w_in": w_re,
        "mu_r": row(mu[0:G]), "mu_k": row(mu[G:2 * G]), "mu_v": row(mu[2 * G:3 * G]),
        "mu_wa": row(mu[3 * G:3 * G + 2 * LORA]),
        "w0": row(rwkv_w0[l]), "w2p": jnp.concatenate([rwkv_w2[l], zl], axis=0).astype(BF16),
        "a0": row(rwkv_a0[l]), "a2p": jnp.concatenate([zl, rwkv_a2[l]], axis=0).astype(BF16),
        "k_k": row(rwkv_k_k[l]), "k_a": row(rwkv_k_a[l]), "r_k": row(rwkv_r_k[l]),
        "lnx_g": row(rwkv_lnx_g[l]), "lnx_b": row(rwkv_lnx_b[l]),
        "seg": (head_id[:, None] == head_id[None, :]).astype(BF16),
        "tri": (t_idx[None, :] <= t_idx[:, None]).astype(BF16),
        "conv_w": lru_conv_w[l], "conv_b": row(lru_conv_b[l]),
        "ga_w": _block_diag(lru_ga_w[l]).astype(BF16), "ga_b": row(lru_ga_b[l]),
        "gx_w": _block_diag(lru_gx_w[l]).astype(BF16), "gx_b": row(lru_gx_b[l]),
        "lam": row(lru_lam[l]), "lru_out_g": row(lru_out_g[l]),
        "q_norm_g": row(mla_q_norm_g[l]), "kv_norm_g": row(mla_kv_norm_g[l]),
        "wq": wq.reshape(Q_LORA, -1).astype(BF16), "wq_sw": wq_sw.reshape(Q_LORA, -1).astype(BF16),
        "wk": wk.reshape(KV_LORA, -1).astype(BF16),
        "wv": wkv3[:, :, QK_NOPE:].reshape(KV_LORA, -1).astype(BF16),
        "k_place": place.reshape(LANE, -1).astype(BF16),
        "mla_out_g": row(mla_out_g[l]), "w_out": w_out[l].astype(BF16),
    }


def kernel(x, ln_g, w_in, rwkv_mu, rwkv_w0, rwkv_w2, rwkv_a0, rwkv_a2, rwkv_k_k, rwkv_k_a, rwkv_r_k,
           rwkv_lnx_g, rwkv_lnx_b, lru_conv_w, lru_conv_b, lru_ga_w, lru_ga_b, lru_gx_w, lru_gx_b, lru_lam,
           lru_out_g, mla_q_norm_g, mla_w_uq, mla_kv_norm_g, mla_w_ukv, mla_out_g, w_out, final_g):
    batch, seq, _ = x.shape
    tm = min(512, seq)
    rope = _rope_tables(seq)
    fg = final_g.reshape(1, -1)
    h = x.reshape(batch * seq, D_MODEL)
    for l in range(DEPTH):
        prm = _layer_params(l, ln_g, w_in, rwkv_mu, rwkv_w0, rwkv_w2, rwkv_a0, rwkv_a2, rwkv_k_k, rwkv_k_a,
                            rwkv_r_k, rwkv_lnx_g, rwkv_lnx_b, lru_conv_w, lru_conv_b, lru_ga_w, lru_ga_b,
                            lru_gx_w, lru_gx_b, lru_lam, lru_out_g, mla_q_norm_g, mla_w_uq, mla_kv_norm_g,
                            mla_w_ukv, mla_out_g, w_out)
        proj2d = _inproj(h, prm["ln_g"], prm["w_in"], tm)
        proj = proj2d.reshape(batch, seq, D_PROJ)
        ya = _rwkv(proj, prm, batch, seq)
        yb = _lru(proj, prm, batch, seq, min(256, seq))
        q, k, v = _mla_pro(proj, prm, rope, batch, seq, tm)
        oc = _attn(q, k, v, batch, seq, min(512, seq))
        h = _outproj(ya.reshape(batch * seq, GROUP_W), yb.reshape(batch * seq, GROUP_W),
                     oc.reshape(batch * seq, GROUP_W), proj2d, h, prm["mla_out_g"], prm["w_out"], fg,
                     l == DEPTH - 1, tm)
    return h.reshape(batch, seq, D_MODEL)
```

```python
import functools

import jax
import jax.numpy as jnp
from jax import lax
from jax.experimental import pallas as pl
from jax.experimental.pallas import tpu as pltpu

F32 = jnp.float32
BF16 = jnp.bfloat16

D_MODEL = 1024
DEPTH = 4
GROUP_W = 512
HEAD_DIM = 64
N_HEADS = 8
NORM_EPS = 1e-6
LORA = 64
GN_EPS = 64e-5
CONV_W = 4
LRU_C = 8.0
Q_LORA = 256
KV_LORA = 128
QK_NOPE = 64
QK_ROPE = 32
V_DIM = 64
ROPE_BASE = 10000.0

LANE = 128
HEAD_PAD = 128
RW_GROUP = 256
RW_CHUNK = 128
COL_R, COL_K, COL_V, COL_ZA, COL_UB, COL_ZB, COL_ZC = (i * GROUP_W for i in range(7))
COL_QLAT = 7 * GROUP_W
COL_KVLAT = COL_QLAT + Q_LORA
COL_KR = COL_KVLAT + KV_LORA
COL_WA = COL_KR + LANE
D_PROJ = COL_WA + LANE

VMEM_LIMIT = 56 * 1024 * 1024


def _cparams(sem):
    return pltpu.CompilerParams(dimension_semantics=sem, vmem_limit_bytes=VMEM_LIMIT)


def _bdot(a, b):
    return jnp.dot(a.astype(BF16), b.astype(BF16), preferred_element_type=F32)


def _bdot_nt(a, b):
    return lax.dot_general(a.astype(BF16), b.astype(BF16), (((1,), (1,)), ((), ())),
                           preferred_element_type=F32)


def _split_dot(x, m, passes):
    acc = None
    rem = x
    for _ in range(passes):
        part = rem.astype(BF16)
        term = jnp.dot(part, m, preferred_element_type=F32)
        acc = term if acc is None else acc + term
        rem = rem - part.astype(F32)
    return acc


def _silu(z):
    return z * jax.nn.sigmoid(z)


def _inproj_kernel(x_ref, g_ref, w_ref, o_ref):
    x = x_ref[...]
    ms = jnp.mean(x * x, axis=-1, keepdims=True)
    xn = x * lax.rsqrt(ms + NORM_EPS) * g_ref[...]
    o_ref[...] = _bdot(xn, w_ref[...])


def _inproj(h2d, g, w, tm):
    m = h2d.shape[0]
    return pl.pallas_call(
        _inproj_kernel,
        grid=(m // tm,),
        in_specs=[pl.BlockSpec((tm, D_MODEL), lambda i: (i, 0)),
                  pl.BlockSpec((1, D_MODEL), lambda i: (0, 0)),
                  pl.BlockSpec((D_MODEL, D_PROJ), lambda i: (0, 0))],
        out_specs=pl.BlockSpec((tm, D_PROJ), lambda i: (i, 0)),
        out_shape=jax.ShapeDtypeStruct((m, D_PROJ), F32),
        compiler_params=_cparams(("parallel",)),
        name="inproj",
    )(h2d, g, w)


def _split_dot_left(m, x):
    acc = None
    rem = x
    for _ in range(3):
        part = rem.astype(BF16)
        term = jnp.dot(m, part, preferred_element_type=F32)
        acc = term if acc is None else acc + term
        rem = rem - part.astype(F32)
    return acc


def _seg_sum(x, seg, passes):
    parts = [_split_dot(x[:, i:i + RW_GROUP], seg, passes) for i in range(0, x.shape[1], RW_GROUP)]
    return jnp.concatenate(parts, axis=1)


def _rwkv_kernel(r_ref, k_ref, v_ref, wa_ref, z_ref,
                 mur_ref, muk_ref, muv_ref, muwa_ref,
                 w0_ref, w2_ref, a0_ref, a2_ref, kk_ref, ka_ref, rk_ref, lng_ref, lnb_ref,
                 seg_ref, tri_ref,
                 o_ref,
                 state_ref, pr_ref, pk_ref, pv_ref, pwa_ref):
    c = pl.program_id(1)
    C = r_ref.shape[0]
    n_groups = GROUP_W // RW_GROUP
    n_pairs = GROUP_W // LANE

    @pl.when(c == 0)
    def _():
        state_ref[...] = jnp.zeros_like(state_ref)
        pr_ref[...] = jnp.zeros_like(pr_ref)
        pk_ref[...] = jnp.zeros_like(pk_ref)
        pv_ref[...] = jnp.zeros_like(pv_ref)
        pwa_ref[...] = jnp.zeros_like(pwa_ref)

    def lerp_shift(x_ref, prev_ref, mu_ref):
        x = x_ref[...]
        row = lax.broadcasted_iota(jnp.int32, x.shape, 0)
        xs = jnp.where(row == 0, prev_ref[...], pltpu.roll(x, 1, 0))
        prev_ref[...] = x[C - 1:C, :]
        return x + mu_ref[...] * (xs - x)

    r = lerp_shift(r_ref, pr_ref, mur_ref)
    k = lerp_shift(k_ref, pk_ref, muk_ref)
    v = lerp_shift(v_ref, pv_ref, muv_ref)
    wa = lerp_shift(wa_ref, pwa_ref, muwa_ref)

    log_w = -jax.nn.softplus(-(w0_ref[...] + _bdot(jnp.tanh(wa), w2_ref[...]))) - 0.5
    ld = -jnp.exp(log_w)
    a = jax.nn.sigmoid(a0_ref[...] + _bdot(wa, a2_ref[...]))
    seg = seg_ref[...]
    kk = k * kk_ref[...]
    kk = kk / jnp.maximum(jnp.sqrt(_seg_sum(kk * kk, seg, 3)), 1e-12)
    k = k * (1.0 + (a - 1.0) * ka_ref[...])
    bb = kk * a

    g = _split_dot_left(tri_ref[...], ld)
    g_last = g[C - 1:C, :]
    inv_g = jnp.exp(-g)
    to_end = jnp.exp(g_last - g)
    lhs = jnp.concatenate([r * jnp.exp(g), kk * jnp.exp(g - ld)], axis=0).astype(BF16)
    rhs = jnp.concatenate([k * inv_g, bb * inv_g], axis=0)
    kb_end = jnp.concatenate([k * to_end, -(bb * to_end)], axis=0)
    vb = v.astype(BF16)

    def grp(x, i):
        return x[:, i * RW_GROUP:(i + 1) * RW_GROUP]

    def pair(x, i):
        return x[:, i * LANE:(i + 1) * LANE]

    p = [lax.dot_general(grp(lhs, i), state_ref[i].astype(BF16), (((1,), (1,)), ((), ())),
                         preferred_element_type=F32) for i in range(n_groups)]
    p = jnp.concatenate(p, axis=1)

    lane2 = lax.broadcasted_iota(jnp.int32, (2 * C, LANE), 1)
    lane1 = lax.broadcasted_iota(jnp.int32, (C, LANE), 1)
    ti = lax.broadcasted_iota(jnp.int32, (C, C), 0)
    si = lax.broadcasted_iota(jnp.int32, (C, C), 1)
    incl = si <= ti
    strict = si < ti
    heads = range(N_HEADS)
    low = [hd % 2 == 0 for hd in heads]

    def own2(hd, x):
        return jnp.where((lane2 < HEAD_DIM) == low[hd], x, 0.0)

    amat = [_bdot_nt(pair(lhs, hd // 2), own2(hd, pair(rhs, hd // 2))) for hd in heads]
    a_y = [jnp.concatenate([jnp.where(incl, m[0:C, 0:C], 0.0), jnp.where(incl, -m[0:C, C:2 * C], 0.0)],
                           axis=1).astype(BF16) for m in amat]
    xs = [pair(p, hd // 2)[C:2 * C, :]
          + _bdot(jnp.where(strict, amat[hd][C:2 * C, 0:C], 0.0), pair(vb, hd // 2)) for hd in heads]
    npow = [jnp.where(strict, -amat[hd][C:2 * C, C:2 * C], 0.0).astype(BF16) for hd in heads]
    span = 1
    while True:
        xs = [x + _bdot(n, x) for n, x in zip(npow, xs)]
        span *= 2
        if span >= C:
            break
        npow = [_bdot(n, n).astype(BF16) for n in npow]
    y_h = [_bdot(a_y[hd], jnp.concatenate([pair(vb, hd // 2), xs[hd].astype(BF16)], axis=0)) for hd in heads]

    first = lane1 < HEAD_DIM
    y = jnp.concatenate([jnp.where(first, y_h[2 * i], y_h[2 * i + 1]) for i in range(n_pairs)], axis=1)
    u = jnp.concatenate([jnp.where(first, xs[2 * i], xs[2 * i + 1]) for i in range(n_pairs)], axis=1)
    y = y + p[0:C, :]

    vu_t = jnp.concatenate([v, u], axis=0).T
    decay_all = jnp.exp(g_last)
    for i in range(n_groups):
        s_new = state_ref[i] * grp(decay_all, i) + _bdot(vu_t[i * RW_GROUP:(i + 1) * RW_GROUP, :], grp(kb_end, i))
        state_ref[i] = jnp.where(seg > 0.5, s_new, 0.0)

    inv_n = 1.0 / HEAD_DIM
    mean = _seg_sum(y, seg, 2) * inv_n
    yc = y - mean
    var = _seg_sum(yc * yc, seg, 2) * inv_n
    yn = yc * lax.rsqrt(var + GN_EPS) * lng_ref[...] + lnb_ref[...]
    bonus = _seg_sum(r * k * rk_ref[...], seg, 2) * v
    o_ref[...] = ((yn + bonus) * _silu(z_ref[...])).astype(o_ref.dtype)


def _rwkv(proj, prm, batch, seq):
    C = RW_CHUNK

    def col(base):
        return pl.BlockSpec((None, C, GROUP_W), lambda b, c, base=base: (b, c, base // GROUP_W))

    def vec(width=GROUP_W):
        return pl.BlockSpec((1, width), lambda b, c: (0, 0))

    def whole(shape):
        return pl.BlockSpec(shape, lambda b, c: tuple(0 for _ in shape))

    in_specs = [col(COL_R), col(COL_K), col(COL_V),
                pl.BlockSpec((None, C, LANE), lambda b, c: (b, c, COL_WA // LANE)),
                col(COL_ZA),
                vec(), vec(), vec(), vec(LANE),
                vec(), whole((LANE, GROUP_W)), vec(), whole((LANE, GROUP_W)),
                vec(), vec(), vec(), vec(), vec(),
                whole((RW_GROUP, RW_GROUP)), whole((C, C))]
    return pl.pallas_call(
        _rwkv_kernel,
        grid=(batch, seq // C),
        in_specs=in_specs,
        out_specs=pl.BlockSpec((None, C, GROUP_W), lambda b, c: (b, c, 0)),
        out_shape=jax.ShapeDtypeStruct((batch, seq, GROUP_W), BF16),
        scratch_shapes=[pltpu.VMEM((GROUP_W // RW_GROUP, RW_GROUP, RW_GROUP), F32),
                        pltpu.VMEM((1, GROUP_W), F32), pltpu.VMEM((1, GROUP_W), F32),
                        pltpu.VMEM((1, GROUP_W), F32), pltpu.VMEM((1, LANE), F32)],
        compiler_params=_cparams(("parallel", "arbitrary")),
        name="rwkv7",
    )(proj, proj, proj, proj, proj,
      prm["mu_r"], prm["mu_k"], prm["mu_v"], prm["mu_wa"],
      prm["w0"], prm["w2p"], prm["a0"], prm["a2p"], prm["k_k"], prm["k_a"], prm["r_k"],
      prm["lnx_g"], prm["lnx_b"], prm["seg"], prm["tri"])


def _shift_rows(x, d, fill):
    row = lax.broadcasted_iota(jnp.int32, x.shape, 0)
    return jnp.where(row >= d, pltpu.roll(x, d, 0), fill)


def _lru_kernel(u_ref, z_ref, cw_ref, cb_ref, gaw_ref, gab_ref, gxw_ref, gxb_ref, lam_ref, og_ref,
                o_ref, tail_ref, hprev_ref):
    c = pl.program_id(1)
    ct = u_ref.shape[0]

    @pl.when(c == 0)
    def _():
        tail_ref[...] = jnp.zeros_like(tail_ref)
        hprev_ref[...] = jnp.zeros_like(hprev_ref)

    u = u_ref[...]
    tail = tail_ref[...]
    row8 = lax.broadcasted_iota(jnp.int32, tail.shape, 0)
    xc = cb_ref[...] + u * cw_ref[CONV_W - 1:CONV_W, :]
    for d in range(1, CONV_W):
        rolled = pltpu.roll(u, d, 0)
        first = jnp.where(row8 < d, pltpu.roll(tail, d, 0), rolled[0:8, :])
        ud = jnp.concatenate([first, rolled[8:, :]], axis=0)
        xc = xc + ud * cw_ref[CONV_W - 1 - d:CONV_W - d, :]
    tail_ref[...] = u[ct - 8:ct, :]

    gate_r = jax.nn.sigmoid(_bdot(xc, gaw_ref[...]) + gab_ref[...])
    gate_i = jax.nn.sigmoid(_bdot(xc, gxw_ref[...]) + gxb_ref[...])
    log_a = -LRU_C * gate_r * jax.nn.softplus(-lam_ref[...])
    acc_a = jnp.exp(log_a)
    th = jnp.tanh(log_a)
    acc_b = jnp.sqrt(-2.0 * th / (1.0 - th)) * (gate_i * xc)
    d = 1
    while d < ct:
        acc_b = acc_a * _shift_rows(acc_b, d, 0.0) + acc_b
        acc_a = acc_a * _shift_rows(acc_a, d, 1.0)
        d *= 2
    hs = acc_a * hprev_ref[...] + acc_b
    hprev_ref[...] = hs[ct - 1:ct, :]
    ms = jnp.mean(hs * hs, axis=-1, keepdims=True)
    yb = hs * lax.rsqrt(ms + NORM_EPS) * og_ref[...]
    o_ref[...] = (yb * _silu(z_ref[...])).astype(o_ref.dtype)


def _lru(proj, prm, batch, seq, ct):
    def vec():
        return pl.BlockSpec((1, GROUP_W), lambda b, c: (0, 0))

    def mat(rows):
        return pl.BlockSpec((rows, GROUP_W), lambda b, c: (0, 0))

    return pl.pallas_call(
        _lru_kernel,
        grid=(batch, seq // ct),
        in_specs=[pl.BlockSpec((None, ct, GROUP_W), lambda b, c: (b, c, COL_UB // GROUP_W)),
                  pl.BlockSpec((None, ct, GROUP_W), lambda b, c: (b, c, COL_ZB // GROUP_W)),
                  mat(CONV_W), vec(), mat(GROUP_W), vec(), mat(GROUP_W), vec(), vec(), vec()],
        out_specs=pl.BlockSpec((None, ct, GROUP_W), lambda b, c: (b, c, 0)),
        out_shape=jax.ShapeDtypeStruct((batch, seq, GROUP_W), BF16),
        scratch_shapes=[pltpu.VMEM((8, GROUP_W), F32), pltpu.VMEM((1, GROUP_W), F32)],
        compiler_params=_cparams(("parallel", "arbitrary")),
        name="rglru",
    )(proj, proj, prm["conv_w"], prm["conv_b"], prm["ga_w"], prm["ga_b"], prm["gx_w"], prm["gx_b"],
      prm["lam"], prm["lru_out_g"])


def _mla_pro_kernel(ql_ref, kvl_ref, kr_ref, qg_ref, kvg_ref, wq_ref, wqs_ref, wk_ref, wv_ref, pk_ref,
                    cq_ref, sq_ref, ck_ref, sk_ref, q_ref, k_ref, v_ref):
    ql = ql_ref[...]
    qn = ql * lax.rsqrt(jnp.mean(ql * ql, axis=-1, keepdims=True) + NORM_EPS) * qg_ref[...]
    qn = qn.astype(BF16)
    cq = jnp.concatenate([cq_ref[...]] * N_HEADS, axis=1)
    sq = jnp.concatenate([sq_ref[...]] * N_HEADS, axis=1)
    q = jnp.dot(qn, wq_ref[...], preferred_element_type=F32) * cq
    q = q + jnp.dot(qn, wqs_ref[...], preferred_element_type=F32) * sq
    q_ref[...] = q.astype(q_ref.dtype)

    kvl = kvl_ref[...]
    kvn = kvl * lax.rsqrt(jnp.mean(kvl * kvl, axis=-1, keepdims=True) + NORM_EPS) * kvg_ref[...]
    kvn = kvn.astype(BF16)
    kr = kr_ref[...]
    roped = kr * ck_ref[...] + pltpu.roll(kr * sk_ref[...], LANE - QK_ROPE, 1)
    kfull = jnp.dot(kvn, wk_ref[...], preferred_element_type=F32) + _bdot(roped, pk_ref[...])
    k_ref[...] = kfull.astype(k_ref.dtype)
    vt = lax.dot_general(wv_ref[...], kvn, (((1,), (1,)), ((), ())), preferred_element_type=F32)
    v_ref[...] = vt.astype(v_ref.dtype)


def _mla_pro(proj, prm, rope, batch, seq, tm):
    qk_w = N_HEADS * HEAD_PAD

    def whole(shape):
        return pl.BlockSpec(shape, lambda b, t: tuple(0 for _ in shape))

    def tab():
        return pl.BlockSpec((tm, LANE), lambda b, t: (t, 0))

    return pl.pallas_call(
        _mla_pro_kernel,
        grid=(batch, seq // tm),
        in_specs=[pl.BlockSpec((None, tm, Q_LORA), lambda b, t: (b, t, COL_QLAT // Q_LORA)),
                  pl.BlockSpec((None, tm, KV_LORA), lambda b, t: (b, t, COL_KVLAT // KV_LORA)),
                  pl.BlockSpec((None, tm, LANE), lambda b, t: (b, t, COL_KR // LANE)),
                  whole((1, Q_LORA)), whole((1, KV_LORA)),
                  whole((Q_LORA, qk_w)), whole((Q_LORA, qk_w)), whole((KV_LORA, qk_w)),
                  whole((GROUP_W, KV_LORA)), whole((LANE, qk_w)),
                  tab(), tab(), tab(), tab()],
        out_specs=[pl.BlockSpec((None, tm, qk_w), lambda b, t: (b, t, 0)),
                   pl.BlockSpec((None, tm, qk_w), lambda b, t: (b, t, 0)),
                   pl.BlockSpec((None, None, GROUP_W, tm), lambda b, t: (b, t, 0, 0))],
        out_shape=[jax.ShapeDtypeStruct((batch, seq, qk_w), BF16),
                   jax.ShapeDtypeStruct((batch, seq, qk_w), BF16),
                   jax.ShapeDtypeStruct((batch, seq // tm, GROUP_W, tm), BF16)],
        compiler_params=_cparams(("parallel", "parallel")),
        name="mla_pro",
    )(proj, proj, proj, prm["q_norm_g"], prm["kv_norm_g"], prm["wq"], prm["wq_sw"], prm["wk"], prm["wv_t"],
      prm["k_place"], rope["cq"], rope["sq"], rope["ck"], rope["sk"])


def _attn_kernel(q_ref, k_ref, vt_ref, o_ref, m_ref, l_ref, acc_ref, *, blk):
    qi = pl.program_id(2)
    m_ref[...] = jnp.full_like(m_ref, -jnp.inf)
    l_ref[...] = jnp.zeros_like(l_ref)
    acc_ref[...] = jnp.zeros_like(acc_ref)
    qs = [q_ref[:, hh * HEAD_PAD:(hh + 1) * HEAD_PAD] for hh in range(2)]

    def step(j, masked):
        off = pl.multiple_of(j * blk, blk)
        ss = [lax.dot_general(k_ref[pl.ds(off, blk), hh * HEAD_PAD:(hh + 1) * HEAD_PAD], qs[hh],
                              (((1,), (1,)), ((), ())), preferred_element_type=F32) for hh in range(2)]
        ps = []
        for hh in range(2):
            s = ss[hh]
            if masked:
                key = lax.broadcasted_iota(jnp.int32, (blk, blk), 0)
                qry = lax.broadcasted_iota(jnp.int32, (blk, blk), 1)
                s = jnp.where(key <= qry, s, -jnp.inf)
            m_prev = m_ref[hh:hh + 1, :]
            m_new = jnp.maximum(m_prev, jnp.max(s, axis=0, keepdims=True))
            alpha = jnp.exp(m_prev - m_new)
            p = jnp.exp(s - m_new)
            l_ref[hh:hh + 1, :] = alpha * l_ref[hh:hh + 1, :] + jnp.sum(p, axis=0, keepdims=True)
            m_ref[hh:hh + 1, :] = m_new
            acc_ref[hh * V_DIM:(hh + 1) * V_DIM, :] = alpha * acc_ref[hh * V_DIM:(hh + 1) * V_DIM, :]
            ps.append(p.astype(BF16))
        for hh in range(2):
            acc_ref[hh * V_DIM:(hh + 1) * V_DIM, :] += jnp.dot(
                vt_ref[j, hh * V_DIM:(hh + 1) * V_DIM, :], ps[hh], preferred_element_type=F32)

    def body(j, carry):
        step(j, False)
        return carry

    lax.fori_loop(0, qi, body, 0)
    step(qi, True)
    inv_l = jnp.concatenate([jnp.broadcast_to(1.0 / l_ref[hh:hh + 1, :], (V_DIM, blk)) for hh in range(2)], axis=0)
    o_ref[...] = (acc_ref[...] * inv_l).T


def _attn(q, k, vt, batch, seq, blk):
    nb = seq // blk
    pair_w = 2 * HEAD_PAD
    return pl.pallas_call(
        functools.partial(_attn_kernel, blk=blk),
        grid=(batch, N_HEADS // 2, nb),
        in_specs=[pl.BlockSpec((None, blk, pair_w), lambda b, h, i: (b, i, h)),
                  pl.BlockSpec((None, seq, pair_w), lambda b, h, i: (b, 0, h)),
                  pl.BlockSpec((None, nb, LANE, blk), lambda b, h, i: (b, 0, h, 0))],
        out_specs=pl.BlockSpec((None, blk, LANE), lambda b, h, i: (b, i, h)),
        out_shape=jax.ShapeDtypeStruct((batch, seq, GROUP_W), F32),
        scratch_shapes=[pltpu.VMEM((8, blk), F32), pltpu.VMEM((8, blk), F32), pltpu.VMEM((LANE, blk), F32)],
        compiler_params=_cparams(("parallel", "parallel", "arbitrary")),
        name="mla_attn",
    )(q, k, vt)


def _outproj_kernel(ya_ref, yb_ref, oc_ref, zc_ref, gc_ref, w_ref, h_ref, fg_ref, o_ref, *, final):
    oc = oc_ref[...]
    yc = oc * lax.rsqrt(jnp.mean(oc * oc, axis=-1, keepdims=True) + NORM_EPS) * gc_ref[...]
    yc = yc * _silu(zc_ref[...])
    acc = h_ref[...]
    acc = acc + jnp.dot(ya_ref[...], w_ref[0:GROUP_W, :], preferred_element_type=F32)
    acc = acc + jnp.dot(yb_ref[...], w_ref[GROUP_W:2 * GROUP_W, :], preferred_element_type=F32)
    acc = acc + _bdot(yc, w_ref[2 * GROUP_W:3 * GROUP_W, :])
    if final:
        acc = acc * lax.rsqrt(jnp.mean(acc * acc, axis=-1, keepdims=True) + NORM_EPS) * fg_ref[...]
    o_ref[...] = acc


def _outproj(ya, yb, oc, proj, h2d, mla_out_g, w_out, final_g, final, tm):
    m = h2d.shape[0]

    def rows(width, blk=0):
        return pl.BlockSpec((tm, width), lambda i, blk=blk: (i, blk))

    return pl.pallas_call(
        functools.partial(_outproj_kernel, final=final),
        grid=(m // tm,),
        in_specs=[rows(GROUP_W), rows(GROUP_W), rows(GROUP_W), rows(GROUP_W, COL_ZC // GROUP_W),
                  pl.BlockSpec((1, GROUP_W), lambda i: (0, 0)),
                  pl.BlockSpec((3 * GROUP_W, D_MODEL), lambda i: (0, 0)),
                  rows(D_MODEL),
                  pl.BlockSpec((1, D_MODEL), lambda i: (0, 0))],
        out_specs=rows(D_MODEL),
        out_shape=jax.ShapeDtypeStruct((m, D_MODEL), F32),
        compiler_params=_cparams(("parallel",)),
        name="outproj_final" if final else "outproj",
    )(ya, yb, oc, proj, mla_out_g, w_out, h2d, final_g)


def _rope_tables(seq):
    half = QK_ROPE // 2
    inv_freq = ROPE_BASE ** (-jnp.arange(half, dtype=F32) * 2.0 / QK_ROPE)
    ang = jnp.arange(seq, dtype=F32)[:, None] * inv_freq[None, :]
    cos2 = jnp.concatenate([jnp.cos(ang)] * 2, axis=1)
    sin2 = jnp.concatenate([jnp.sin(ang)] * 2, axis=1)
    zeros = lambda w: jnp.zeros((seq, w), F32)
    ones = jnp.ones((seq, QK_NOPE), F32)
    return {
        "cq": jnp.concatenate([ones, cos2, zeros(HEAD_PAD - QK_NOPE - QK_ROPE)], axis=1),
        "sq": jnp.concatenate([zeros(QK_NOPE), sin2, zeros(HEAD_PAD - QK_NOPE - QK_ROPE)], axis=1),
        "ck": jnp.concatenate([cos2, zeros(LANE - QK_ROPE)], axis=1),
        "sk": jnp.concatenate([zeros(QK_ROPE), sin2, zeros(LANE - 2 * QK_ROPE)], axis=1),
    }


def _swap_halves(w):
    half = w.shape[-1] // 2
    return jnp.concatenate([-w[..., half:], w[..., :half]], axis=-1)


def _block_diag(w):
    h, n, _ = w.shape
    eye = jnp.eye(h, dtype=w.dtype)
    return (eye[:, None, :, None] * w[:, :, None, :]).reshape(h * n, h * n)


def _layer_params(l, ln_g, w_in, rwkv_mu, rwkv_w0, rwkv_w2, rwkv_a0, rwkv_a2, rwkv_k_k, rwkv_k_a, rwkv_r_k,
                  rwkv_lnx_g, rwkv_lnx_b, lru_conv_w, lru_conv_b, lru_ga_w, lru_ga_b, lru_gx_w, lru_gx_b,
                  lru_lam, lru_out_g, mla_q_norm_g, mla_w_uq, mla_kv_norm_g, mla_w_ukv, mla_out_g, w_out):
    G = GROUP_W
    w = w_in[l]
    o_ua, o_za = 0, 3 * G + 2 * LORA
    o_ub = o_za + G
    o_zb = o_ub + G
    o_ql = o_zb + G
    o_kv = o_ql + Q_LORA
    o_kr = o_kv + KV_LORA
    o_zc = o_kr + QK_ROPE
    sl = lambda a, n: w[:, a:a + n]
    kr_w = sl(o_kr, QK_ROPE)
    w_re = jnp.concatenate([
        sl(o_ua, G), sl(o_ua + G, G), sl(o_ua + 2 * G, G), sl(o_za, G), sl(o_ub, G), sl(o_zb, G), sl(o_zc, G),
        sl(o_ql, Q_LORA), sl(o_kv, KV_LORA),
        kr_w, _swap_halves(kr_w), jnp.zeros((D_MODEL, LANE - 2 * QK_ROPE), F32),
        sl(o_ua + 3 * G, 2 * LORA)], axis=1).astype(BF16)
    row = lambda v: v.reshape(1, -1)
    mu = rwkv_mu[l]
    zl = jnp.zeros((LORA, G), F32)
    head_id = jnp.arange(RW_GROUP) // HEAD_DIM
    t_idx = jnp.arange(RW_CHUNK)
    scale = (QK_NOPE + QK_ROPE) ** -0.5
    wq3 = mla_w_uq[l].reshape(Q_LORA, N_HEADS, QK_NOPE + QK_ROPE) * scale
    zq = lambda n: jnp.zeros((Q_LORA, N_HEADS, n), F32)
    wq = jnp.concatenate([wq3, zq(HEAD_PAD - QK_NOPE - QK_ROPE)], axis=2)
    wq_sw = jnp.concatenate([zq(QK_NOPE), _swap_halves(wq3[:, :, QK_NOPE:]),
                             zq(HEAD_PAD - QK_NOPE - QK_ROPE)], axis=2)
    wkv3 = mla_w_ukv[l].reshape(KV_LORA, N_HEADS, QK_NOPE + V_DIM)
    wk = jnp.concatenate([wkv3[:, :, :QK_NOPE], jnp.zeros((KV_LORA, N_HEADS, HEAD_PAD - QK_NOPE), F32)], axis=2)
    place = jnp.zeros((LANE, N_HEADS, HEAD_PAD), F32)
    place = place.at[jnp.arange(QK_ROPE), :, QK_NOPE + jnp.arange(QK_ROPE)].set(1.0)
    return {
        "ln_g": row(ln_g[l]), "w_in": w_re,
        "mu_r": row(mu[0:G]), "mu_k": row(mu[G:2 * G]), "mu_v": row(mu[2 * G:3 * G]),
        "mu_wa": row(mu[3 * G:3 * G + 2 * LORA]),
        "w0": row(rwkv_w0[l]), "w2p": jnp.concatenate([rwkv_w2[l], zl], axis=0).astype(BF16),
        "a0": row(rwkv_a0[l]), "a2p": jnp.concatenate([zl, rwkv_a2[l]], axis=0).astype(BF16),
        "k_k": row(rwkv_k_k[l]), "k_a": row(rwkv_k_a[l]), "r_k": row(rwkv_r_k[l]),
        "lnx_g": row(rwkv_lnx_g[l]), "lnx_b": row(rwkv_lnx_b[l]),
        "seg": (head_id[:, None] == head_id[None, :]).astype(BF16),
        "tri": (t_idx[None, :] <= t_idx[:, None]).astype(BF16),
        "conv_w": lru_conv_w[l], "conv_b": row(lru_conv_b[l]),
        "ga_w": _block_diag(lru_ga_w[l]).astype(BF16), "ga_b": row(lru_ga_b[l]),
        "gx_w": _block_diag(lru_gx_w[l]).astype(BF16), "gx_b": row(lru_gx_b[l]),
        "lam": row(lru_lam[l]), "lru_out_g": row(lru_out_g[l]),
        "q_norm_g": row(mla_q_norm_g[l]), "kv_norm_g": row(mla_kv_norm_g[l]),
        "wq": wq.reshape(Q_LORA, -1).astype(BF16), "wq_sw": wq_sw.reshape(Q_LORA, -1).astype(BF16),
        "wk": wk.reshape(KV_LORA, -1).astype(BF16),
        "wv_t": wkv3[:, :, QK_NOPE:].reshape(KV_LORA, -1).T.astype(BF16),
        "k_place": place.reshape(LANE, -1).astype(BF16),
        "mla_out_g": row(mla_out_g[l]), "w_out": w_out[l].astype(BF16),
    }


def kernel(x, ln_g, w_in, rwkv_mu, rwkv_w0, rwkv_w2, rwkv_a0, rwkv_a2, rwkv_k_k, rwkv_k_a, rwkv_r_k,
           rwkv_lnx_g, rwkv_lnx_b, lru_conv_w, lru_conv_b, lru_ga_w, lru_ga_b, lru_gx_w, lru_gx_b, lru_lam,
           lru_out_g, mla_q_norm_g, mla_w_uq, mla_kv_norm_g, mla_w_ukv, mla_out_g, w_out, final_g):
    batch, seq, _ = x.shape
    tm = min(512, seq)
    rope = _rope_tables(seq)
    fg = final_g.reshape(1, -1)
    h = x.reshape(batch * seq, D_MODEL)
    for l in range(DEPTH):
        prm = _layer_params(l, ln_g, w_in, rwkv_mu, rwkv_w0, rwkv_w2, rwkv_a0, rwkv_a2, rwkv_k_k, rwkv_k_a,
                            rwkv_r_k, rwkv_lnx_g, rwkv_lnx_b, lru_conv_w, lru_conv_b, lru_ga_w, lru_ga_b,
                            lru_gx_w, lru_gx_b, lru_lam, lru_out_g, mla_q_norm_g, mla_w_uq, mla_kv_norm_g,
                            mla_w_ukv, mla_out_g, w_out)
        proj2d = _inproj(h, prm["ln_g"], prm["w_in"], tm)
        proj = proj2d.reshape(batch, seq, D_PROJ)
        ya = _rwkv(proj, prm, batch, seq)
        yb = _lru(proj, prm, batch, seq, min(256, seq))
        q, k, v = _mla_pro(proj, prm, rope, batch, seq, tm)
        oc = _attn(q, k, v, batch, seq, min(512, seq))
        h = _outproj(ya.reshape(batch * seq, GROUP_W), yb.reshape(batch * seq, GROUP_W),
                     oc.reshape(batch * seq, GROUP_W), proj2d, h, prm["mla_out_g"], prm["w_out"], fg,
                     l == DEPTH - 1, tm)
    return h.reshape(batch, seq, D_MODEL)
```

```python
import functools

import jax
import jax.numpy as jnp
from jax import lax
from jax.experimental import pallas as pl
from jax.experimental.pallas import tpu as pltpu

F32 = jnp.float32
BF16 = jnp.bfloat16

D_MODEL = 1024
DEPTH = 4
GROUP_W = 512
HEAD_DIM = 64
N_HEADS = 8
NORM_EPS = 1e-6
LORA = 64
GN_EPS = 64e-5
CONV_W = 4
LRU_C = 8.0
Q_LORA = 256
KV_LORA = 128
QK_NOPE = 64
QK_ROPE = 32
V_DIM = 64
ROPE_BASE = 10000.0

LANE = 128
SUBLANES = 8
HEAD_PAD = 128
RW_GROUP = 256
RW_CHUNK = 128
BF16_ROWS = 16
ATT_ONES_ROWS = BF16_ROWS
ATT_ACC_ROWS = V_DIM + ATT_ONES_ROWS
LOG2E = 1.4426950408889634
COL_R, COL_K, COL_V, COL_ZA, COL_UB, COL_ZB, COL_ZC = (i * GROUP_W for i in range(7))
COL_QLAT = 7 * GROUP_W
COL_KVLAT = COL_QLAT + Q_LORA
COL_KR = COL_KVLAT + KV_LORA
COL_WA = COL_KR + LANE
D_PROJ = COL_WA + LANE

VMEM_LIMIT = 56 * 1024 * 1024


def _cparams(sem):
    return pltpu.CompilerParams(dimension_semantics=sem, vmem_limit_bytes=VMEM_LIMIT)


def _bdot(a, b):
    return jnp.dot(a.astype(BF16), b.astype(BF16), preferred_element_type=F32)


def _bdot_nt(a, b):
    return lax.dot_general(a.astype(BF16), b.astype(BF16), (((1,), (1,)), ((), ())),
                           preferred_element_type=F32)


def _split_dot(x, m, passes):
    acc = None
    rem = x
    for _ in range(passes):
        part = rem.astype(BF16)
        term = jnp.dot(part, m, preferred_element_type=F32)
        acc = term if acc is None else acc + term
        rem = rem - part.astype(F32)
    return acc


def _silu(z):
    return z * jax.nn.sigmoid(z)


def _inproj_kernel(x_ref, g_ref, w_ref, o_ref):
    x = x_ref[...]
    ms = jnp.mean(x * x, axis=-1, keepdims=True)
    xn = x * lax.rsqrt(ms + NORM_EPS) * g_ref[...]
    o_ref[...] = _bdot(xn, w_ref[...])


def _inproj(h2d, g, w, tm):
    m = h2d.shape[0]
    return pl.pallas_call(
        _inproj_kernel,
        grid=(m // tm,),
        in_specs=[pl.BlockSpec((tm, D_MODEL), lambda i: (i, 0)),
                  pl.BlockSpec((1, D_MODEL), lambda i: (0, 0)),
                  pl.BlockSpec((D_MODEL, D_PROJ), lambda i: (0, 0))],
        out_specs=pl.BlockSpec((tm, D_PROJ), lambda i: (i, 0)),
        out_shape=jax.ShapeDtypeStruct((m, D_PROJ), F32),
        compiler_params=_cparams(("parallel",)),
        name="inproj",
    )(h2d, g, w)


def _split_dot_left(m, x):
    acc = None
    rem = x
    for _ in range(3):
        part = rem.astype(BF16)
        term = jnp.dot(m, part, preferred_element_type=F32)
        acc = term if acc is None else acc + term
        rem = rem - part.astype(F32)
    return acc


def _seg_sum(x, seg, passes):
    parts = [_split_dot(x[:, i:i + RW_GROUP], seg, passes) for i in range(0, x.shape[1], RW_GROUP)]
    return jnp.concatenate(parts, axis=1)


def _rwkv_kernel(r_ref, k_ref, v_ref, wa_ref, z_ref,
                 mur_ref, muk_ref, muv_ref, muwa_ref,
                 w0_ref, w2_ref, a0_ref, a2_ref, kk_ref, ka_ref, rk_ref, lng_ref, lnb_ref,
                 seg_ref, tri_ref,
                 o_ref,
                 state_ref, pr_ref, pk_ref, pv_ref, pwa_ref):
    c = pl.program_id(1)
    C = r_ref.shape[0]
    n_groups = GROUP_W // RW_GROUP

    @pl.when(c == 0)
    def _():
        state_ref[...] = jnp.zeros_like(state_ref)
        pr_ref[...] = jnp.zeros_like(pr_ref)
        pk_ref[...] = jnp.zeros_like(pk_ref)
        pv_ref[...] = jnp.zeros_like(pv_ref)
        pwa_ref[...] = jnp.zeros_like(pwa_ref)

    def lerp_shift(x_ref, prev_ref, mu_ref):
        x = x_ref[...]
        row = lax.broadcasted_iota(jnp.int32, x.shape, 0)
        xs = jnp.where(row == 0, prev_ref[...], pltpu.roll(x, 1, 0))
        prev_ref[...] = x[C - 1:C, :]
        return x + mu_ref[...] * (xs - x)

    r = lerp_shift(r_ref, pr_ref, mur_ref)
    k = lerp_shift(k_ref, pk_ref, muk_ref)
    v = lerp_shift(v_ref, pv_ref, muv_ref)
    wa = lerp_shift(wa_ref, pwa_ref, muwa_ref)

    log_w = -jax.nn.softplus(-(w0_ref[...] + _bdot(jnp.tanh(wa), w2_ref[...]))) - 0.5
    ld = -jnp.exp(log_w)
    a = jax.nn.sigmoid(a0_ref[...] + _bdot(wa, a2_ref[...]))
    seg = seg_ref[...]
    kk = k * kk_ref[...]
    kk = kk / jnp.maximum(jnp.sqrt(_seg_sum(kk * kk, seg, 3)), 1e-12)
    k = k * (1.0 + (a - 1.0) * ka_ref[...])
    bb = kk * a

    g = _split_dot_left(tri_ref[...], ld)
    g_last = g[C - 1:C, :]
    inv_g = jnp.exp(-g)
    to_end = jnp.exp(g_last - g)
    lhs = jnp.concatenate([r * jnp.exp(g), kk * jnp.exp(g - ld)], axis=0).astype(BF16)
    rhs = jnp.concatenate([k * inv_g, bb * inv_g], axis=0)
    kb_end = jnp.concatenate([k * to_end, -(bb * to_end)], axis=0)
    vb = v.astype(BF16)

    def grp(x, i):
        return x[:, i * RW_GROUP:(i + 1) * RW_GROUP]

    def pair(x, i):
        return x[:, i * LANE:(i + 1) * LANE]

    p = [lax.dot_general(grp(lhs, i), state_ref[i].astype(BF16), (((1,), (1,)), ((), ())),
                         preferred_element_type=F32) for i in range(n_groups)]
    p = jnp.concatenate(p, axis=1)

    lane2 = lax.broadcasted_iota(jnp.int32, (2 * C, LANE), 1)
    lane1 = lax.broadcasted_iota(jnp.int32, (C, LANE), 1)
    ti = lax.broadcasted_iota(jnp.int32, (C, C), 0)
    si = lax.broadcasted_iota(jnp.int32, (C, C), 1)
    incl = si <= ti
    strict = si < ti
    heads = range(N_HEADS)
    low = [hd % 2 == 0 for hd in heads]

    def own2(hd, x):
        return jnp.where((lane2 < HEAD_DIM) == low[hd], x, 0.0)

    amat = [_bdot_nt(pair(lhs, hd // 2), own2(hd, pair(rhs, hd // 2))) for hd in heads]
    a_y = [jnp.concatenate([jnp.where(incl, m[0:C, 0:C], 0.0), jnp.where(incl, -m[0:C, C:2 * C], 0.0)],
                           axis=1).astype(BF16) for m in amat]
    xs = [pair(p, hd // 2)[C:2 * C, :]
          + _bdot(jnp.where(strict, amat[hd][C:2 * C, 0:C], 0.0), pair(vb, hd // 2)) for hd in heads]
    npow = [jnp.where(strict, -amat[hd][C:2 * C, C:2 * C], 0.0).astype(BF16) for hd in heads]
    span = 1
    while span < C:
        lo = span if span % BF16_ROWS == 0 else 0
        if 2 * span < C:
            outs = [_bdot(n[lo:, :], jnp.concatenate([n, x.astype(BF16)], axis=1)) for n, x in zip(npow, xs)]
            npow = [jnp.concatenate([jnp.zeros((lo, C), BF16), o[:, 0:C].astype(BF16)], axis=0) if lo
                    else o[:, 0:C].astype(BF16) for o in outs]
            upd = [o[:, C:] for o in outs]
        else:
            upd = [_bdot(n[lo:, :], x) for n, x in zip(npow, xs)]
        xs = [jnp.concatenate([x[0:lo, :], x[lo:, :] + d], axis=0) if lo else x + d for x, d in zip(xs, upd)]
        span *= 2
    y_h = [_bdot(a_y[hd], jnp.concatenate([pair(vb, hd // 2), xs[hd].astype(BF16)], axis=0)) for hd in heads]

    first = lane1 < HEAD_DIM
    n_pairs = GROUP_W // LANE
    y = jnp.concatenate([jnp.where(first, y_h[2 * i], y_h[2 * i + 1]) for i in range(n_pairs)], axis=1)
    u = jnp.concatenate([jnp.where(first, xs[2 * i], xs[2 * i + 1]) for i in range(n_pairs)], axis=1)
    y = y + p[0:C, :]

    vu_t = jnp.concatenate([v, u], axis=0).T
    decay_all = jnp.exp(g_last)
    for i in range(n_groups):
        s_new = state_ref[i] * grp(decay_all, i) + _bdot(vu_t[i * RW_GROUP:(i + 1) * RW_GROUP, :], grp(kb_end, i))
        state_ref[i] = jnp.where(seg > 0.5, s_new, 0.0)

    inv_n = 1.0 / HEAD_DIM
    mean = _seg_sum(y, seg, 2) * inv_n
    yc = y - mean
    var = _seg_sum(yc * yc, seg, 2) * inv_n
    yn = yc * lax.rsqrt(var + GN_EPS) * lng_ref[...] + lnb_ref[...]
    bonus = _seg_sum(r * k * rk_ref[...], seg, 2) * v
    o_ref[...] = ((yn + bonus) * _silu(z_ref[...])).astype(o_ref.dtype)


def _rwkv(proj, prm, batch, seq):
    C = RW_CHUNK

    def col(base):
        return pl.BlockSpec((None, C, GROUP_W), lambda b, c, base=base: (b, c, base // GROUP_W))

    def vec(width=GROUP_W):
        return pl.BlockSpec((1, width), lambda b, c: (0, 0))

    def whole(shape):
        return pl.BlockSpec(shape, lambda b, c: tuple(0 for _ in shape))

    in_specs = [col(COL_R), col(COL_K), col(COL_V),
                pl.BlockSpec((None, C, LANE), lambda b, c: (b, c, COL_WA // LANE)),
                col(COL_ZA),
                vec(), vec(), vec(), vec(LANE),
                vec(), whole((LANE, GROUP_W)), vec(), whole((LANE, GROUP_W)),
                vec(), vec(), vec(), vec(), vec(),
                whole((RW_GROUP, RW_GROUP)), whole((C, C))]
    return pl.pallas_call(
        _rwkv_kernel,
        grid=(batch, seq // C),
        in_specs=in_specs,
        out_specs=pl.BlockSpec((None, C, GROUP_W), lambda b, c: (b, c, 0)),
        out_shape=jax.ShapeDtypeStruct((batch, seq, GROUP_W), BF16),
        scratch_shapes=[pltpu.VMEM((GROUP_W // RW_GROUP, RW_GROUP, RW_GROUP), F32),
                        pltpu.VMEM((1, GROUP_W), F32), pltpu.VMEM((1, GROUP_W), F32),
                        pltpu.VMEM((1, GROUP_W), F32), pltpu.VMEM((1, LANE), F32)],
        compiler_params=_cparams(("parallel", "arbitrary")),
        name="rwkv7",
    )(proj, proj, proj, proj, proj,
      prm["mu_r"], prm["mu_k"], prm["mu_v"], prm["mu_wa"],
      prm["w0"], prm["w2p"], prm["a0"], prm["a2p"], prm["k_k"], prm["k_a"], prm["r_k"],
      prm["lnx_g"], prm["lnx_b"], prm["seg"], prm["tri"])


def _lru_kernel(u_ref, z_ref, cw_ref, cb_ref, gaw_ref, gab_ref, gxw_ref, gxb_ref, lam_ref, og_ref,
                o_ref, tail_ref, hprev_ref):
    c = pl.program_id(1)
    ct = u_ref.shape[0]

    @pl.when(c == 0)
    def _():
        tail_ref[...] = jnp.zeros_like(tail_ref)
        hprev_ref[...] = jnp.zeros_like(hprev_ref)

    u = u_ref[...]
    tail = tail_ref[...]
    row8 = lax.broadcasted_iota(jnp.int32, tail.shape, 0)
    xc = cb_ref[...] + u * cw_ref[CONV_W - 1:CONV_W, :]
    for d in range(1, CONV_W):
        rolled = pltpu.roll(u, d, 0)
        first = jnp.where(row8 < d, pltpu.roll(tail, d, 0), rolled[0:8, :])
        ud = jnp.concatenate([first, rolled[8:, :]], axis=0)
        xc = xc + ud * cw_ref[CONV_W - 1 - d:CONV_W - d, :]
    tail_ref[...] = u[ct - 8:ct, :]

    gate_r = jax.nn.sigmoid(_bdot(xc, gaw_ref[...]) + gab_ref[...])
    gate_i = jax.nn.sigmoid(_bdot(xc, gxw_ref[...]) + gxb_ref[...])
    log_a = -LRU_C * gate_r * jax.nn.softplus(-lam_ref[...])
    acc_a = jnp.exp(log_a)
    th = jnp.tanh(log_a)
    acc_b = jnp.sqrt(-2.0 * th / (1.0 - th)) * (gate_i * xc)
    sub = lax.broadcasted_iota(jnp.int32, acc_a.shape, 0) % SUBLANES
    d = 1
    while d < SUBLANES:
        keep = sub >= d
        acc_b = acc_a * jnp.where(keep, pltpu.roll(acc_b, d, 0), 0.0) + acc_b
        acc_a = acc_a * jnp.where(keep, pltpu.roll(acc_a, d, 0), 1.0)
        d *= 2
    carry = hprev_ref[...]
    groups = []
    for i in range(ct // SUBLANES):
        rows = slice(i * SUBLANES, (i + 1) * SUBLANES)
        h_i = acc_a[rows, :] * carry + acc_b[rows, :]
        groups.append(h_i)
        carry = h_i[SUBLANES - 1:SUBLANES, :]
    hs = jnp.concatenate(groups, axis=0)
    hprev_ref[...] = hs[ct - 1:ct, :]
    ms = jnp.mean(hs * hs, axis=-1, keepdims=True)
    yb = hs * lax.rsqrt(ms + NORM_EPS) * og_ref[...]
    o_ref[...] = (yb * _silu(z_ref[...])).astype(o_ref.dtype)


def _lru(proj, prm, batch, seq, ct):
    def vec():
        return pl.BlockSpec((1, GROUP_W), lambda b, c: (0, 0))

    def mat(rows):
        return pl.BlockSpec((rows, GROUP_W), lambda b, c: (0, 0))

    return pl.pallas_call(
        _lru_kernel,
        grid=(batch, seq // ct),
        in_specs=[pl.BlockSpec((None, ct, GROUP_W), lambda b, c: (b, c, COL_UB // GROUP_W)),
                  pl.BlockSpec((None, ct, GROUP_W), lambda b, c: (b, c, COL_ZB // GROUP_W)),
                  mat(CONV_W), vec(), mat(GROUP_W), vec(), mat(GROUP_W), vec(), vec(), vec()],
        out_specs=pl.BlockSpec((None, ct, GROUP_W), lambda b, c: (b, c, 0)),
        out_shape=jax.ShapeDtypeStruct((batch, seq, GROUP_W), BF16),
        scratch_shapes=[pltpu.VMEM((8, GROUP_W), F32), pltpu.VMEM((1, GROUP_W), F32)],
        compiler_params=_cparams(("parallel", "arbitrary")),
        name="rglru",
    )(proj, proj, prm["conv_w"], prm["conv_b"], prm["ga_w"], prm["ga_b"], prm["gx_w"], prm["gx_b"],
      prm["lam"], prm["lru_out_g"])


def _mla_pro_kernel(ql_ref, kvl_ref, kr_ref, qg_ref, kvg_ref, wq_ref, wqs_ref, wk_ref, wv_ref, pk_ref,
                    cq_ref, sq_ref, ck_ref, sk_ref, q_ref, k_ref, v_ref):
    ql = ql_ref[...]
    qn = ql * lax.rsqrt(jnp.mean(ql * ql, axis=-1, keepdims=True) + NORM_EPS) * qg_ref[...]
    qn = qn.astype(BF16)
    cq = jnp.concatenate([cq_ref[...]] * N_HEADS, axis=1)
    sq = jnp.concatenate([sq_ref[...]] * N_HEADS, axis=1)
    q = jnp.dot(qn, wq_ref[...], preferred_element_type=F32) * cq
    q = q + jnp.dot(qn, wqs_ref[...], preferred_element_type=F32) * sq
    q_ref[...] = q.astype(q_ref.dtype)

    kvl = kvl_ref[...]
    kvn = kvl * lax.rsqrt(jnp.mean(kvl * kvl, axis=-1, keepdims=True) + NORM_EPS) * kvg_ref[...]
    kvn = kvn.astype(BF16)
    kr = kr_ref[...]
    roped = kr * ck_ref[...] + pltpu.roll(kr * sk_ref[...], LANE - QK_ROPE, 1)
    kfull = jnp.dot(kvn, wk_ref[...], preferred_element_type=F32) + _bdot(roped, pk_ref[...])
    k_ref[...] = kfull.astype(k_ref.dtype)
    vt = lax.dot_general(wv_ref[...], kvn, (((1,), (1,)), ((), ())), preferred_element_type=F32)
    v_ref[...] = vt.astype(v_ref.dtype)


def _mla_pro(proj, prm, rope, batch, seq, tm):
    qk_w = N_HEADS * HEAD_PAD

    def whole(shape):
        return pl.BlockSpec(shape, lambda b, t: tuple(0 for _ in shape))

    def tab():
        return pl.BlockSpec((tm, LANE), lambda b, t: (t, 0))

    return pl.pallas_call(
        _mla_pro_kernel,
        grid=(batch, seq // tm),
        in_specs=[pl.BlockSpec((None, tm, Q_LORA), lambda b, t: (b, t, COL_QLAT // Q_LORA)),
                  pl.BlockSpec((None, tm, KV_LORA), lambda b, t: (b, t, COL_KVLAT // KV_LORA)),
                  pl.BlockSpec((None, tm, LANE), lambda b, t: (b, t, COL_KR // LANE)),
                  whole((1, Q_LORA)), whole((1, KV_LORA)),
                  whole((Q_LORA, qk_w)), whole((Q_LORA, qk_w)), whole((KV_LORA, qk_w)),
                  whole((GROUP_W, KV_LORA)), whole((LANE, qk_w)),
                  tab(), tab(), tab(), tab()],
        out_specs=[pl.BlockSpec((None, tm, qk_w), lambda b, t: (b, t, 0)),
                   pl.BlockSpec((None, tm, qk_w), lambda b, t: (b, t, 0)),
                   pl.BlockSpec((None, None, GROUP_W, tm), lambda b, t: (b, t, 0, 0))],
        out_shape=[jax.ShapeDtypeStruct((batch, seq, qk_w), BF16),
                   jax.ShapeDtypeStruct((batch, seq, qk_w), BF16),
                   jax.ShapeDtypeStruct((batch, seq // tm, GROUP_W, tm), BF16)],
        compiler_params=_cparams(("parallel", "parallel")),
        name="mla_pro",
    )(proj, proj, proj, prm["q_norm_g"], prm["kv_norm_g"], prm["wq"], prm["wq_sw"], prm["wk"], prm["wv_t"],
      prm["k_place"], rope["cq"], rope["sq"], rope["ck"], rope["sk"])


def _attn_kernel(q_ref, k_ref, vt_ref, o_ref, m_ref, acc_ref, sa_ref, sb_ref, pa_ref, pb_ref, ala_ref, alb_ref,
                 *, blk, kblk):
    qi = pl.program_id(2)
    m_ref[...] = jnp.full_like(m_ref, -jnp.inf)
    acc_ref[...] = jnp.zeros_like(acc_ref)
    pb_ref[...] = jnp.zeros_like(pb_ref)
    alb_ref[...] = jnp.ones_like(alb_ref)
    qs = [q_ref[:, hh * HEAD_PAD:(hh + 1) * HEAD_PAD] for hh in range(2)]
    ones = jnp.ones((ATT_ONES_ROWS, kblk), BF16)
    key = lax.broadcasted_iota(jnp.int32, (kblk, blk), 0)
    qry = lax.broadcasted_iota(jnp.int32, (kblk, blk), 1)

    def scores(t, s_ref):
        off = pl.multiple_of(t * kblk, kblk)
        for hh in range(2):
            s_ref[hh] = lax.dot_general(k_ref[pl.ds(off, kblk), hh * HEAD_PAD:(hh + 1) * HEAD_PAD], qs[hh],
                                        (((1,), (1,)), ((), ())), preferred_element_type=F32)

    def softmax(s_ref, p_ref, al_ref, key_shift):
        for hh in range(2):
            s = s_ref[hh]
            if key_shift is not None:
                s = jnp.where(key + key_shift <= qry, s, -jnp.inf)
            m_prev = m_ref[hh:hh + 1, :]
            m_new = jnp.maximum(m_prev, jnp.max(s, axis=0, keepdims=True))
            al_ref[hh:hh + 1, :] = jnp.exp2(m_prev - m_new)
            p_ref[hh] = jnp.exp2(s - m_new).astype(BF16)
            m_ref[hh:hh + 1, :] = m_new

    def values(t, p_ref, al_ref):
        for hh in range(2):
            rows = slice(hh * ATT_ACC_ROWS, (hh + 1) * ATT_ACC_ROWS)
            vt_ext = jnp.concatenate([vt_ref[t, hh * V_DIM:(hh + 1) * V_DIM, :], ones], axis=0)
            acc_ref[rows, :] = (al_ref[hh:hh + 1, :] * acc_ref[rows, :]
                                + jnp.dot(vt_ext, p_ref[hh], preferred_element_type=F32))

    scores(0, sa_ref)

    def pair(u, carry):
        t = 2 * u
        scores(t + 1, sb_ref)
        softmax(sa_ref, pa_ref, ala_ref, None)
        values(jnp.maximum(t - 1, 0), pb_ref, alb_ref)
        scores(t + 2, sa_ref)
        softmax(sb_ref, pb_ref, alb_ref, None)
        values(t, pa_ref, ala_ref)
        return carry

    lax.fori_loop(0, qi, pair, 0)
    t = 2 * qi
    scores(t + 1, sb_ref)
    softmax(sa_ref, pa_ref, ala_ref, 0)
    values(jnp.maximum(t - 1, 0), pb_ref, alb_ref)
    softmax(sb_ref, pb_ref, alb_ref, kblk)
    values(t, pa_ref, ala_ref)
    values(t + 1, pb_ref, alb_ref)
    outs = []
    for hh in range(2):
        base = hh * ATT_ACC_ROWS
        outs.append(acc_ref[base:base + V_DIM, :] / acc_ref[base + V_DIM:base + V_DIM + 1, :])
    o_ref[...] = jnp.concatenate(outs, axis=0).T


def _attn(q, k, vt, batch, seq, blk, kblk):
    pair_w = 2 * HEAD_PAD
    s_buf = pltpu.VMEM((2, kblk, blk), F32)
    p_buf = pltpu.VMEM((2, kblk, blk), BF16)
    al_buf = pltpu.VMEM((8, blk), F32)
    return pl.pallas_call(
        functools.partial(_attn_kernel, blk=blk, kblk=kblk),
        grid=(batch, N_HEADS // 2, seq // blk),
        in_specs=[pl.BlockSpec((None, blk, pair_w), lambda b, h, i: (b, i, h)),
                  pl.BlockSpec((None, seq, pair_w), lambda b, h, i: (b, 0, h)),
                  pl.BlockSpec((None, seq // kblk, LANE, kblk), lambda b, h, i: (b, 0, h, 0))],
        out_specs=pl.BlockSpec((None, blk, LANE), lambda b, h, i: (b, i, h)),
        out_shape=jax.ShapeDtypeStruct((batch, seq, GROUP_W), F32),
        scratch_shapes=[pltpu.VMEM((8, blk), F32), pltpu.VMEM((2 * ATT_ACC_ROWS, blk), F32),
                        s_buf, s_buf, p_buf, p_buf, al_buf, al_buf],
        compiler_params=_cparams(("parallel", "parallel", "arbitrary")),
        name="mla_attn",
    )(q, k, vt)


def _outproj_kernel(ya_ref, yb_ref, oc_ref, zc_ref, gc_ref, w_ref, h_ref, fg_ref, o_ref, *, final):
    oc = oc_ref[...]
    yc = oc * lax.rsqrt(jnp.mean(oc * oc, axis=-1, keepdims=True) + NORM_EPS) * gc_ref[...]
    yc = yc * _silu(zc_ref[...])
    acc = h_ref[...]
    acc = acc + jnp.dot(ya_ref[...], w_ref[0:GROUP_W, :], preferred_element_type=F32)
    acc = acc + jnp.dot(yb_ref[...], w_ref[GROUP_W:2 * GROUP_W, :], preferred_element_type=F32)
    acc = acc + _bdot(yc, w_ref[2 * GROUP_W:3 * GROUP_W, :])
    if final:
        acc = acc * lax.rsqrt(jnp.mean(acc * acc, axis=-1, keepdims=True) + NORM_EPS) * fg_ref[...]
    o_ref[...] = acc


def _outproj(ya, yb, oc, proj, h2d, mla_out_g, w_out, final_g, final, tm):
    m = h2d.shape[0]

    def rows(width, blk=0):
        return pl.BlockSpec((tm, width), lambda i, blk=blk: (i, blk))

    return pl.pallas_call(
        functools.partial(_outproj_kernel, final=final),
        grid=(m // tm,),
        in_specs=[rows(GROUP_W), rows(GROUP_W), rows(GROUP_W), rows(GROUP_W, COL_ZC // GROUP_W),
                  pl.BlockSpec((1, GROUP_W), lambda i: (0, 0)),
                  pl.BlockSpec((3 * GROUP_W, D_MODEL), lambda i: (0, 0)),
                  rows(D_MODEL),
                  pl.BlockSpec((1, D_MODEL), lambda i: (0, 0))],
        out_specs=rows(D_MODEL),
        out_shape=jax.ShapeDtypeStruct((m, D_MODEL), F32),
        compiler_params=_cparams(("parallel",)),
        name="outproj_final" if final else "outproj",
    )(ya, yb, oc, proj, mla_out_g, w_out, h2d, final_g)


def _rope_tables(seq):
    half = QK_ROPE // 2
    inv_freq = ROPE_BASE ** (-jnp.arange(half, dtype=F32) * 2.0 / QK_ROPE)
    ang = jnp.arange(seq, dtype=F32)[:, None] * inv_freq[None, :]
    cos2 = jnp.concatenate([jnp.cos(ang)] * 2, axis=1)
    sin2 = jnp.concatenate([jnp.sin(ang)] * 2, axis=1)
    zeros = lambda w: jnp.zeros((seq, w), F32)
    ones = jnp.ones((seq, QK_NOPE), F32)
    return {
        "cq": jnp.concatenate([ones, cos2, zeros(HEAD_PAD - QK_NOPE - QK_ROPE)], axis=1),
        "sq": jnp.concatenate([zeros(QK_NOPE), sin2, zeros(HEAD_PAD - QK_NOPE - QK_ROPE)], axis=1),
        "ck": jnp.concatenate([cos2, zeros(LANE - QK_ROPE)], axis=1),
        "sk": jnp.concatenate([zeros(QK_ROPE), sin2, zeros(LANE - 2 * QK_ROPE)], axis=1),
    }


def _swap_halves(w):
    half = w.shape[-1] // 2
    return jnp.concatenate([-w[..., half:], w[..., :half]], axis=-1)


def _block_diag(w):
    h, n, _ = w.shape
    eye = jnp.eye(h, dtype=w.dtype)
    return (eye[:, None, :, None] * w[:, :, None, :]).reshape(h * n, h * n)


def _layer_params(l, ln_g, w_in, rwkv_mu, rwkv_w0, rwkv_w2, rwkv_a0, rwkv_a2, rwkv_k_k, rwkv_k_a, rwkv_r_k,
                  rwkv_lnx_g, rwkv_lnx_b, lru_conv_w, lru_conv_b, lru_ga_w, lru_ga_b, lru_gx_w, lru_gx_b,
                  lru_lam, lru_out_g, mla_q_norm_g, mla_w_uq, mla_kv_norm_g, mla_w_ukv, mla_out_g, w_out):
    G = GROUP_W
    w = w_in[l]
    o_ua, o_za = 0, 3 * G + 2 * LORA
    o_ub = o_za + G
    o_zb = o_ub + G
    o_ql = o_zb + G
    o_kv = o_ql + Q_LORA
    o_kr = o_kv + KV_LORA
    o_zc = o_kr + QK_ROPE
    sl = lambda a, n: w[:, a:a + n]
    kr_w = sl(o_kr, QK_ROPE)
    w_re = jnp.concatenate([
        sl(o_ua, G), sl(o_ua + G, G), sl(o_ua + 2 * G, G), sl(o_za, G), sl(o_ub, G), sl(o_zb, G), sl(o_zc, G),
        sl(o_ql, Q_LORA), sl(o_kv, KV_LORA),
        kr_w, _swap_halves(kr_w), jnp.zeros((D_MODEL, LANE - 2 * QK_ROPE), F32),
        sl(o_ua + 3 * G, 2 * LORA)], axis=1).astype(BF16)
    row = lambda v: v.reshape(1, -1)
    mu = rwkv_mu[l]
    zl = jnp.zeros((LORA, G), F32)
    head_id = jnp.arange(RW_GROUP) // HEAD_DIM
    t_idx = jnp.arange(RW_CHUNK)
    scale = (QK_NOPE + QK_ROPE) ** -0.5 * LOG2E
    wq3 = mla_w_uq[l].reshape(Q_LORA, N_HEADS, QK_NOPE + QK_ROPE) * scale
    zq = lambda n: jnp.zeros((Q_LORA, N_HEADS, n), F32)
    wq = jnp.concatenate([wq3, zq(HEAD_PAD - QK_NOPE - QK_ROPE)], axis=2)
    wq_sw = jnp.concatenate([zq(QK_NOPE), _swap_halves(wq3[:, :, QK_NOPE:]),
                             zq(HEAD_PAD - QK_NOPE - QK_ROPE)], axis=2)
    wkv3 = mla_w_ukv[l].reshape(KV_LORA, N_HEADS, QK_NOPE + V_DIM)
    wk = jnp.concatenate([wkv3[:, :, :QK_NOPE], jnp.zeros((KV_LORA, N_HEADS, HEAD_PAD - QK_NOPE), F32)], axis=2)
    place = jnp.zeros((LANE, N_HEADS, HEAD_PAD), F32)
    place = place.at[jnp.arange(QK_ROPE), :, QK_NOPE + jnp.arange(QK_ROPE)].set(1.0)
    return {
        "ln_g": row(ln_g[l]), "w_in": w_re,
        "mu_r": row(mu[0:G]), "mu_k": row(mu[G:2 * G]), "mu_v": row(mu[2 * G:3 * G]),
        "mu_wa": row(mu[3 * G:3 * G + 2 * LORA]),
        "w0": row(rwkv_w0[l]), "w2p": jnp.concatenate([rwkv_w2[l], zl], axis=0).astype(BF16),
        "a0": row(rwkv_a0[l]), "a2p": jnp.concatenate([zl, rwkv_a2[l]], axis=0).astype(BF16),
        "k_k": row(rwkv_k_k[l]), "k_a": row(rwkv_k_a[l]), "r_k": row(rwkv_r_k[l]),
        "lnx_g": row(rwkv_lnx_g[l]), "lnx_b": row(rwkv_lnx_b[l]),
        "seg": (head_id[:, None] == head_id[None, :]).astype(BF16),
        "tri": (t_idx[None, :] <= t_idx[:, None]).astype(BF16),
        "conv_w": lru_conv_w[l], "conv_b": row(lru_conv_b[l]),
        "ga_w": _block_diag(lru_ga_w[l]).astype(BF16), "ga_b": row(lru_ga_b[l]),
        "gx_w": _block_diag(lru_gx_w[l]).astype(BF16), "gx_b": row(lru_gx_b[l]),
        "lam": row(lru_lam[l]), "lru_out_g": row(lru_out_g[l]),
        "q_norm_g": row(mla_q_norm_g[l]), "kv_norm_g": row(mla_kv_norm_g[l]),
        "wq": wq.reshape(Q_LORA, -1).astype(BF16), "wq_sw": wq_sw.reshape(Q_LORA, -1).astype(BF16),
        "wk": wk.reshape(KV_LORA, -1).astype(BF16),
        "wv_t": wkv3[:, :, QK_NOPE:].reshape(KV_LORA, -1).T.astype(BF16),
        "k_place": place.reshape(LANE, -1).astype(BF16),
        "mla_out_g": row(mla_out_g[l]), "w_out": w_out[l].astype(BF16),
    }


def kernel(x, ln_g, w_in, rwkv_mu, rwkv_w0, rwkv_w2, rwkv_a0, rwkv_a2, rwkv_k_k, rwkv_k_a, rwkv_r_k,
           rwkv_lnx_g, rwkv_lnx_b, lru_conv_w, lru_conv_b, lru_ga_w, lru_ga_b, lru_gx_w, lru_gx_b, lru_lam,
           lru_out_g, mla_q_norm_g, mla_w_uq, mla_kv_norm_g, mla_w_ukv, mla_out_g, w_out, final_g):
    batch, seq, _ = x.shape
    tm = min(512, seq)
    rope = _rope_tables(seq)
    fg = final_g.reshape(1, -1)
    h = x.reshape(batch * seq, D_MODEL)
    for l in range(DEPTH):
        prm = _layer_params(l, ln_g, w_in, rwkv_mu, rwkv_w0, rwkv_w2, rwkv_a0, rwkv_a2, rwkv_k_k, rwkv_k_a,
                            rwkv_r_k, rwkv_lnx_g, rwkv_lnx_b, lru_conv_w, lru_conv_b, lru_ga_w, lru_ga_b,
                            lru_gx_w, lru_gx_b, lru_lam, lru_out_g, mla_q_norm_g, mla_w_uq, mla_kv_norm_g,
                            mla_w_ukv, mla_out_g, w_out)
        proj2d = _inproj(h, prm["ln_g"], prm["w_in"], tm)
        proj = proj2d.reshape(batch, seq, D_PROJ)
        ya = _rwkv(proj, prm, batch, seq)
        yb = _lru(proj, prm, batch, seq, min(256, seq))
        blk = min(512, seq)
        q, k, vt = _mla_pro(proj, prm, rope, batch, seq, blk // 2)
        oc = _attn(q, k, vt, batch, seq, blk, blk // 2)
        h = _outproj(ya.reshape(batch * seq, GROUP_W), yb.reshape(batch * seq, GROUP_W),
                     oc.reshape(batch * seq, GROUP_W), proj2d, h, prm["mla_out_g"], prm["w_out"], fg,
                     l == DEPTH - 1, tm)
    return h.reshape(batch, seq, D_MODEL)
```

```python
import functools

import jax
import jax.numpy as jnp
from jax import lax
from jax.experimental import pallas as pl
from jax.experimental.pallas import tpu as pltpu

F32 = jnp.float32
BF16 = jnp.bfloat16

D_MODEL = 1024
DEPTH = 4
GROUP_W = 512
HEAD_DIM = 64
N_HEADS = 8
NORM_EPS = 1e-6
LORA = 64
GN_EPS = 64e-5
CONV_W = 4
LRU_C = 8.0
Q_LORA = 256
KV_LORA = 128
QK_NOPE = 64
QK_ROPE = 32
V_DIM = 64
ROPE_BASE = 10000.0

LANE = 128
SUBLANES = 8
HEAD_PAD = 128
RW_GROUP = 256
RW_CHUNK = 128
BF16_ROWS = 16
ATT_ONES_ROWS = BF16_ROWS
ATT_ACC_ROWS = V_DIM + ATT_ONES_ROWS
LOG2E = 1.4426950408889634
COL_R, COL_K, COL_V, COL_ZA, COL_UB, COL_ZB, COL_ZC = (i * GROUP_W for i in range(7))
COL_QLAT = 7 * GROUP_W
COL_KVLAT = COL_QLAT + Q_LORA
COL_KR = COL_KVLAT + KV_LORA
COL_WA = COL_KR + LANE
D_PROJ = COL_WA + LANE

VMEM_LIMIT = 56 * 1024 * 1024


def _cparams(sem):
    return pltpu.CompilerParams(dimension_semantics=sem, vmem_limit_bytes=VMEM_LIMIT)


def _bdot(a, b):
    return jnp.dot(a.astype(BF16), b.astype(BF16), preferred_element_type=F32)


def _bdot_nt(a, b):
    return lax.dot_general(a.astype(BF16), b.astype(BF16), (((1,), (1,)), ((), ())),
                           preferred_element_type=F32)


def _split_dot(x, m, passes):
    acc = None
    rem = x
    for _ in range(passes):
        part = rem.astype(BF16)
        term = jnp.dot(part, m, preferred_element_type=F32)
        acc = term if acc is None else acc + term
        rem = rem - part.astype(F32)
    return acc


def _silu(z):
    return z * jax.nn.sigmoid(z)


def _inproj_kernel(x_ref, g_ref, w_ref, o_ref):
    x = x_ref[...]
    ms = jnp.mean(x * x, axis=-1, keepdims=True)
    xn = x * lax.rsqrt(ms + NORM_EPS) * g_ref[...]
    o_ref[...] = _bdot(xn, w_ref[...])


def _inproj(h2d, g, w, tm):
    m = h2d.shape[0]
    return pl.pallas_call(
        _inproj_kernel,
        grid=(m // tm,),
        in_specs=[pl.BlockSpec((tm, D_MODEL), lambda i: (i, 0)),
                  pl.BlockSpec((1, D_MODEL), lambda i: (0, 0)),
                  pl.BlockSpec((D_MODEL, D_PROJ), lambda i: (0, 0))],
        out_specs=pl.BlockSpec((tm, D_PROJ), lambda i: (i, 0)),
        out_shape=jax.ShapeDtypeStruct((m, D_PROJ), F32),
        compiler_params=_cparams(("parallel",)),
        name="inproj",
    )(h2d, g, w)


def _split_dot_left(m, x):
    acc = None
    rem = x
    for _ in range(3):
        part = rem.astype(BF16)
        term = jnp.dot(m, part, preferred_element_type=F32)
        acc = term if acc is None else acc + term
        rem = rem - part.astype(F32)
    return acc


def _seg_sum(x, seg, passes):
    parts = [_split_dot(x[:, i:i + RW_GROUP], seg, passes) for i in range(0, x.shape[1], RW_GROUP)]
    return jnp.concatenate(parts, axis=1)


def _rwkv_kernel(r_ref, k_ref, v_ref, wa_ref, z_ref,
                 mur_ref, muk_ref, muv_ref, muwa_ref,
                 w0_ref, w2_ref, a0_ref, a2_ref, kk_ref, ka_ref, rk_ref, lng_ref, lnb_ref,
                 seg_ref, tri_ref,
                 o_ref,
                 state_ref, pr_ref, pk_ref, pv_ref, pwa_ref):
    c = pl.program_id(1)
    C = r_ref.shape[0]
    n_groups = GROUP_W // RW_GROUP

    @pl.when(c == 0)
    def _():
        state_ref[...] = jnp.zeros_like(state_ref)
        pr_ref[...] = jnp.zeros_like(pr_ref)
        pk_ref[...] = jnp.zeros_like(pk_ref)
        pv_ref[...] = jnp.zeros_like(pv_ref)
        pwa_ref[...] = jnp.zeros_like(pwa_ref)

    def lerp_shift(x_ref, prev_ref, mu_ref):
        x = x_ref[...]
        row = lax.broadcasted_iota(jnp.int32, x.shape, 0)
        xs = jnp.where(row == 0, prev_ref[...], pltpu.roll(x, 1, 0))
        prev_ref[...] = x[C - 1:C, :]
        return x + mu_ref[...] * (xs - x)

    r = lerp_shift(r_ref, pr_ref, mur_ref)
    k = lerp_shift(k_ref, pk_ref, muk_ref)
    v = lerp_shift(v_ref, pv_ref, muv_ref)
    wa = lerp_shift(wa_ref, pwa_ref, muwa_ref)

    log_w = -jax.nn.softplus(-(w0_ref[...] + _bdot(jnp.tanh(wa), w2_ref[...]))) - 0.5
    ld = -jnp.exp(log_w)
    a = jax.nn.sigmoid(a0_ref[...] + _bdot(wa, a2_ref[...]))
    seg = seg_ref[...]
    kk = k * kk_ref[...]
    kk = kk / jnp.maximum(jnp.sqrt(_seg_sum(kk * kk, seg, 3)), 1e-12)
    k = k * (1.0 + (a - 1.0) * ka_ref[...])
    bb = kk * a

    g = _split_dot_left(tri_ref[...], ld)
    g_last = g[C - 1:C, :]
    inv_g = jnp.exp(-g)
    to_end = jnp.exp(g_last - g)
    lhs = jnp.concatenate([r * jnp.exp(g), kk * jnp.exp(g - ld)], axis=0).astype(BF16)
    rhs = jnp.concatenate([k * inv_g, bb * inv_g], axis=0)
    kb_end = jnp.concatenate([k * to_end, -(bb * to_end)], axis=0)
    vb = v.astype(BF16)

    def grp(x, i):
        return x[:, i * RW_GROUP:(i + 1) * RW_GROUP]

    def pair(x, i):
        return x[:, i * LANE:(i + 1) * LANE]

    p = [lax.dot_general(grp(lhs, i), state_ref[i].astype(BF16), (((1,), (1,)), ((), ())),
                         preferred_element_type=F32) for i in range(n_groups)]
    p = jnp.concatenate(p, axis=1)

    lane2 = lax.broadcasted_iota(jnp.int32, (2 * C, LANE), 1)
    lane1 = lax.broadcasted_iota(jnp.int32, (C, LANE), 1)
    ti = lax.broadcasted_iota(jnp.int32, (C, C), 0)
    si = lax.broadcasted_iota(jnp.int32, (C, C), 1)
    incl = si <= ti
    strict = si < ti
    heads = range(N_HEADS)
    low = [hd % 2 == 0 for hd in heads]

    def own2(hd, x):
        return jnp.where((lane2 < HEAD_DIM) == low[hd], x, 0.0)

    amat = [_bdot_nt(pair(lhs, hd // 2), own2(hd, pair(rhs, hd // 2))) for hd in heads]
    a_y = [jnp.concatenate([jnp.where(incl, m[0:C, 0:C], 0.0), jnp.where(incl, -m[0:C, C:2 * C], 0.0)],
                           axis=1).astype(BF16) for m in amat]
    xs = [pair(p, hd // 2)[C:2 * C, :]
          + _bdot(jnp.where(strict, amat[hd][C:2 * C, 0:C], 0.0), pair(vb, hd // 2)) for hd in heads]
    npow = [jnp.where(strict, -amat[hd][C:2 * C, C:2 * C], 0.0).astype(BF16) for hd in heads]
    span = 1
    while span < C:
        lo = span if span % BF16_ROWS == 0 else 0
        if 2 * span < C:
            outs = [_bdot(n[lo:, :], jnp.concatenate([n, x.astype(BF16)], axis=1)) for n, x in zip(npow, xs)]
            npow = [jnp.concatenate([jnp.zeros((lo, C), BF16), o[:, 0:C].astype(BF16)], axis=0) if lo
                    else o[:, 0:C].astype(BF16) for o in outs]
            upd = [o[:, C:] for o in outs]
        else:
            upd = [_bdot(n[lo:, :], x) for n, x in zip(npow, xs)]
        xs = [jnp.concatenate([x[0:lo, :], x[lo:, :] + d], axis=0) if lo else x + d for x, d in zip(xs, upd)]
        span *= 2
    y_h = [_bdot(a_y[hd], jnp.concatenate([pair(vb, hd // 2), xs[hd].astype(BF16)], axis=0)) for hd in heads]

    first = lane1 < HEAD_DIM
    n_pairs = GROUP_W // LANE
    y = jnp.concatenate([jnp.where(first, y_h[2 * i], y_h[2 * i + 1]) for i in range(n_pairs)], axis=1)
    u = jnp.concatenate([jnp.where(first, xs[2 * i], xs[2 * i + 1]) for i in range(n_pairs)], axis=1)
    y = y + p[0:C, :]

    vu_t = jnp.concatenate([v, u], axis=0).T
    decay_all = jnp.exp(g_last)
    for i in range(n_groups):
        s_new = state_ref[i] * grp(decay_all, i) + _bdot(vu_t[i * RW_GROUP:(i + 1) * RW_GROUP, :], grp(kb_end, i))
        state_ref[i] = jnp.where(seg > 0.5, s_new, 0.0)

    inv_n = 1.0 / HEAD_DIM
    mean = _seg_sum(y, seg, 2) * inv_n
    yc = y - mean
    var = _seg_sum(yc * yc, seg, 2) * inv_n
    yn = yc * lax.rsqrt(var + GN_EPS) * lng_ref[...] + lnb_ref[...]
    bonus = _seg_sum(r * k * rk_ref[...], seg, 2) * v
    o_ref[...] = ((yn + bonus) * _silu(z_ref[...])).astype(o_ref.dtype)


def _rwkv(proj, prm, batch, seq):
    C = RW_CHUNK

    def col(base):
        return pl.BlockSpec((None, C, GROUP_W), lambda b, c, base=base: (b, c, base // GROUP_W))

    def vec(width=GROUP_W):
        return pl.BlockSpec((1, width), lambda b, c: (0, 0))

    def whole(shape):
        return pl.BlockSpec(shape, lambda b, c: tuple(0 for _ in shape))

    in_specs = [col(COL_R), col(COL_K), col(COL_V),
                pl.BlockSpec((None, C, LANE), lambda b, c: (b, c, COL_WA // LANE)),
                col(COL_ZA),
                vec(), vec(), vec(), vec(LANE),
                vec(), whole((LANE, GROUP_W)), vec(), whole((LANE, GROUP_W)),
                vec(), vec(), vec(), vec(), vec(),
                whole((RW_GROUP, RW_GROUP)), whole((C, C))]
    return pl.pallas_call(
        _rwkv_kernel,
        grid=(batch, seq // C),
        in_specs=in_specs,
        out_specs=pl.BlockSpec((None, C, GROUP_W), lambda b, c: (b, c, 0)),
        out_shape=jax.ShapeDtypeStruct((batch, seq, GROUP_W), BF16),
        scratch_shapes=[pltpu.VMEM((GROUP_W // RW_GROUP, RW_GROUP, RW_GROUP), F32),
                        pltpu.VMEM((1, GROUP_W), F32), pltpu.VMEM((1, GROUP_W), F32),
                        pltpu.VMEM((1, GROUP_W), F32), pltpu.VMEM((1, LANE), F32)],
        compiler_params=_cparams(("parallel", "arbitrary")),
        name="rwkv7",
    )(proj, proj, proj, proj, proj,
      prm["mu_r"], prm["mu_k"], prm["mu_v"], prm["mu_wa"],
      prm["w0"], prm["w2p"], prm["a0"], prm["a2p"], prm["k_k"], prm["k_a"], prm["r_k"],
      prm["lnx_g"], prm["lnx_b"], prm["seg"], prm["tri"])


def _lru_kernel(u_ref, z_ref, cw_ref, cb_ref, gaw_ref, gab_ref, gxw_ref, gxb_ref, lam_ref, og_ref,
                o_ref, tail_ref, hprev_ref):
    c = pl.program_id(1)
    ct = u_ref.shape[0]

    @pl.when(c == 0)
    def _():
        tail_ref[...] = jnp.zeros_like(tail_ref)
        hprev_ref[...] = jnp.zeros_like(hprev_ref)

    u = u_ref[...]
    tail = tail_ref[...]
    row8 = lax.broadcasted_iota(jnp.int32, tail.shape, 0)
    xc = cb_ref[...] + u * cw_ref[CONV_W - 1:CONV_W, :]
    for d in range(1, CONV_W):
        rolled = pltpu.roll(u, d, 0)
        first = jnp.where(row8 < d, pltpu.roll(tail, d, 0), rolled[0:8, :])
        ud = jnp.concatenate([first, rolled[8:, :]], axis=0)
        xc = xc + ud * cw_ref[CONV_W - 1 - d:CONV_W - d, :]
    tail_ref[...] = u[ct - 8:ct, :]

    gate_r = jax.nn.sigmoid(_bdot(xc, gaw_ref[...]) + gab_ref[...])
    gate_i = jax.nn.sigmoid(_bdot(xc, gxw_ref[...]) + gxb_ref[...])
    log_a = -LRU_C * gate_r * jax.nn.softplus(-lam_ref[...])
    acc_a = jnp.exp(log_a)
    th = jnp.tanh(log_a)
    acc_b = jnp.sqrt(-2.0 * th / (1.0 - th)) * (gate_i * xc)
    sub = lax.broadcasted_iota(jnp.int32, acc_a.shape, 0) % SUBLANES
    d = 1
    while d < SUBLANES:
        keep = sub >= d
        acc_b = acc_a * jnp.where(keep, pltpu.roll(acc_b, d, 0), 0.0) + acc_b
        acc_a = acc_a * jnp.where(keep, pltpu.roll(acc_a, d, 0), 1.0)
        d *= 2
    carry = hprev_ref[...]
    groups = []
    for i in range(ct // SUBLANES):
        rows = slice(i * SUBLANES, (i + 1) * SUBLANES)
        h_i = acc_a[rows, :] * carry + acc_b[rows, :]
        groups.append(h_i)
        carry = h_i[SUBLANES - 1:SUBLANES, :]
    hs = jnp.concatenate(groups, axis=0)
    hprev_ref[...] = hs[ct - 1:ct, :]
    ms = jnp.mean(hs * hs, axis=-1, keepdims=True)
    yb = hs * lax.rsqrt(ms + NORM_EPS) * og_ref[...]
    o_ref[...] = (yb * _silu(z_ref[...])).astype(o_ref.dtype)


def _lru(proj, prm, batch, seq, ct):
    def vec():
        return pl.BlockSpec((1, GROUP_W), lambda b, c: (0, 0))

    def mat(rows):
        return pl.BlockSpec((rows, GROUP_W), lambda b, c: (0, 0))

    return pl.pallas_call(
        _lru_kernel,
        grid=(batch, seq // ct),
        in_specs=[pl.BlockSpec((None, ct, GROUP_W), lambda b, c: (b, c, COL_UB // GROUP_W)),
                  pl.BlockSpec((None, ct, GROUP_W), lambda b, c: (b, c, COL_ZB // GROUP_W)),
                  mat(CONV_W), vec(), mat(GROUP_W), vec(), mat(GROUP_W), vec(), vec(), vec()],
        out_specs=pl.BlockSpec((None, ct, GROUP_W), lambda b, c: (b, c, 0)),
        out_shape=jax.ShapeDtypeStruct((batch, seq, GROUP_W), BF16),
        scratch_shapes=[pltpu.VMEM((8, GROUP_W), F32), pltpu.VMEM((1, GROUP_W), F32)],
        compiler_params=_cparams(("parallel", "arbitrary")),
        name="rglru",
    )(proj, proj, prm["conv_w"], prm["conv_b"], prm["ga_w"], prm["ga_b"], prm["gx_w"], prm["gx_b"],
      prm["lam"], prm["lru_out_g"])


def _mla_pro_kernel(ql_ref, kvl_ref, kr_ref, qg_ref, kvg_ref, wq_ref, wqs_ref, wk_ref, wv_ref, pk_ref,
                    cq_ref, sq_ref, ck_ref, sk_ref, q_ref, k_ref, v_ref):
    ql = ql_ref[...]
    qn = ql * lax.rsqrt(jnp.mean(ql * ql, axis=-1, keepdims=True) + NORM_EPS) * qg_ref[...]
    qn = qn.astype(BF16)
    cq = jnp.concatenate([cq_ref[...]] * N_HEADS, axis=1)
    sq = jnp.concatenate([sq_ref[...]] * N_HEADS, axis=1)
    q = jnp.dot(qn, wq_ref[...], preferred_element_type=F32) * cq
    q = q + jnp.dot(qn, wqs_ref[...], preferred_element_type=F32) * sq
    q_ref[...] = q.astype(q_ref.dtype)

    kvl = kvl_ref[...]
    kvn = kvl * lax.rsqrt(jnp.mean(kvl * kvl, axis=-1, keepdims=True) + NORM_EPS) * kvg_ref[...]
    kvn = kvn.astype(BF16)
    kr = kr_ref[...]
    roped = kr * ck_ref[...] + pltpu.roll(kr * sk_ref[...], LANE - QK_ROPE, 1)
    kfull = jnp.dot(kvn, wk_ref[...], preferred_element_type=F32) + _bdot(roped, pk_ref[...])
    k_ref[...] = kfull.astype(k_ref.dtype)
    vt = lax.dot_general(wv_ref[...], kvn, (((1,), (1,)), ((), ())), preferred_element_type=F32)
    kblk = v_ref.shape[-1]
    for i in range(v_ref.shape[0]):
        v_ref[i] = vt[:, i * kblk:(i + 1) * kblk].astype(v_ref.dtype)


def _mla_pro(proj, prm, rope, batch, seq, tm, kblk):
    qk_w = N_HEADS * HEAD_PAD

    def whole(shape):
        return pl.BlockSpec(shape, lambda b, t: tuple(0 for _ in shape))

    def tab():
        return pl.BlockSpec((tm, LANE), lambda b, t: (t, 0))

    return pl.pallas_call(
        _mla_pro_kernel,
        grid=(batch, seq // tm),
        in_specs=[pl.BlockSpec((None, tm, Q_LORA), lambda b, t: (b, t, COL_QLAT // Q_LORA)),
                  pl.BlockSpec((None, tm, KV_LORA), lambda b, t: (b, t, COL_KVLAT // KV_LORA)),
                  pl.BlockSpec((None, tm, LANE), lambda b, t: (b, t, COL_KR // LANE)),
                  whole((1, Q_LORA)), whole((1, KV_LORA)),
                  whole((Q_LORA, qk_w)), whole((Q_LORA, qk_w)), whole((KV_LORA, qk_w)),
                  whole((GROUP_W, KV_LORA)), whole((LANE, qk_w)),
                  tab(), tab(), tab(), tab()],
        out_specs=[pl.BlockSpec((None, tm, qk_w), lambda b, t: (b, t, 0)),
                   pl.BlockSpec((None, tm, qk_w), lambda b, t: (b, t, 0)),
                   pl.BlockSpec((None, tm // kblk, GROUP_W, kblk), lambda b, t: (b, t, 0, 0))],
        out_shape=[jax.ShapeDtypeStruct((batch, seq, qk_w), BF16),
                   jax.ShapeDtypeStruct((batch, seq, qk_w), BF16),
                   jax.ShapeDtypeStruct((batch, seq // kblk, GROUP_W, kblk), BF16)],
        compiler_params=_cparams(("parallel", "parallel")),
        name="mla_pro",
    )(proj, proj, proj, prm["q_norm_g"], prm["kv_norm_g"], prm["wq"], prm["wq_sw"], prm["wk"], prm["wv_t"],
      prm["k_place"], rope["cq"], rope["sq"], rope["ck"], rope["sk"])


def _attn_kernel(q_ref, k_ref, vt_ref, o_ref, m_ref, acc_ref, sa_ref, sb_ref, mxa_ref, mxb_ref, *, blk, kblk):
    qi = pl.program_id(2)
    m_ref[...] = jnp.full_like(m_ref, -jnp.inf)
    acc_ref[...] = jnp.zeros_like(acc_ref)
    qs = [q_ref[:, hh * HEAD_PAD:(hh + 1) * HEAD_PAD] for hh in range(2)]
    ones = jnp.ones((ATT_ONES_ROWS, kblk), BF16)
    key = lax.broadcasted_iota(jnp.int32, (kblk, blk), 0)
    qry = lax.broadcasted_iota(jnp.int32, (kblk, blk), 1)

    def scores(t, s_ref, mx_ref):
        off = pl.multiple_of(t * kblk, kblk)
        for hh in range(2):
            s = lax.dot_general(k_ref[pl.ds(off, kblk), hh * HEAD_PAD:(hh + 1) * HEAD_PAD], qs[hh],
                                (((1,), (1,)), ((), ())), preferred_element_type=F32)
            s_ref[hh] = s
            mx_ref[hh:hh + 1, :] = jnp.max(s, axis=0, keepdims=True)

    def consume(t, s_ref, mx_ref, key_shift):
        for hh in range(2):
            s = s_ref[hh]
            if key_shift is None:
                mx = mx_ref[hh:hh + 1, :]
            else:
                s = jnp.where(key + key_shift <= qry, s, -jnp.inf)
                mx = jnp.max(s, axis=0, keepdims=True)
            m_prev = m_ref[hh:hh + 1, :]
            m_new = jnp.maximum(m_prev, mx)
            alpha = jnp.exp2(m_prev - m_new)
            p = jnp.exp2(s - m_new).astype(BF16)
            m_ref[hh:hh + 1, :] = m_new
            rows = slice(hh * ATT_ACC_ROWS, (hh + 1) * ATT_ACC_ROWS)
            vt_ext = jnp.concatenate([vt_ref[t, hh * V_DIM:(hh + 1) * V_DIM, :], ones], axis=0)
            acc_ref[rows, :] = alpha * acc_ref[rows, :] + jnp.dot(vt_ext, p, preferred_element_type=F32)

    scores(0, sa_ref, mxa_ref)

    def pair(u, carry):
        t = 2 * u
        scores(t + 1, sb_ref, mxb_ref)
        consume(t, sa_ref, mxa_ref, None)
        scores(t + 2, sa_ref, mxa_ref)
        consume(t + 1, sb_ref, mxb_ref, None)
        return carry

    lax.fori_loop(0, qi, pair, 0)
    t = 2 * qi
    scores(t + 1, sb_ref, mxb_ref)
    consume(t, sa_ref, mxa_ref, 0)
    consume(t + 1, sb_ref, mxb_ref, kblk)
    outs = []
    for hh in range(2):
        base = hh * ATT_ACC_ROWS
        outs.append(acc_ref[base:base + V_DIM, :] / acc_ref[base + V_DIM:base + V_DIM + 1, :])
    o_ref[...] = jnp.concatenate(outs, axis=0).T


def _attn(q, k, vt, batch, seq, blk, kblk):
    pair_w = 2 * HEAD_PAD
    s_buf = pltpu.VMEM((2, kblk, blk), F32)
    mx_buf = pltpu.VMEM((SUBLANES, blk), F32)
    return pl.pallas_call(
        functools.partial(_attn_kernel, blk=blk, kblk=kblk),
        grid=(batch, N_HEADS // 2, seq // blk),
        in_specs=[pl.BlockSpec((None, blk, pair_w), lambda b, h, i: (b, i, h)),
                  pl.BlockSpec((None, seq, pair_w), lambda b, h, i: (b, 0, h)),
                  pl.BlockSpec((None, seq // kblk, LANE, kblk), lambda b, h, i: (b, 0, h, 0))],
        out_specs=pl.BlockSpec((None, blk, LANE), lambda b, h, i: (b, i, h)),
        out_shape=jax.ShapeDtypeStruct((batch, seq, GROUP_W), F32),
        scratch_shapes=[pltpu.VMEM((SUBLANES, blk), F32), pltpu.VMEM((2 * ATT_ACC_ROWS, blk), F32),
                        s_buf, s_buf, mx_buf, mx_buf],
        compiler_params=_cparams(("parallel", "parallel", "arbitrary")),
        name="mla_attn",
    )(q, k, vt)


def _outproj_kernel(ya_ref, yb_ref, oc_ref, zc_ref, gc_ref, w_ref, h_ref, fg_ref, o_ref, *, final):
    oc = oc_ref[...]
    yc = oc * lax.rsqrt(jnp.mean(oc * oc, axis=-1, keepdims=True) + NORM_EPS) * gc_ref[...]
    yc = yc * _silu(zc_ref[...])
    acc = h_ref[...]
    acc = acc + jnp.dot(ya_ref[...], w_ref[0:GROUP_W, :], preferred_element_type=F32)
    acc = acc + jnp.dot(yb_ref[...], w_ref[GROUP_W:2 * GROUP_W, :], preferred_element_type=F32)
    acc = acc + _bdot(yc, w_ref[2 * GROUP_W:3 * GROUP_W, :])
    if final:
        acc = acc * lax.rsqrt(jnp.mean(acc * acc, axis=-1, keepdims=True) + NORM_EPS) * fg_ref[...]
    o_ref[...] = acc


def _outproj(ya, yb, oc, proj, h2d, mla_out_g, w_out, final_g, final, tm):
    m = h2d.shape[0]

    def rows(width, blk=0):
        return pl.BlockSpec((tm, width), lambda i, blk=blk: (i, blk))

    return pl.pallas_call(
        functools.partial(_outproj_kernel, final=final),
        grid=(m // tm,),
        in_specs=[rows(GROUP_W), rows(GROUP_W), rows(GROUP_W), rows(GROUP_W, COL_ZC // GROUP_W),
                  pl.BlockSpec((1, GROUP_W), lambda i: (0, 0)),
                  pl.BlockSpec((3 * GROUP_W, D_MODEL), lambda i: (0, 0)),
                  rows(D_MODEL),
                  pl.BlockSpec((1, D_MODEL), lambda i: (0, 0))],
        out_specs=rows(D_MODEL),
        out_shape=jax.ShapeDtypeStruct((m, D_MODEL), F32),
        compiler_params=_cparams(("parallel",)),
        name="outproj_final" if final else "outproj",
    )(ya, yb, oc, proj, mla_out_g, w_out, h2d, final_g)


def _rope_tables(seq):
    half = QK_ROPE // 2
    inv_freq = ROPE_BASE ** (-jnp.arange(half, dtype=F32) * 2.0 / QK_ROPE)
    ang = jnp.arange(seq, dtype=F32)[:, None] * inv_freq[None, :]
    cos2 = jnp.concatenate([jnp.cos(ang)] * 2, axis=1)
    sin2 = jnp.concatenate([jnp.sin(ang)] * 2, axis=1)
    zeros = lambda w: jnp.zeros((seq, w), F32)
    ones = jnp.ones((seq, QK_NOPE), F32)
    return {
        "cq": jnp.concatenate([ones, cos2, zeros(HEAD_PAD - QK_NOPE - QK_ROPE)], axis=1),
        "sq": jnp.concatenate([zeros(QK_NOPE), sin2, zeros(HEAD_PAD - QK_NOPE - QK_ROPE)], axis=1),
        "ck": jnp.concatenate([cos2, zeros(LANE - QK_ROPE)], axis=1),
        "sk": jnp.concatenate([zeros(QK_ROPE), sin2, zeros(LANE - 2 * QK_ROPE)], axis=1),
    }


def _swap_halves(w):
    half = w.shape[-1] // 2
    return jnp.concatenate([-w[..., half:], w[..., :half]], axis=-1)


def _block_diag(w):
    h, n, _ = w.shape
    eye = jnp.eye(h, dtype=w.dtype)
    return (eye[:, None, :, None] * w[:, :, None, :]).reshape(h * n, h * n)


def _layer_params(l, ln_g, w_in, rwkv_mu, rwkv_w0, rwkv_w2, rwkv_a0, rwkv_a2, rwkv_k_k, rwkv_k_a, rwkv_r_k,
                  rwkv_lnx_g, rwkv_lnx_b, lru_conv_w, lru_conv_b, lru_ga_w, lru_ga_b, lru_gx_w, lru_gx_b,
                  lru_lam, lru_out_g, mla_q_norm_g, mla_w_uq, mla_kv_norm_g, mla_w_ukv, mla_out_g, w_out):
    G = GROUP_W
    w = w_in[l]
    o_ua, o_za = 0, 3 * G + 2 * LORA
    o_ub = o_za + G
    o_zb = o_ub + G
    o_ql = o_zb + G
    o_kv = o_ql + Q_LORA
    o_kr = o_kv + KV_LORA
    o_zc = o_kr + QK_ROPE
    sl = lambda a, n: w[:, a:a + n]
    kr_w = sl(o_kr, QK_ROPE)
    w_re = jnp.concatenate([
        sl(o_ua, G), sl(o_ua + G, G), sl(o_ua + 2 * G, G), sl(o_za, G), sl(o_ub, G), sl(o_zb, G), sl(o_zc, G),
        sl(o_ql, Q_LORA), sl(o_kv, KV_LORA),
        kr_w, _swap_halves(kr_w), jnp.zeros((D_MODEL, LANE - 2 * QK_ROPE), F32),
        sl(o_ua + 3 * G, 2 * LORA)], axis=1).astype(BF16)
    row = lambda v: v.reshape(1, -1)
    mu = rwkv_mu[l]
    zl = jnp.zeros((LORA, G), F32)
    head_id = jnp.arange(RW_GROUP) // HEAD_DIM
    t_idx = jnp.arange(RW_CHUNK)
    scale = (QK_NOPE + QK_ROPE) ** -0.5 * LOG2E
    wq3 = mla_w_uq[l].reshape(Q_LORA, N_HEADS, QK_NOPE + QK_ROPE) * scale
    zq = lambda n: jnp.zeros((Q_LORA, N_HEADS, n), F32)
    wq = jnp.concatenate([wq3, zq(HEAD_PAD - QK_NOPE - QK_ROPE)], axis=2)
    wq_sw = jnp.concatenate([zq(QK_NOPE), _swap_halves(wq3[:, :, QK_NOPE:]),
                             zq(HEAD_PAD - QK_NOPE - QK_ROPE)], axis=2)
    wkv3 = mla_w_ukv[l].reshape(KV_LORA, N_HEADS, QK_NOPE + V_DIM)
    wk = jnp.concatenate([wkv3[:, :, :QK_NOPE], jnp.zeros((KV_LORA, N_HEADS, HEAD_PAD - QK_NOPE), F32)], axis=2)
    place = jnp.zeros((LANE, N_HEADS, HEAD_PAD), F32)
    place = place.at[jnp.arange(QK_ROPE), :, QK_NOPE + jnp.arange(QK_ROPE)].set(1.0)
    return {
        "ln_g": row(ln_g[l]), "w_in": w_re,
        "mu_r": row(mu[0:G]), "mu_k": row(mu[G:2 * G]), "mu_v": row(mu[2 * G:3 * G]),
        "mu_wa": row(mu[3 * G:3 * G + 2 * LORA]),
        "w0": row(rwkv_w0[l]), "w2p": jnp.concatenate([rwkv_w2[l], zl], axis=0).astype(BF16),
        "a0": row(rwkv_a0[l]), "a2p": jnp.concatenate([zl, rwkv_a2[l]], axis=0).astype(BF16),
        "k_k": row(rwkv_k_k[l]), "k_a": row(rwkv_k_a[l]), "r_k": row(rwkv_r_k[l]),
        "lnx_g": row(rwkv_lnx_g[l]), "lnx_b": row(rwkv_lnx_b[l]),
        "seg": (head_id[:, None] == head_id[None, :]).astype(BF16),
        "tri": (t_idx[None, :] <= t_idx[:, None]).astype(BF16),
        "conv_w": lru_conv_w[l], "conv_b": row(lru_conv_b[l]),
        "ga_w": _block_diag(lru_ga_w[l]).astype(BF16), "ga_b": row(lru_ga_b[l]),
        "gx_w": _block_diag(lru_gx_w[l]).astype(BF16), "gx_b": row(lru_gx_b[l]),
        "lam": row(lru_lam[l]), "lru_out_g": row(lru_out_g[l]),
        "q_norm_g": row(mla_q_norm_g[l]), "kv_norm_g": row(mla_kv_norm_g[l]),
        "wq": wq.reshape(Q_LORA, -1).astype(BF16), "wq_sw": wq_sw.reshape(Q_LORA, -1).astype(BF16),
        "wk": wk.reshape(KV_LORA, -1).astype(BF16),
        "wv_t": wkv3[:, :, QK_NOPE:].reshape(KV_LORA, -1).T.astype(BF16),
        "k_place": place.reshape(LANE, -1).astype(BF16),
        "mla_out_g": row(mla_out_g[l]), "w_out": w_out[l].astype(BF16),
    }


def kernel(x, ln_g, w_in, rwkv_mu, rwkv_w0, rwkv_w2, rwkv_a0, rwkv_a2, rwkv_k_k, rwkv_k_a, rwkv_r_k,
           rwkv_lnx_g, rwkv_lnx_b, lru_conv_w, lru_conv_b, lru_ga_w, lru_ga_b, lru_gx_w, lru_gx_b, lru_lam,
           lru_out_g, mla_q_norm_g, mla_w_uq, mla_kv_norm_g, mla_w_ukv, mla_out_g, w_out, final_g):
    batch, seq, _ = x.shape
    tm = min(512, seq)
    rope = _rope_tables(seq)
    fg = final_g.reshape(1, -1)
    h = x.reshape(batch * seq, D_MODEL)
    for l in range(DEPTH):
        prm = _layer_params(l, ln_g, w_in, rwkv_mu, rwkv_w0, rwkv_w2, rwkv_a0, rwkv_a2, rwkv_k_k, rwkv_k_a,
                            rwkv_r_k, rwkv_lnx_g, rwkv_lnx_b, lru_conv_w, lru_conv_b, lru_ga_w, lru_ga_b,
                            lru_gx_w, lru_gx_b, lru_lam, lru_out_g, mla_q_norm_g, mla_w_uq, mla_kv_norm_g,
                            mla_w_ukv, mla_out_g, w_out)
        proj2d = _inproj(h, prm["ln_g"], prm["w_in"], tm)
        proj = proj2d.reshape(batch, seq, D_PROJ)
        ya = _rwkv(proj, prm, batch, seq)
        yb = _lru(proj, prm, batch, seq, min(256, seq))
        blk = min(512, seq)
        q, k, vt = _mla_pro(proj, prm, rope, batch, seq, blk, blk // 2)
        oc = _attn(q, k, vt, batch, seq, blk, blk // 2)
        h = _outproj(ya.reshape(batch * seq, GROUP_W), yb.reshape(batch * seq, GROUP_W),
                     oc.reshape(batch * seq, GROUP_W), proj2d, h, prm["mla_out_g"], prm["w_out"], fg,
                     l == DEPTH - 1, tm)
    return h.reshape(batch, seq, D_MODEL)
```

```python
import functools

import jax
import jax.numpy as jnp
from jax import lax
from jax.experimental import pallas as pl
from jax.experimental.pallas import tpu as pltpu

F32 = jnp.float32
BF16 = jnp.bfloat16

D_MODEL = 1024
DEPTH = 4
GROUP_W = 512
HEAD_DIM = 64
N_HEADS = 8
NORM_EPS = 1e-6
LORA = 64
GN_EPS = 64e-5
CONV_W = 4
LRU_C = 8.0
Q_LORA = 256
KV_LORA = 128
QK_NOPE = 64
QK_ROPE = 32
V_DIM = 64
ROPE_BASE = 10000.0

LANE = 128
SUBLANES = 8
HEAD_PAD = 128
RW_GROUP = 256
RW_CHUNK = 128
RW_STREAMS = 2
BF16_ROWS = 16
ATT_ONES_ROWS = BF16_ROWS
ATT_ACC_ROWS = V_DIM + ATT_ONES_ROWS
LOG2E = 1.4426950408889634
COL_R, COL_K, COL_V, COL_ZA, COL_UB, COL_ZB, COL_ZC = (i * GROUP_W for i in range(7))
COL_QLAT = 7 * GROUP_W
COL_KVLAT = COL_QLAT + Q_LORA
COL_KR = COL_KVLAT + KV_LORA
COL_WA = COL_KR + LANE
D_PROJ = COL_WA + LANE

VMEM_LIMIT = 56 * 1024 * 1024


def _cparams(sem):
    return pltpu.CompilerParams(dimension_semantics=sem, vmem_limit_bytes=VMEM_LIMIT)


def _bdot(a, b):
    return jnp.dot(a.astype(BF16), b.astype(BF16), preferred_element_type=F32)


def _bdot_nt(a, b):
    return lax.dot_general(a.astype(BF16), b.astype(BF16), (((1,), (1,)), ((), ())),
                           preferred_element_type=F32)


def _split_dot(x, m, passes):
    acc = None
    rem = x
    for i in range(passes):
        part = rem.astype(BF16)
        term = jnp.dot(part, m, preferred_element_type=F32)
        acc = term if acc is None else acc + term
        if i + 1 < passes:
            rem = rem - part.astype(F32)
    return acc


def _silu(z):
    return z * jax.nn.sigmoid(z)


def _inproj_kernel(x_ref, g_ref, w_ref, o_ref):
    x = x_ref[...]
    ms = jnp.mean(x * x, axis=-1, keepdims=True)
    xn = x * lax.rsqrt(ms + NORM_EPS) * g_ref[...]
    o_ref[...] = _bdot(xn, w_ref[...])


def _inproj(h2d, g, w, tm):
    m = h2d.shape[0]
    return pl.pallas_call(
        _inproj_kernel,
        grid=(m // tm,),
        in_specs=[pl.BlockSpec((tm, D_MODEL), lambda i: (i, 0)),
                  pl.BlockSpec((1, D_MODEL), lambda i: (0, 0)),
                  pl.BlockSpec((D_MODEL, D_PROJ), lambda i: (0, 0))],
        out_specs=pl.BlockSpec((tm, D_PROJ), lambda i: (i, 0)),
        out_shape=jax.ShapeDtypeStruct((m, D_PROJ), F32),
        compiler_params=_cparams(("parallel",)),
        name="inproj",
    )(h2d, g, w)


def _split_dot_left(m, x, passes):
    acc = None
    rem = x
    for i in range(passes):
        part = rem.astype(BF16)
        term = jnp.dot(m, part, preferred_element_type=F32)
        acc = term if acc is None else acc + term
        if i + 1 < passes:
            rem = rem - part.astype(F32)
    return acc


def _seg_sum(x, seg, passes):
    parts = [_split_dot(x[:, i:i + RW_GROUP], seg, passes) for i in range(0, x.shape[1], RW_GROUP)]
    return jnp.concatenate(parts, axis=1)


class _RwkvStream:
    def __init__(self, bi, refs, shared):
        self.bi = bi
        self.refs = refs
        self.sh = shared

    def _lerp_shift(self, x_ref, prev_ref, mu_ref):
        C = self.sh["C"]
        x = x_ref[self.bi]
        row = lax.broadcasted_iota(jnp.int32, x.shape, 0)
        xs = jnp.where(row == 0, prev_ref[self.bi], pltpu.roll(x, 1, 0))
        prev_ref[self.bi] = x[C - 1:C, :]
        return x + mu_ref[...] * (xs - x)

    def shift(self):
        f = self.refs
        self.r = self._lerp_shift(f["r"], f["pr"], f["mur"])
        self.k = self._lerp_shift(f["k"], f["pk"], f["muk"])
        self.v = self._lerp_shift(f["v"], f["pv"], f["muv"])
        self.wa = self._lerp_shift(f["wa"], f["pwa"], f["muwa"])

    def mm_lora(self):
        f = self.refs
        self.lw = _bdot(jnp.tanh(self.wa), f["w2"][...])
        self.la = _bdot(self.wa, f["a2"][...])

    def decay(self):
        f = self.refs
        log_w = -jax.nn.softplus(-(f["w0"][...] + self.lw)) - 0.5
        self.ld = -jnp.exp(log_w)
        self.a = jax.nn.sigmoid(f["a0"][...] + self.la)
        self.kk = self.k * f["kk"][...]

    def mm_norm_cumsum(self):
        self.n2 = _seg_sum(self.kk * self.kk, self.sh["seg"], 2)
        self.g = _split_dot_left(self.sh["tri"], self.ld, 2)

    def build(self):
        f, C = self.refs, self.sh["C"]
        kk = self.kk / jnp.maximum(jnp.sqrt(self.n2), 1e-12)
        k = self.k * (1.0 + (self.a - 1.0) * f["ka"][...])
        bb = kk * self.a
        g, ld = self.g, self.ld
        self.g_last = g[C - 1:C, :]
        inv_g = jnp.exp(-g)
        to_end = jnp.exp(self.g_last - g)
        self.lhs = jnp.concatenate([self.r * jnp.exp(g), kk * jnp.exp(g - ld)], axis=0).astype(BF16)
        self.rhs = jnp.concatenate([k * inv_g, bb * inv_g], axis=0)
        self.kb_end = jnp.concatenate([k * to_end, -(bb * to_end)], axis=0)
        self.vb = self.v.astype(BF16)
        self.kmod = k

    @staticmethod
    def _grp(x, i):
        return x[:, i * RW_GROUP:(i + 1) * RW_GROUP]

    @staticmethod
    def _pair(x, i):
        return x[:, i * LANE:(i + 1) * LANE]

    def mm_state_in(self):
        st = self.refs["state"]
        p = [lax.dot_general(self._grp(self.lhs, i), st[self.bi, i].astype(BF16), (((1,), (1,)), ((), ())),
                             preferred_element_type=F32) for i in range(GROUP_W // RW_GROUP)]
        self.p = jnp.concatenate(p, axis=1)

    def mm_amat(self):
        lane2 = self.sh["lane2"]
        pair = self._pair

        def own2(hd, x):
            return jnp.where((lane2 < HEAD_DIM) == (hd % 2 == 0), x, 0.0)

        self.amat = [_bdot_nt(pair(self.lhs, hd // 2), own2(hd, pair(self.rhs, hd // 2)))
                     for hd in range(N_HEADS)]

    def mm_xinit(self):
        C, incl, strict, pair = self.sh["C"], self.sh["incl"], self.sh["strict"], self._pair
        self.a_y = [jnp.concatenate([jnp.where(incl, m[0:C, 0:C], 0.0), jnp.where(incl, -m[0:C, C:2 * C], 0.0)],
                                    axis=1).astype(BF16) for m in self.amat]
        self.xs = [pair(self.p, hd // 2)[C:2 * C, :]
                   + _bdot(jnp.where(strict, self.amat[hd][C:2 * C, 0:C], 0.0), pair(self.vb, hd // 2))
                   for hd in range(N_HEADS)]
        self.npow = [jnp.where(strict, -self.amat[hd][C:2 * C, C:2 * C], 0.0).astype(BF16)
                     for hd in range(N_HEADS)]
        self.span = 1

    def mm_level(self):
        C, span = self.sh["C"], self.span
        lo = span if span % BF16_ROWS == 0 else 0
        if 2 * span < C:
            outs = [_bdot(n[lo:, :], jnp.concatenate([n, x.astype(BF16)], axis=1))
                    for n, x in zip(self.npow, self.xs)]
            self.npow = [jnp.concatenate([jnp.zeros((lo, C), BF16), o[:, 0:C].astype(BF16)], axis=0) if lo
                         else o[:, 0:C].astype(BF16) for o in outs]
            upd = [o[:, C:] for o in outs]
        else:
            upd = [_bdot(n[lo:, :], x) for n, x in zip(self.npow, self.xs)]
        self.xs = [jnp.concatenate([x[0:lo, :], x[lo:, :] + d], axis=0) if lo else x + d
                   for x, d in zip(self.xs, upd)]
        self.span = 2 * span

    def mm_y(self):
        C, pair = self.sh["C"], self._pair
        y_h = [_bdot(self.a_y[hd], jnp.concatenate([pair(self.vb, hd // 2), self.xs[hd].astype(BF16)], axis=0))
               for hd in range(N_HEADS)]
        first = self.sh["lane1"] < HEAD_DIM
        n_pairs = GROUP_W // LANE
        y = jnp.concatenate([jnp.where(first, y_h[2 * i], y_h[2 * i + 1]) for i in range(n_pairs)], axis=1)
        self.u = jnp.concatenate([jnp.where(first, self.xs[2 * i], self.xs[2 * i + 1]) for i in range(n_pairs)],
                                 axis=1)
        self.y = y + self.p[0:C, :]

    def mm_state_out(self):
        st, seg = self.refs["state"], self.sh["seg"]
        vu_t = jnp.concatenate([self.v, self.u], axis=0).T
        decay_all = jnp.exp(self.g_last)
        for i in range(GROUP_W // RW_GROUP):
            s_new = (st[self.bi, i] * self._grp(decay_all, i)
                     + _bdot(vu_t[i * RW_GROUP:(i + 1) * RW_GROUP, :], self._grp(self.kb_end, i)))
            st[self.bi, i] = jnp.where(seg > 0.5, s_new, 0.0)

    def mm_mean(self):
        self.mean = _seg_sum(self.y, self.sh["seg"], 1) * (1.0 / HEAD_DIM)

    def mm_var_bonus(self):
        f, seg = self.refs, self.sh["seg"]
        self.yc = self.y - self.mean
        self.var = _seg_sum(self.yc * self.yc, seg, 1) * (1.0 / HEAD_DIM)
        self.bonus = _seg_sum(self.r * self.kmod * f["rk"][...], seg, 1) * self.v

    def finish(self):
        f = self.refs
        yn = self.yc * lax.rsqrt(self.var + GN_EPS) * f["lng"][...] + f["lnb"][...]
        f["o"][self.bi] = ((yn + self.bonus) * _silu(f["z"][self.bi])).astype(f["o"].dtype)


def _rwkv_kernel(r_ref, k_ref, v_ref, wa_ref, z_ref,
                 mur_ref, muk_ref, muv_ref, muwa_ref,
                 w0_ref, w2_ref, a0_ref, a2_ref, kk_ref, ka_ref, rk_ref, lng_ref, lnb_ref,
                 seg_ref, tri_ref,
                 o_ref,
                 state_ref, pr_ref, pk_ref, pv_ref, pwa_ref):
    c = pl.program_id(1)
    C = r_ref.shape[1]

    @pl.when(c == 0)
    def _():
        state_ref[...] = jnp.zeros_like(state_ref)
        pr_ref[...] = jnp.zeros_like(pr_ref)
        pk_ref[...] = jnp.zeros_like(pk_ref)
        pv_ref[...] = jnp.zeros_like(pv_ref)
        pwa_ref[...] = jnp.zeros_like(pwa_ref)

    ti = lax.broadcasted_iota(jnp.int32, (C, C), 0)
    si = lax.broadcasted_iota(jnp.int32, (C, C), 1)
    shared = {"C": C, "seg": seg_ref[...], "tri": tri_ref[...], "incl": si <= ti, "strict": si < ti,
              "lane2": lax.broadcasted_iota(jnp.int32, (2 * C, LANE), 1),
              "lane1": lax.broadcasted_iota(jnp.int32, (C, LANE), 1)}
    refs = {"r": r_ref, "k": k_ref, "v": v_ref, "wa": wa_ref, "z": z_ref, "o": o_ref,
            "mur": mur_ref, "muk": muk_ref, "muv": muv_ref, "muwa": muwa_ref,
            "w0": w0_ref, "w2": w2_ref, "a0": a0_ref, "a2": a2_ref, "kk": kk_ref, "ka": ka_ref, "rk": rk_ref,
            "lng": lng_ref, "lnb": lnb_ref, "state": state_ref,
            "pr": pr_ref, "pk": pk_ref, "pv": pv_ref, "pwa": pwa_ref}
    a, b = _RwkvStream(0, refs, shared), _RwkvStream(1, refs, shared)
    n_levels = C.bit_length() - 1

    a.shift()
    a.mm_lora()
    a.decay()
    a.mm_norm_cumsum()
    a.build()
    a.mm_state_in()
    a.mm_amat()
    b.shift()
    b.mm_lora()
    b.decay()
    a.mm_xinit()
    b.mm_norm_cumsum()
    b.build()
    for _ in range(n_levels):
        a.mm_level()
    a.mm_y()
    a.mm_state_out()
    b.mm_state_in()
    b.mm_amat()
    a.mm_mean()
    b.mm_xinit()
    a.mm_var_bonus()
    a.finish()
    for _ in range(n_levels):
        b.mm_level()
    b.mm_y()
    b.mm_state_out()
    b.mm_mean()
    b.mm_var_bonus()
    b.finish()


def _rwkv(proj, prm, batch, seq):
    C = RW_CHUNK
    nb = RW_STREAMS

    def col(base):
        return pl.BlockSpec((nb, C, GROUP_W), lambda b, c, base=base: (b, c, base // GROUP_W))

    def vec(width=GROUP_W):
        return pl.BlockSpec((1, width), lambda b, c: (0, 0))

    def whole(shape):
        return pl.BlockSpec(shape, lambda b, c: tuple(0 for _ in shape))

    in_specs = [col(COL_R), col(COL_K), col(COL_V),
                pl.BlockSpec((nb, C, LANE), lambda b, c: (b, c, COL_WA // LANE)),
                col(COL_ZA),
                vec(), vec(), vec(), vec(LANE),
                vec(), whole((LANE, GROUP_W)), vec(), whole((LANE, GROUP_W)),
                vec(), vec(), vec(), vec(), vec(),
                whole((RW_GROUP, RW_GROUP)), whole((C, C))]
    return pl.pallas_call(
        _rwkv_kernel,
        grid=(batch // nb, seq // C),
        in_specs=in_specs,
        out_specs=pl.BlockSpec((nb, C, GROUP_W), lambda b, c: (b, c, 0)),
        out_shape=jax.ShapeDtypeStruct((batch, seq, GROUP_W), BF16),
        scratch_shapes=[pltpu.VMEM((nb, GROUP_W // RW_GROUP, RW_GROUP, RW_GROUP), F32),
                        pltpu.VMEM((nb, 1, GROUP_W), F32), pltpu.VMEM((nb, 1, GROUP_W), F32),
                        pltpu.VMEM((nb, 1, GROUP_W), F32), pltpu.VMEM((nb, 1, LANE), F32)],
        compiler_params=_cparams(("parallel", "arbitrary")),
        name="rwkv7",
    )(proj, proj, proj, proj, proj,
      prm["mu_r"], prm["mu_k"], prm["mu_v"], prm["mu_wa"],
      prm["w0"], prm["w2p"], prm["a0"], prm["a2p"], prm["k_k"], prm["k_a"], prm["r_k"],
      prm["lnx_g"], prm["lnx_b"], prm["seg"], prm["tri"])


def _lru_kernel(u_ref, z_ref, cw_ref, cb_ref, gaw_ref, gab_ref, gxw_ref, gxb_ref, lam_ref, og_ref,
                o_ref, tail_ref, hprev_ref):
    c = pl.program_id(1)
    ct = u_ref.shape[0]

    @pl.when(c == 0)
    def _():
        tail_ref[...] = jnp.zeros_like(tail_ref)
        hprev_ref[...] = jnp.zeros_like(hprev_ref)

    u = u_ref[...]
    tail = tail_ref[...]
    row8 = lax.broadcasted_iota(jnp.int32, tail.shape, 0)
    xc = cb_ref[...] + u * cw_ref[CONV_W - 1:CONV_W, :]
    for d in range(1, CONV_W):
        rolled = pltpu.roll(u, d, 0)
        first = jnp.where(row8 < d, pltpu.roll(tail, d, 0), rolled[0:8, :])
        ud = jnp.concatenate([first, rolled[8:, :]], axis=0)
        xc = xc + ud * cw_ref[CONV_W - 1 - d:CONV_W - d, :]
    tail_ref[...] = u[ct - 8:ct, :]

    gate_r = jax.nn.sigmoid(_bdot(xc, gaw_ref[...]) + gab_ref[...])
    gate_i = jax.nn.sigmoid(_bdot(xc, gxw_ref[...]) + gxb_ref[...])
    log_a = -LRU_C * gate_r * jax.nn.softplus(-lam_ref[...])
    acc_a = jnp.exp(log_a)
    th = jnp.tanh(log_a)
    acc_b = jnp.sqrt(-2.0 * th / (1.0 - th)) * (gate_i * xc)
    sub = lax.broadcasted_iota(jnp.int32, acc_a.shape, 0) % SUBLANES
    d = 1
    while d < SUBLANES:
        keep = sub >= d
        acc_b = acc_a * jnp.where(keep, pltpu.roll(acc_b, d, 0), 0.0) + acc_b
        acc_a = acc_a * jnp.where(keep, pltpu.roll(acc_a, d, 0), 1.0)
        d *= 2
    carry = hprev_ref[...]
    groups = []
    for i in range(ct // SUBLANES):
        rows = slice(i * SUBLANES, (i + 1) * SUBLANES)
        h_i = acc_a[rows, :] * carry + acc_b[rows, :]
        groups.append(h_i)
        carry = h_i[SUBLANES - 1:SUBLANES, :]
    hs = jnp.concatenate(groups, axis=0)
    hprev_ref[...] = hs[ct - 1:ct, :]
    ms = jnp.mean(hs * hs, axis=-1, keepdims=True)
    yb = hs * lax.rsqrt(ms + NORM_EPS) * og_ref[...]
    o_ref[...] = (yb * _silu(z_ref[...])).astype(o_ref.dtype)


def _lru(proj, prm, batch, seq, ct):
    def vec():
        return pl.BlockSpec((1, GROUP_W), lambda b, c: (0, 0))

    def mat(rows):
        return pl.BlockSpec((rows, GROUP_W), lambda b, c: (0, 0))

    return pl.pallas_call(
        _lru_kernel,
        grid=(batch, seq // ct),
        in_specs=[pl.BlockSpec((None, ct, GROUP_W), lambda b, c: (b, c, COL_UB // GROUP_W)),
                  pl.BlockSpec((None, ct, GROUP_W), lambda b, c: (b, c, COL_ZB // GROUP_W)),
                  mat(CONV_W), vec(), mat(GROUP_W), vec(), mat(GROUP_W), vec(), vec(), vec()],
        out_specs=pl.BlockSpec((None, ct, GROUP_W), lambda b, c: (b, c, 0)),
        out_shape=jax.ShapeDtypeStruct((batch, seq, GROUP_W), BF16),
        scratch_shapes=[pltpu.VMEM((8, GROUP_W), F32), pltpu.VMEM((1, GROUP_W), F32)],
        compiler_params=_cparams(("parallel", "arbitrary")),
        name="rglru",
    )(proj, proj, prm["conv_w"], prm["conv_b"], prm["ga_w"], prm["ga_b"], prm["gx_w"], prm["gx_b"],
      prm["lam"], prm["lru_out_g"])


def _mla_pro_kernel(ql_ref, kvl_ref, kr_ref, qg_ref, kvg_ref, wq_ref, wqs_ref, wk_ref, wv_ref, pk_ref,
                    cq_ref, sq_ref, ck_ref, sk_ref, q_ref, k_ref, v_ref):
    ql = ql_ref[...]
    qn = ql * lax.rsqrt(jnp.mean(ql * ql, axis=-1, keepdims=True) + NORM_EPS) * qg_ref[...]
    qn = qn.astype(BF16)
    cq = jnp.concatenate([cq_ref[...]] * N_HEADS, axis=1)
    sq = jnp.concatenate([sq_ref[...]] * N_HEADS, axis=1)
    q = jnp.dot(qn, wq_ref[...], preferred_element_type=F32) * cq
    q = q + jnp.dot(qn, wqs_ref[...], preferred_element_type=F32) * sq
    q_ref[...] = q.astype(q_ref.dtype)

    kvl = kvl_ref[...]
    kvn = kvl * lax.rsqrt(jnp.mean(kvl * kvl, axis=-1, keepdims=True) + NORM_EPS) * kvg_ref[...]
    kvn = kvn.astype(BF16)
    kr = kr_ref[...]
    roped = kr * ck_ref[...] + pltpu.roll(kr * sk_ref[...], LANE - QK_ROPE, 1)
    kfull = jnp.dot(kvn, wk_ref[...], preferred_element_type=F32) + _bdot(roped, pk_ref[...])
    k_ref[...] = kfull.astype(k_ref.dtype)
    vt = lax.dot_general(wv_ref[...], kvn, (((1,), (1,)), ((), ())), preferred_element_type=F32)
    kblk = v_ref.shape[-1]
    for i in range(v_ref.shape[0]):
        v_ref[i] = vt[:, i * kblk:(i + 1) * kblk].astype(v_ref.dtype)


def _mla_pro(proj, prm, rope, batch, seq, tm, kblk):
    qk_w = N_HEADS * HEAD_PAD

    def whole(shape):
        return pl.BlockSpec(shape, lambda b, t: tuple(0 for _ in shape))

    def tab():
        return pl.BlockSpec((tm, LANE), lambda b, t: (t, 0))

    return pl.pallas_call(
        _mla_pro_kernel,
        grid=(batch, seq // tm),
        in_specs=[pl.BlockSpec((None, tm, Q_LORA), lambda b, t: (b, t, COL_QLAT // Q_LORA)),
                  pl.BlockSpec((None, tm, KV_LORA), lambda b, t: (b, t, COL_KVLAT // KV_LORA)),
                  pl.BlockSpec((None, tm, LANE), lambda b, t: (b, t, COL_KR // LANE)),
                  whole((1, Q_LORA)), whole((1, KV_LORA)),
                  whole((Q_LORA, qk_w)), whole((Q_LORA, qk_w)), whole((KV_LORA, qk_w)),
                  whole((GROUP_W, KV_LORA)), whole((LANE, qk_w)),
                  tab(), tab(), tab(), tab()],
        out_specs=[pl.BlockSpec((None, tm, qk_w), lambda b, t: (b, t, 0)),
                   pl.BlockSpec((None, tm, qk_w), lambda b, t: (b, t, 0)),
                   pl.BlockSpec((None, tm // kblk, GROUP_W, kblk), lambda b, t: (b, t, 0, 0))],
        out_shape=[jax.ShapeDtypeStruct((batch, seq, qk_w), BF16),
                   jax.ShapeDtypeStruct((batch, seq, qk_w), BF16),
                   jax.ShapeDtypeStruct((batch, seq // kblk, GROUP_W, kblk), BF16)],
        compiler_params=_cparams(("parallel", "parallel")),
        name="mla_pro",
    )(proj, proj, proj, prm["q_norm_g"], prm["kv_norm_g"], prm["wq"], prm["wq_sw"], prm["wk"], prm["wv_t"],
      prm["k_place"], rope["cq"], rope["sq"], rope["ck"], rope["sk"])


def _attn_kernel(q_ref, k_ref, vt_ref, o_ref, m_ref, acc_ref, sa_ref, sb_ref, mxa_ref, mxb_ref, *, blk, kblk):
    qi = pl.program_id(2)
    m_ref[...] = jnp.full_like(m_ref, -jnp.inf)
    acc_ref[...] = jnp.zeros_like(acc_ref)
    qs = [q_ref[:, hh * HEAD_PAD:(hh + 1) * HEAD_PAD] for hh in range(2)]
    ones = jnp.ones((ATT_ONES_ROWS, kblk), BF16)
    key = lax.broadcasted_iota(jnp.int32, (kblk, blk), 0)
    qry = lax.broadcasted_iota(jnp.int32, (kblk, blk), 1)

    def scores(t, s_ref, mx_ref):
        off = pl.multiple_of(t * kblk, kblk)
        for hh in range(2):
            s = lax.dot_general(k_ref[pl.ds(off, kblk), hh * HEAD_PAD:(hh + 1) * HEAD_PAD], qs[hh],
                                (((1,), (1,)), ((), ())), preferred_element_type=F32)
            s_ref[hh] = s
            mx_ref[hh:hh + 1, :] = jnp.max(s, axis=0, keepdims=True)

    def consume(t, s_ref, mx_ref, key_shift):
        for hh in range(2):
            s = s_ref[hh]
            if key_shift is None:
                mx = mx_ref[hh:hh + 1, :]
            else:
                s = jnp.where(key + key_shift <= qry, s, -jnp.inf)
                mx = jnp.max(s, axis=0, keepdims=True)
            m_prev = m_ref[hh:hh + 1, :]
            m_new = jnp.maximum(m_prev, mx)
            alpha = jnp.exp2(m_prev - m_new)
            p = jnp.exp2(s - m_new).astype(BF16)
            m_ref[hh:hh + 1, :] = m_new
            rows = slice(hh * ATT_ACC_ROWS, (hh + 1) * ATT_ACC_ROWS)
            vt_ext = jnp.concatenate([vt_ref[t, hh * V_DIM:(hh + 1) * V_DIM, :], ones], axis=0)
            acc_ref[rows, :] = alpha * acc_ref[rows, :] + jnp.dot(vt_ext, p, preferred_element_type=F32)

    scores(0, sa_ref, mxa_ref)

    def pair(u, carry):
        t = 2 * u
        scores(t + 1, sb_ref, mxb_ref)
        consume(t, sa_ref, mxa_ref, None)
        scores(t + 2, sa_ref, mxa_ref)
        consume(t + 1, sb_ref, mxb_ref, None)
        return carry

    lax.fori_loop(0, qi, pair, 0)
    t = 2 * qi
    scores(t + 1, sb_ref, mxb_ref)
    consume(t, sa_ref, mxa_ref, 0)
    consume(t + 1, sb_ref, mxb_ref, kblk)
    outs = []
    for hh in range(2):
        base = hh * ATT_ACC_ROWS
        outs.append(acc_ref[base:base + V_DIM, :] / acc_ref[base + V_DIM:base + V_DIM + 1, :])
    o_ref[...] = jnp.concatenate(outs, axis=0).T


def _attn(q, k, vt, batch, seq, blk, kblk):
    pair_w = 2 * HEAD_PAD
    s_buf = pltpu.VMEM((2, kblk, blk), F32)
    mx_buf = pltpu.VMEM((SUBLANES, blk), F32)
    return pl.pallas_call(
        functools.partial(_attn_kernel, blk=blk, kblk=kblk),
        grid=(batch, N_HEADS // 2, seq // blk),
        in_specs=[pl.BlockSpec((None, blk, pair_w), lambda b, h, i: (b, i, h)),
                  pl.BlockSpec((None, seq, pair_w), lambda b, h, i: (b, 0, h)),
                  pl.BlockSpec((None, seq // kblk, LANE, kblk), lambda b, h, i: (b, 0, h, 0))],
        out_specs=pl.BlockSpec((None, blk, LANE), lambda b, h, i: (b, i, h)),
        out_shape=jax.ShapeDtypeStruct((batch, seq, GROUP_W), F32),
        scratch_shapes=[pltpu.VMEM((SUBLANES, blk), F32), pltpu.VMEM((2 * ATT_ACC_ROWS, blk), F32),
                        s_buf, s_buf, mx_buf, mx_buf],
        compiler_params=_cparams(("parallel", "parallel", "arbitrary")),
        name="mla_attn",
    )(q, k, vt)


def _outproj_kernel(ya_ref, yb_ref, oc_ref, zc_ref, gc_ref, w_ref, h_ref, fg_ref, o_ref, *, final):
    oc = oc_ref[...]
    yc = oc * lax.rsqrt(jnp.mean(oc * oc, axis=-1, keepdims=True) + NORM_EPS) * gc_ref[...]
    yc = yc * _silu(zc_ref[...])
    acc = h_ref[...]
    acc = acc + jnp.dot(ya_ref[...], w_ref[0:GROUP_W, :], preferred_element_type=F32)
    acc = acc + jnp.dot(yb_ref[...], w_ref[GROUP_W:2 * GROUP_W, :], preferred_element_type=F32)
    acc = acc + _bdot(yc, w_ref[2 * GROUP_W:3 * GROUP_W, :])
    if final:
        acc = acc * lax.rsqrt(jnp.mean(acc * acc, axis=-1, keepdims=True) + NORM_EPS) * fg_ref[...]
    o_ref[...] = acc


def _outproj(ya, yb, oc, proj, h2d, mla_out_g, w_out, final_g, final, tm):
    m = h2d.shape[0]

    def rows(width, blk=0):
        return pl.BlockSpec((tm, width), lambda i, blk=blk: (i, blk))

    return pl.pallas_call(
        functools.partial(_outproj_kernel, final=final),
        grid=(m // tm,),
        in_specs=[rows(GROUP_W), rows(GROUP_W), rows(GROUP_W), rows(GROUP_W, COL_ZC // GROUP_W),
                  pl.BlockSpec((1, GROUP_W), lambda i: (0, 0)),
                  pl.BlockSpec((3 * GROUP_W, D_MODEL), lambda i: (0, 0)),
                  rows(D_MODEL),
                  pl.BlockSpec((1, D_MODEL), lambda i: (0, 0))],
        out_specs=rows(D_MODEL),
        out_shape=jax.ShapeDtypeStruct((m, D_MODEL), F32),
        compiler_params=_cparams(("parallel",)),
        name="outproj_final" if final else "outproj",
    )(ya, yb, oc, proj, mla_out_g, w_out, h2d, final_g)


def _rope_tables(seq):
    half = QK_ROPE // 2
    inv_freq = ROPE_BASE ** (-jnp.arange(half, dtype=F32) * 2.0 / QK_ROPE)
    ang = jnp.arange(seq, dtype=F32)[:, None] * inv_freq[None, :]
    cos2 = jnp.concatenate([jnp.cos(ang)] * 2, axis=1)
    sin2 = jnp.concatenate([jnp.sin(ang)] * 2, axis=1)
    zeros = lambda w: jnp.zeros((seq, w), F32)
    ones = jnp.ones((seq, QK_NOPE), F32)
    return {
        "cq": jnp.concatenate([ones, cos2, zeros(HEAD_PAD - QK_NOPE - QK_ROPE)], axis=1),
        "sq": jnp.concatenate([zeros(QK_NOPE), sin2, zeros(HEAD_PAD - QK_NOPE - QK_ROPE)], axis=1),
        "ck": jnp.concatenate([cos2, zeros(LANE - QK_ROPE)], axis=1),
        "sk": jnp.concatenate([zeros(QK_ROPE), sin2, zeros(LANE - 2 * QK_ROPE)], axis=1),
    }


def _swap_halves(w):
    half = w.shape[-1] // 2
    return jnp.concatenate([-w[..., half:], w[..., :half]], axis=-1)


def _block_diag(w):
    h, n, _ = w.shape
    eye = jnp.eye(h, dtype=w.dtype)
    return (eye[:, None, :, None] * w[:, :, None, :]).reshape(h * n, h * n)


def _layer_params(l, ln_g, w_in, rwkv_mu, rwkv_w0, rwkv_w2, rwkv_a0, rwkv_a2, rwkv_k_k, rwkv_k_a, rwkv_r_k,
                  rwkv_lnx_g, rwkv_lnx_b, lru_conv_w, lru_conv_b, lru_ga_w, lru_ga_b, lru_gx_w, lru_gx_b,
                  lru_lam, lru_out_g, mla_q_norm_g, mla_w_uq, mla_kv_norm_g, mla_w_ukv, mla_out_g, w_out):
    G = GROUP_W
    w = w_in[l]
    o_ua, o_za = 0, 3 * G + 2 * LORA
    o_ub = o_za + G
    o_zb = o_ub + G
    o_ql = o_zb + G
    o_kv = o_ql + Q_LORA
    o_kr = o_kv + KV_LORA
    o_zc = o_kr + QK_ROPE
    sl = lambda a, n: w[:, a:a + n]
    kr_w = sl(o_kr, QK_ROPE)
    w_re = jnp.concatenate([
        sl(o_ua, G), sl(o_ua + G, G), sl(o_ua + 2 * G, G), sl(o_za, G), sl(o_ub, G), sl(o_zb, G), sl(o_zc, G),
        sl(o_ql, Q_LORA), sl(o_kv, KV_LORA),
        kr_w, _swap_halves(kr_w), jnp.zeros((D_MODEL, LANE - 2 * QK_ROPE), F32),
        sl(o_ua + 3 * G, 2 * LORA)], axis=1).astype(BF16)
    row = lambda v: v.reshape(1, -1)
    mu = rwkv_mu[l]
    zl = jnp.zeros((LORA, G), F32)
    head_id = jnp.arange(RW_GROUP) // HEAD_DIM
    t_idx = jnp.arange(RW_CHUNK)
    scale = (QK_NOPE + QK_ROPE) ** -0.5 * LOG2E
    wq3 = mla_w_uq[l].reshape(Q_LORA, N_HEADS, QK_NOPE + QK_ROPE) * scale
    zq = lambda n: jnp.zeros((Q_LORA, N_HEADS, n), F32)
    wq = jnp.concatenate([wq3, zq(HEAD_PAD - QK_NOPE - QK_ROPE)], axis=2)
    wq_sw = jnp.concatenate([zq(QK_NOPE), _swap_halves(wq3[:, :, QK_NOPE:]),
                             zq(HEAD_PAD - QK_NOPE - QK_ROPE)], axis=2)
    wkv3 = mla_w_ukv[l].reshape(KV_LORA, N_HEADS, QK_NOPE + V_DIM)
    wk = jnp.concatenate([wkv3[:, :, :QK_NOPE], jnp.zeros((KV_LORA, N_HEADS, HEAD_PAD - QK_NOPE), F32)], axis=2)
    place = jnp.zeros((LANE, N_HEADS, HEAD_PAD), F32)
    place = place.at[jnp.arange(QK_ROPE), :, QK_NOPE + jnp.arange(QK_ROPE)].set(1.0)
    return {
        "ln_g": row(ln_g[l]), "w_in": w_re,
        "mu_r": row(mu[0:G]), "mu_k": row(mu[G:2 * G]), "mu_v": row(mu[2 * G:3 * G]),
        "mu_wa": row(mu[3 * G:3 * G + 2 * LORA]),
        "w0": row(rwkv_w0[l]), "w2p": jnp.concatenate([rwkv_w2[l], zl], axis=0).astype(BF16),
        "a0": row(rwkv_a0[l]), "a2p": jnp.concatenate([zl, rwkv_a2[l]], axis=0).astype(BF16),
        "k_k": row(rwkv_k_k[l]), "k_a": row(rwkv_k_a[l]), "r_k": row(rwkv_r_k[l]),
        "lnx_g": row(rwkv_lnx_g[l]), "lnx_b": row(rwkv_lnx_b[l]),
        "seg": (head_id[:, None] == head_id[None, :]).astype(BF16),
        "tri": (t_idx[None, :] <= t_idx[:, None]).astype(BF16),
        "conv_w": lru_conv_w[l], "conv_b": row(lru_conv_b[l]),
        "ga_w": _block_diag(lru_ga_w[l]).astype(BF16), "ga_b": row(lru_ga_b[l]),
        "gx_w": _block_diag(lru_gx_w[l]).astype(BF16), "gx_b": row(lru_gx_b[l]),
        "lam": row(lru_lam[l]), "lru_out_g": row(lru_out_g[l]),
        "q_norm_g": row(mla_q_norm_g[l]), "kv_norm_g": row(mla_kv_norm_g[l]),
        "wq": wq.reshape(Q_LORA, -1).astype(BF16), "wq_sw": wq_sw.reshape(Q_LORA, -1).astype(BF16),
        "wk": wk.reshape(KV_LORA, -1).astype(BF16),
        "wv_t": wkv3[:, :, QK_NOPE:].reshape(KV_LORA, -1).T.astype(BF16),
        "k_place": place.reshape(LANE, -1).astype(BF16),
        "mla_out_g": row(mla_out_g[l]), "w_out": w_out[l].astype(BF16),
    }


def kernel(x, ln_g, w_in, rwkv_mu, rwkv_w0, rwkv_w2, rwkv_a0, rwkv_a2, rwkv_k_k, rwkv_k_a, rwkv_r_k,
           rwkv_lnx_g, rwkv_lnx_b, lru_conv_w, lru_conv_b, lru_ga_w, lru_ga_b, lru_gx_w, lru_gx_b, lru_lam,
           lru_out_g, mla_q_norm_g, mla_w_uq, mla_kv_norm_g, mla_w_ukv, mla_out_g, w_out, final_g):
    batch, seq, _ = x.shape
    tm = min(512, seq)
    rope = _rope_tables(seq)
    fg = final_g.reshape(1, -1)
    h = x.reshape(batch * seq, D_MODEL)
    for l in range(DEPTH):
        prm = _layer_params(l, ln_g, w_in, rwkv_mu, rwkv_w0, rwkv_w2, rwkv_a0, rwkv_a2, rwkv_k_k, rwkv_k_a,
                            rwkv_r_k, rwkv_lnx_g, rwkv_lnx_b, lru_conv_w, lru_conv_b, lru_ga_w, lru_ga_b,
                            lru_gx_w, lru_gx_b, lru_lam, lru_out_g, mla_q_norm_g, mla_w_uq, mla_kv_norm_g,
                            mla_w_ukv, mla_out_g, w_out)
        proj2d = _inproj(h, prm["ln_g"], prm["w_in"], tm)
        proj = proj2d.reshape(batch, seq, D_PROJ)
        ya = _rwkv(proj, prm, batch, seq)
        yb = _lru(proj, prm, batch, seq, min(256, seq))
        blk = min(512, seq)
        q, k, vt = _mla_pro(proj, prm, rope, batch, seq, blk, blk // 2)
        oc = _attn(q, k, vt, batch, seq, blk, blk // 2)
        h = _outproj(ya.reshape(batch * seq, GROUP_W), yb.reshape(batch * seq, GROUP_W),
                     oc.reshape(batch * seq, GROUP_W), proj2d, h, prm["mla_out_g"], prm["w_out"], fg,
                     l == DEPTH - 1, tm)
    return h.reshape(batch, seq, D_MODEL)
```

```python
import functools

import jax
import jax.numpy as jnp
from jax import lax
from jax.experimental import pallas as pl
from jax.experimental.pallas import tpu as pltpu

F32 = jnp.float32
BF16 = jnp.bfloat16

D_MODEL = 1024
DEPTH = 4
GROUP_W = 512
HEAD_DIM = 64
N_HEADS = 8
NORM_EPS = 1e-6
LORA = 64
GN_EPS = 64e-5
CONV_W = 4
LRU_C = 8.0
Q_LORA = 256
KV_LORA = 128
QK_NOPE = 64
QK_ROPE = 32
V_DIM = 64
ROPE_BASE = 10000.0

LANE = 128
SUBLANES = 8
HEAD_PAD = 128
RW_GROUP = 256
RW_CHUNK = 128
RW_STREAMS = 2
BF16_ROWS = 16
ATT_ONES_ROWS = BF16_ROWS
ATT_ACC_ROWS = V_DIM + ATT_ONES_ROWS
LOG2E = 1.4426950408889634
COL_R, COL_K, COL_V, COL_ZA, COL_UB, COL_ZB, COL_ZC = (i * GROUP_W for i in range(7))
COL_QLAT = 7 * GROUP_W
COL_KVLAT = COL_QLAT + Q_LORA
COL_KR = COL_KVLAT + KV_LORA
COL_WA = COL_KR + LANE
D_PROJ = COL_WA + LANE

VMEM_LIMIT = 56 * 1024 * 1024


def _cparams(sem):
    return pltpu.CompilerParams(dimension_semantics=sem, vmem_limit_bytes=VMEM_LIMIT)


def _bdot(a, b):
    return jnp.dot(a.astype(BF16), b.astype(BF16), preferred_element_type=F32)


def _bdot_nt(a, b):
    return lax.dot_general(a.astype(BF16), b.astype(BF16), (((1,), (1,)), ((), ())),
                           preferred_element_type=F32)


def _split_dot(x, m, passes):
    acc = None
    rem = x
    for i in range(passes):
        part = rem.astype(BF16)
        term = jnp.dot(part, m, preferred_element_type=F32)
        acc = term if acc is None else acc + term
        if i + 1 < passes:
            rem = rem - part.astype(F32)
    return acc


def _silu(z):
    return z * jax.nn.sigmoid(z)


def _shift_rows(x, tail, d):
    rolled = pltpu.roll(x, d, 0)
    row8 = lax.broadcasted_iota(jnp.int32, tail.shape, 0)
    first = jnp.where(row8 < d, pltpu.roll(tail, d, 0), rolled[0:SUBLANES, :])
    return jnp.concatenate([first, rolled[SUBLANES:, :]], axis=0)


def _inproj_kernel(x_ref, g_ref, w_ref, mu_ref, muwa_ref, cw_ref, cb_ref, o_ref, tail_ref, *, tiles_per_seq):
    i = pl.program_id(0)
    tm = x_ref.shape[0]
    x = x_ref[...]
    ms = jnp.mean(x * x, axis=-1, keepdims=True)
    xn = x * lax.rsqrt(ms + NORM_EPS) * g_ref[...]
    y = _bdot(xn, w_ref[...])
    tail = jnp.where(i % tiles_per_seq == 0, 0.0, tail_ref[...])
    tail_ref[...] = y[tm - SUBLANES:tm, :]

    def lerp(lo, width, mu):
        cur = y[:, lo:lo + width]
        return cur + mu * (_shift_rows(cur, tail[:, lo:lo + width], 1) - cur)

    o_ref[:, COL_R:COL_ZA] = lerp(COL_R, COL_ZA - COL_R, mu_ref[...])
    o_ref[:, COL_ZA:COL_UB] = y[:, COL_ZA:COL_UB]
    u = y[:, COL_UB:COL_ZB]
    u_tail = tail[:, COL_UB:COL_ZB]
    xc = cb_ref[...] + u * cw_ref[CONV_W - 1:CONV_W, :]
    for d in range(1, CONV_W):
        xc = xc + _shift_rows(u, u_tail, d) * cw_ref[CONV_W - 1 - d:CONV_W - d, :]
    o_ref[:, COL_UB:COL_ZB] = xc
    o_ref[:, COL_ZB:COL_WA] = y[:, COL_ZB:COL_WA]
    o_ref[:, COL_WA:D_PROJ] = lerp(COL_WA, D_PROJ - COL_WA, muwa_ref[...])


def _inproj(h2d, prm, tm, seq):
    m = h2d.shape[0]

    def whole(shape):
        return pl.BlockSpec(shape, lambda i: tuple(0 for _ in shape))

    return pl.pallas_call(
        functools.partial(_inproj_kernel, tiles_per_seq=seq // tm),
        grid=(m // tm,),
        in_specs=[pl.BlockSpec((tm, D_MODEL), lambda i: (i, 0)),
                  whole((1, D_MODEL)), whole((D_MODEL, D_PROJ)),
                  whole((1, COL_ZA - COL_R)), whole((1, LANE)), whole((CONV_W, GROUP_W)), whole((1, GROUP_W))],
        out_specs=pl.BlockSpec((tm, D_PROJ), lambda i: (i, 0)),
        out_shape=jax.ShapeDtypeStruct((m, D_PROJ), F32),
        scratch_shapes=[pltpu.VMEM((SUBLANES, D_PROJ), F32)],
        compiler_params=_cparams(("arbitrary",)),
        name="inproj",
    )(h2d, prm["ln_g"], prm["w_in"], prm["mu_rkv"], prm["mu_wa"], prm["conv_w"], prm["conv_b"])


def _split_dot_left(m, x, passes):
    acc = None
    rem = x
    for i in range(passes):
        part = rem.astype(BF16)
        term = jnp.dot(m, part, preferred_element_type=F32)
        acc = term if acc is None else acc + term
        if i + 1 < passes:
            rem = rem - part.astype(F32)
    return acc


def _seg_sum(x, seg, passes):
    parts = [_split_dot(x[:, i:i + RW_GROUP], seg, passes) for i in range(0, x.shape[1], RW_GROUP)]
    return jnp.concatenate(parts, axis=1)


class _RwkvStream:
    def __init__(self, bi, refs, shared):
        self.bi = bi
        self.refs = refs
        self.sh = shared

    def load(self):
        f = self.refs
        self.r, self.k, self.v, self.wa = f["r"][self.bi], f["k"][self.bi], f["v"][self.bi], f["wa"][self.bi]

    def mm_lora(self):
        f = self.refs
        self.lw = _bdot(jnp.tanh(self.wa), f["w2"][...])
        self.la = _bdot(self.wa, f["a2"][...])

    def decay(self):
        f = self.refs
        log_w = -jax.nn.softplus(-(f["w0"][...] + self.lw)) - 0.5
        self.ld = -jnp.exp(log_w)
        self.a = jax.nn.sigmoid(f["a0"][...] + self.la)
        self.kk = self.k * f["kk"][...]

    def mm_norm_cumsum(self):
        self.n2 = _seg_sum(self.kk * self.kk, self.sh["seg"], 2)
        self.g = _split_dot_left(self.sh["tri"], self.ld, 2)

    def build(self):
        f, C = self.refs, self.sh["C"]
        kk = self.kk / jnp.maximum(jnp.sqrt(self.n2), 1e-12)
        k = self.k * (1.0 + (self.a - 1.0) * f["ka"][...])
        bb = kk * self.a
        g, ld = self.g, self.ld
        self.g_last = g[C - 1:C, :]
        inv_g = jnp.exp(-g)
        to_end = jnp.exp(self.g_last - g)
        self.lhs = jnp.concatenate([self.r * jnp.exp(g), kk * jnp.exp(g - ld)], axis=0).astype(BF16)
        self.rhs = jnp.concatenate([k * inv_g, bb * inv_g], axis=0)
        self.kb_end = jnp.concatenate([k * to_end, -(bb * to_end)], axis=0)
        self.vb = self.v.astype(BF16)
        self.kmod = k

    @staticmethod
    def _grp(x, i):
        return x[:, i * RW_GROUP:(i + 1) * RW_GROUP]

    @staticmethod
    def _pair(x, i):
        return x[:, i * LANE:(i + 1) * LANE]

    def mm_state_in(self):
        st = self.refs["state"]
        p = [lax.dot_general(self._grp(self.lhs, i), st[self.bi, i].astype(BF16), (((1,), (1,)), ((), ())),
                             preferred_element_type=F32) for i in range(GROUP_W // RW_GROUP)]
        self.p = jnp.concatenate(p, axis=1)

    def mm_amat(self):
        lane2 = self.sh["lane2"]
        pair = self._pair

        def own2(hd, x):
            return jnp.where((lane2 < HEAD_DIM) == (hd % 2 == 0), x, 0.0)

        self.amat = [_bdot_nt(pair(self.lhs, hd // 2), own2(hd, pair(self.rhs, hd // 2)))
                     for hd in range(N_HEADS)]

    def mm_xinit(self):
        C, incl, strict, pair = self.sh["C"], self.sh["incl"], self.sh["strict"], self._pair
        self.a_y = [jnp.concatenate([jnp.where(incl, m[0:C, 0:C], 0.0), jnp.where(incl, -m[0:C, C:2 * C], 0.0)],
                                    axis=1).astype(BF16) for m in self.amat]
        self.xs = [pair(self.p, hd // 2)[C:2 * C, :]
                   + _bdot(jnp.where(strict, self.amat[hd][C:2 * C, 0:C], 0.0), pair(self.vb, hd // 2))
                   for hd in range(N_HEADS)]
        self.npow = [jnp.where(strict, -self.amat[hd][C:2 * C, C:2 * C], 0.0).astype(BF16)
                     for hd in range(N_HEADS)]
        self.span = 1

    def mm_level(self):
        C, span = self.sh["C"], self.span
        lo = span if span % BF16_ROWS == 0 else 0
        if 2 * span < C:
            outs = [_bdot(n[lo:, :], jnp.concatenate([n, x.astype(BF16)], axis=1))
                    for n, x in zip(self.npow, self.xs)]
            self.npow = [jnp.concatenate([jnp.zeros((lo, C), BF16), o[:, 0:C].astype(BF16)], axis=0) if lo
                         else o[:, 0:C].astype(BF16) for o in outs]
            upd = [o[:, C:] for o in outs]
        else:
            upd = [_bdot(n[lo:, :], x) for n, x in zip(self.npow, self.xs)]
        self.xs = [jnp.concatenate([x[0:lo, :], x[lo:, :] + d], axis=0) if lo else x + d
                   for x, d in zip(self.xs, upd)]
        self.span = 2 * span

    def mm_y(self):
        C, pair = self.sh["C"], self._pair
        y_h = [_bdot(self.a_y[hd], jnp.concatenate([pair(self.vb, hd // 2), self.xs[hd].astype(BF16)], axis=0))
               for hd in range(N_HEADS)]
        first = self.sh["lane1"] < HEAD_DIM
        n_pairs = GROUP_W // LANE
        y = jnp.concatenate([jnp.where(first, y_h[2 * i], y_h[2 * i + 1]) for i in range(n_pairs)], axis=1)
        self.u = jnp.concatenate([jnp.where(first, self.xs[2 * i], self.xs[2 * i + 1]) for i in range(n_pairs)],
                                 axis=1)
        self.y = y + self.p[0:C, :]

    def mm_state_out(self):
        st, seg = self.refs["state"], self.sh["seg"]
        vu_t = jnp.concatenate([self.v, self.u], axis=0).T
        decay_all = jnp.exp(self.g_last)
        for i in range(GROUP_W // RW_GROUP):
            s_new = (st[self.bi, i] * self._grp(decay_all, i)
                     + _bdot(vu_t[i * RW_GROUP:(i + 1) * RW_GROUP, :], self._grp(self.kb_end, i)))
            st[self.bi, i] = jnp.where(seg > 0.5, s_new, 0.0)

    def mm_mean(self):
        self.mean = _seg_sum(self.y, self.sh["seg"], 1) * (1.0 / HEAD_DIM)

    def mm_var_bonus(self):
        f, seg = self.refs, self.sh["seg"]
        self.yc = self.y - self.mean
        self.var = _seg_sum(self.yc * self.yc, seg, 1) * (1.0 / HEAD_DIM)
        self.bonus = _seg_sum(self.r * self.kmod * f["rk"][...], seg, 1) * self.v

    def finish(self):
        f = self.refs
        yn = self.yc * lax.rsqrt(self.var + GN_EPS) * f["lng"][...] + f["lnb"][...]
        f["o"][self.bi] = ((yn + self.bonus) * _silu(f["z"][self.bi])).astype(f["o"].dtype)


def _rwkv_kernel(r_ref, k_ref, v_ref, wa_ref, z_ref,
                 w0_ref, w2_ref, a0_ref, a2_ref, kk_ref, ka_ref, rk_ref, lng_ref, lnb_ref,
                 seg_ref, tri_ref,
                 o_ref,
                 state_ref):
    c = pl.program_id(1)
    C = r_ref.shape[1]

    @pl.when(c == 0)
    def _():
        state_ref[...] = jnp.zeros_like(state_ref)

    ti = lax.broadcasted_iota(jnp.int32, (C, C), 0)
    si = lax.broadcasted_iota(jnp.int32, (C, C), 1)
    shared = {"C": C, "seg": seg_ref[...], "tri": tri_ref[...], "incl": si <= ti, "strict": si < ti,
              "lane2": lax.broadcasted_iota(jnp.int32, (2 * C, LANE), 1),
              "lane1": lax.broadcasted_iota(jnp.int32, (C, LANE), 1)}
    refs = {"r": r_ref, "k": k_ref, "v": v_ref, "wa": wa_ref, "z": z_ref, "o": o_ref,
            "w0": w0_ref, "w2": w2_ref, "a0": a0_ref, "a2": a2_ref, "kk": kk_ref, "ka": ka_ref, "rk": rk_ref,
            "lng": lng_ref, "lnb": lnb_ref, "state": state_ref}
    a, b = _RwkvStream(0, refs, shared), _RwkvStream(1, refs, shared)
    n_levels = C.bit_length() - 1

    a.load()
    a.mm_lora()
    a.decay()
    a.mm_norm_cumsum()
    a.build()
    a.mm_state_in()
    a.mm_amat()
    b.load()
    b.mm_lora()
    b.decay()
    a.mm_xinit()
    b.mm_norm_cumsum()
    b.build()
    for _ in range(n_levels):
        a.mm_level()
    a.mm_y()
    a.mm_state_out()
    b.mm_state_in()
    b.mm_amat()
    a.mm_mean()
    b.mm_xinit()
    a.mm_var_bonus()
    a.finish()
    for _ in range(n_levels):
        b.mm_level()
    b.mm_y()
    b.mm_state_out()
    b.mm_mean()
    b.mm_var_bonus()
    b.finish()


def _rwkv(proj, prm, batch, seq):
    C = RW_CHUNK
    nb = RW_STREAMS

    def col(base):
        return pl.BlockSpec((nb, C, GROUP_W), lambda b, c, base=base: (b, c, base // GROUP_W))

    def vec():
        return pl.BlockSpec((1, GROUP_W), lambda b, c: (0, 0))

    def whole(shape):
        return pl.BlockSpec(shape, lambda b, c: tuple(0 for _ in shape))

    in_specs = [col(COL_R), col(COL_K), col(COL_V),
                pl.BlockSpec((nb, C, LANE), lambda b, c: (b, c, COL_WA // LANE)),
                col(COL_ZA),
                vec(), whole((LANE, GROUP_W)), vec(), whole((LANE, GROUP_W)),
                vec(), vec(), vec(), vec(), vec(),
                whole((RW_GROUP, RW_GROUP)), whole((C, C))]
    return pl.pallas_call(
        _rwkv_kernel,
        grid=(batch // nb, seq // C),
        in_specs=in_specs,
        out_specs=pl.BlockSpec((nb, C, GROUP_W), lambda b, c: (b, c, 0)),
        out_shape=jax.ShapeDtypeStruct((batch, seq, GROUP_W), BF16),
        scratch_shapes=[pltpu.VMEM((nb, GROUP_W // RW_GROUP, RW_GROUP, RW_GROUP), F32)],
        compiler_params=_cparams(("parallel", "arbitrary")),
        name="rwkv7",
    )(proj, proj, proj, proj, proj,
      prm["w0"], prm["w2p"], prm["a0"], prm["a2p"], prm["k_k"], prm["k_a"], prm["r_k"],
      prm["lnx_g"], prm["lnx_b"], prm["seg"], prm["tri"])


def _lru_kernel(xc_ref, z_ref, gaw_ref, gab_ref, gxw_ref, gxb_ref, lam_ref, og_ref, o_ref, hprev_ref):
    c = pl.program_id(1)
    ct = xc_ref.shape[0]

    @pl.when(c == 0)
    def _():
        hprev_ref[...] = jnp.zeros_like(hprev_ref)

    xc = xc_ref[...]
    gate_r = jax.nn.sigmoid(_bdot(xc, gaw_ref[...]) + gab_ref[...])
    gate_i = jax.nn.sigmoid(_bdot(xc, gxw_ref[...]) + gxb_ref[...])
    log_a = -LRU_C * gate_r * jax.nn.softplus(-lam_ref[...])
    acc_a = jnp.exp(log_a)
    th = jnp.tanh(log_a)
    acc_b = jnp.sqrt(-2.0 * th / (1.0 - th)) * (gate_i * xc)
    n_groups = ct // SUBLANES
    acc_a = acc_a.reshape(n_groups, SUBLANES, GROUP_W)
    acc_b = acc_b.reshape(n_groups, SUBLANES, GROUP_W)
    sub = lax.broadcasted_iota(jnp.int32, acc_a.shape, 1)
    d = 1
    while d < SUBLANES:
        keep = sub >= d
        acc_b = acc_a * jnp.where(keep, pltpu.roll(acc_b, d, 1), 0.0) + acc_b
        acc_a = acc_a * jnp.where(keep, pltpu.roll(acc_a, d, 1), 1.0)
        d *= 2
    carry = hprev_ref[...]
    groups = []
    for i in range(n_groups):
        h_i = acc_a[i] * carry + acc_b[i]
        groups.append(h_i)
        carry = h_i[SUBLANES - 1:SUBLANES, :]
    hs = jnp.concatenate(groups, axis=0)
    hprev_ref[...] = hs[ct - 1:ct, :]
    ms = jnp.mean(hs * hs, axis=-1, keepdims=True)
    yb = hs * lax.rsqrt(ms + NORM_EPS) * og_ref[...]
    o_ref[...] = (yb * _silu(z_ref[...])).astype(o_ref.dtype)


def _lru(proj, prm, batch, seq, ct):
    def vec():
        return pl.BlockSpec((1, GROUP_W), lambda b, c: (0, 0))

    def mat(rows):
        return pl.BlockSpec((rows, GROUP_W), lambda b, c: (0, 0))

    return pl.pallas_call(
        _lru_kernel,
        grid=(batch, seq // ct),
        in_specs=[pl.BlockSpec((None, ct, GROUP_W), lambda b, c: (b, c, COL_UB // GROUP_W)),
                  pl.BlockSpec((None, ct, GROUP_W), lambda b, c: (b, c, COL_ZB // GROUP_W)),
                  mat(GROUP_W), vec(), mat(GROUP_W), vec(), vec(), vec()],
        out_specs=pl.BlockSpec((None, ct, GROUP_W), lambda b, c: (b, c, 0)),
        out_shape=jax.ShapeDtypeStruct((batch, seq, GROUP_W), BF16),
        scratch_shapes=[pltpu.VMEM((1, GROUP_W), F32)],
        compiler_params=_cparams(("parallel", "arbitrary")),
        name="rglru",
    )(proj, proj, prm["ga_w"], prm["ga_b"], prm["gx_w"], prm["gx_b"],
      prm["lam"], prm["lru_out_g"])


def _mla_pro_kernel(ql_ref, kvl_ref, kr_ref, qg_ref, kvg_ref, wq_ref, wqs_ref, wk_ref, wv_ref, pk_ref,
                    cq_ref, sq_ref, ck_ref, sk_ref, q_ref, k_ref, v_ref):
    ql = ql_ref[...]
    qn = ql * lax.rsqrt(jnp.mean(ql * ql, axis=-1, keepdims=True) + NORM_EPS) * qg_ref[...]
    qn = qn.astype(BF16)
    cq = jnp.concatenate([cq_ref[...]] * N_HEADS, axis=1)
    sq = jnp.concatenate([sq_ref[...]] * N_HEADS, axis=1)
    q = jnp.dot(qn, wq_ref[...], preferred_element_type=F32) * cq
    q = q + jnp.dot(qn, wqs_ref[...], preferred_element_type=F32) * sq
    q_ref[...] = q.astype(q_ref.dtype)

    kvl = kvl_ref[...]
    kvn = kvl * lax.rsqrt(jnp.mean(kvl * kvl, axis=-1, keepdims=True) + NORM_EPS) * kvg_ref[...]
    kvn = kvn.astype(BF16)
    kr = kr_ref[...]
    roped = kr * ck_ref[...] + pltpu.roll(kr * sk_ref[...], LANE - QK_ROPE, 1)
    kfull = jnp.dot(kvn, wk_ref[...], preferred_element_type=F32) + _bdot(roped, pk_ref[...])
    k_ref[...] = kfull.astype(k_ref.dtype)
    vt = lax.dot_general(wv_ref[...], kvn, (((1,), (1,)), ((), ())), preferred_element_type=F32)
    kblk = v_ref.shape[-1]
    for i in range(v_ref.shape[0]):
        v_ref[i] = vt[:, i * kblk:(i + 1) * kblk].astype(v_ref.dtype)


def _mla_pro(proj, prm, rope, batch, seq, tm, kblk):
    qk_w = N_HEADS * HEAD_PAD

    def whole(shape):
        return pl.BlockSpec(shape, lambda b, t: tuple(0 for _ in shape))

    def tab():
        return pl.BlockSpec((tm, LANE), lambda b, t: (t, 0))

    return pl.pallas_call(
        _mla_pro_kernel,
        grid=(batch, seq // tm),
        in_specs=[pl.BlockSpec((None, tm, Q_LORA), lambda b, t: (b, t, COL_QLAT // Q_LORA)),
                  pl.BlockSpec((None, tm, KV_LORA), lambda b, t: (b, t, COL_KVLAT // KV_LORA)),
                  pl.BlockSpec((None, tm, LANE), lambda b, t: (b, t, COL_KR // LANE)),
                  whole((1, Q_LORA)), whole((1, KV_LORA)),
                  whole((Q_LORA, qk_w)), whole((Q_LORA, qk_w)), whole((KV_LORA, qk_w)),
                  whole((GROUP_W, KV_LORA)), whole((LANE, qk_w)),
                  tab(), tab(), tab(), tab()],
        out_specs=[pl.BlockSpec((None, tm, qk_w), lambda b, t: (b, t, 0)),
                   pl.BlockSpec((None, tm, qk_w), lambda b, t: (b, t, 0)),
                   pl.BlockSpec((None, tm // kblk, GROUP_W, kblk), lambda b, t: (b, t, 0, 0))],
        out_shape=[jax.ShapeDtypeStruct((batch, seq, qk_w), BF16),
                   jax.ShapeDtypeStruct((batch, seq, qk_w), BF16),
                   jax.ShapeDtypeStruct((batch, seq // kblk, GROUP_W, kblk), BF16)],
        compiler_params=_cparams(("parallel", "parallel")),
        name="mla_pro",
    )(proj, proj, proj, prm["q_norm_g"], prm["kv_norm_g"], prm["wq"], prm["wq_sw"], prm["wk"], prm["wv_t"],
      prm["k_place"], rope["cq"], rope["sq"], rope["ck"], rope["sk"])


def _attn_kernel(q_ref, k_ref, vt_ref, o_ref, m_ref, acc_ref, sa_ref, sb_ref, mxa_ref, mxb_ref, *, blk, kblk):
    qi = pl.program_id(2)
    m_ref[...] = jnp.full_like(m_ref, -jnp.inf)
    acc_ref[...] = jnp.zeros_like(acc_ref)
    qs = [q_ref[:, hh * HEAD_PAD:(hh + 1) * HEAD_PAD] for hh in range(2)]
    ones = jnp.ones((ATT_ONES_ROWS, kblk), BF16)
    key = lax.broadcasted_iota(jnp.int32, (kblk, blk), 0)
    qry = lax.broadcasted_iota(jnp.int32, (kblk, blk), 1)

    def scores(t, s_ref, mx_ref):
        off = pl.multiple_of(t * kblk, kblk)
        for hh in range(2):
            s = lax.dot_general(k_ref[pl.ds(off, kblk), hh * HEAD_PAD:(hh + 1) * HEAD_PAD], qs[hh],
                                (((1,), (1,)), ((), ())), preferred_element_type=F32)
            s_ref[hh] = s
            mx_ref[hh:hh + 1, :] = jnp.max(s, axis=0, keepdims=True)

    def consume(t, s_ref, mx_ref, key_shift):
        for hh in range(2):
            s = s_ref[hh]
            if key_shift is None:
                mx = mx_ref[hh:hh + 1, :]
            else:
                s = jnp.where(key + key_shift <= qry, s, -jnp.inf)
                mx = jnp.max(s, axis=0, keepdims=True)
            m_prev = m_ref[hh:hh + 1, :]
            m_new = jnp.maximum(m_prev, mx)
            alpha = jnp.exp2(m_prev - m_new)
            p = jnp.exp2(s - m_new).astype(BF16)
            m_ref[hh:hh + 1, :] = m_new
            rows = slice(hh * ATT_ACC_ROWS, (hh + 1) * ATT_ACC_ROWS)
            vt_ext = jnp.concatenate([vt_ref[t, hh * V_DIM:(hh + 1) * V_DIM, :], ones], axis=0)
            acc_ref[rows, :] = alpha * acc_ref[rows, :] + jnp.dot(vt_ext, p, preferred_element_type=F32)

    scores(0, sa_ref, mxa_ref)

    def pair(u, carry):
        t = 2 * u
        scores(t + 1, sb_ref, mxb_ref)
        consume(t, sa_ref, mxa_ref, None)
        scores(t + 2, sa_ref, mxa_ref)
        consume(t + 1, sb_ref, mxb_ref, None)
        return carry

    lax.fori_loop(0, qi, pair, 0)
    t = 2 * qi
    scores(t + 1, sb_ref, mxb_ref)
    consume(t, sa_ref, mxa_ref, 0)
    consume(t + 1, sb_ref, mxb_ref, kblk)
    outs = []
    for hh in range(2):
        base = hh * ATT_ACC_ROWS
        outs.append(acc_ref[base:base + V_DIM, :] / acc_ref[base + V_DIM:base + V_DIM + 1, :])
    o_ref[...] = jnp.concatenate(outs, axis=0).T


def _attn(q, k, vt, batch, seq, blk, kblk):
    pair_w = 2 * HEAD_PAD
    s_buf = pltpu.VMEM((2, kblk, blk), F32)
    mx_buf = pltpu.VMEM((SUBLANES, blk), F32)
    return pl.pallas_call(
        functools.partial(_attn_kernel, blk=blk, kblk=kblk),
        grid=(batch, N_HEADS // 2, seq // blk),
        in_specs=[pl.BlockSpec((None, blk, pair_w), lambda b, h, i: (b, i, h)),
                  pl.BlockSpec((None, seq, pair_w), lambda b, h, i: (b, 0, h)),
                  pl.BlockSpec((None, seq // kblk, LANE, kblk), lambda b, h, i: (b, 0, h, 0))],
        out_specs=pl.BlockSpec((None, blk, LANE), lambda b, h, i: (b, i, h)),
        out_shape=jax.ShapeDtypeStruct((batch, seq, GROUP_W), F32),
        scratch_shapes=[pltpu.VMEM((SUBLANES, blk), F32), pltpu.VMEM((2 * ATT_ACC_ROWS, blk), F32),
                        s_buf, s_buf, mx_buf, mx_buf],
        compiler_params=_cparams(("parallel", "parallel", "arbitrary")),
        name="mla_attn",
    )(q, k, vt)


def _outproj_kernel(ya_ref, yb_ref, oc_ref, zc_ref, gc_ref, w_ref, h_ref, fg_ref, o_ref, *, final):
    oc = oc_ref[...]
    yc = oc * lax.rsqrt(jnp.mean(oc * oc, axis=-1, keepdims=True) + NORM_EPS) * gc_ref[...]
    yc = yc * _silu(zc_ref[...])
    acc = h_ref[...]
    acc = acc + jnp.dot(ya_ref[...], w_ref[0:GROUP_W, :], preferred_element_type=F32)
    acc = acc + jnp.dot(yb_ref[...], w_ref[GROUP_W:2 * GROUP_W, :], preferred_element_type=F32)
    acc = acc + _bdot(yc, w_ref[2 * GROUP_W:3 * GROUP_W, :])
    if final:
        acc = acc * lax.rsqrt(jnp.mean(acc * acc, axis=-1, keepdims=True) + NORM_EPS) * fg_ref[...]
    o_ref[...] = acc


def _outproj(ya, yb, oc, proj, h2d, mla_out_g, w_out, final_g, final, tm):
    m = h2d.shape[0]

    def rows(width, blk=0):
        return pl.BlockSpec((tm, width), lambda i, blk=blk: (i, blk))

    return pl.pallas_call(
        functools.partial(_outproj_kernel, final=final),
        grid=(m // tm,),
        in_specs=[rows(GROUP_W), rows(GROUP_W), rows(GROUP_W), rows(GROUP_W, COL_ZC // GROUP_W),
                  pl.BlockSpec((1, GROUP_W), lambda i: (0, 0)),
                  pl.BlockSpec((3 * GROUP_W, D_MODEL), lambda i: (0, 0)),
                  rows(D_MODEL),
                  pl.BlockSpec((1, D_MODEL), lambda i: (0, 0))],
        out_specs=rows(D_MODEL),
        out_shape=jax.ShapeDtypeStruct((m, D_MODEL), F32),
        compiler_params=_cparams(("parallel",)),
        name="outproj_final" if final else "outproj",
    )(ya, yb, oc, proj, mla_out_g, w_out, h2d, final_g)


def _rope_tables(seq):
    half = QK_ROPE // 2
    inv_freq = ROPE_BASE ** (-jnp.arange(half, dtype=F32) * 2.0 / QK_ROPE)
    ang = jnp.arange(seq, dtype=F32)[:, None] * inv_freq[None, :]
    cos2 = jnp.concatenate([jnp.cos(ang)] * 2, axis=1)
    sin2 = jnp.concatenate([jnp.sin(ang)] * 2, axis=1)
    zeros = lambda w: jnp.zeros((seq, w), F32)
    ones = jnp.ones((seq, QK_NOPE), F32)
    return {
        "cq": jnp.concatenate([ones, cos2, zeros(HEAD_PAD - QK_NOPE - QK_ROPE)], axis=1),
        "sq": jnp.concatenate([zeros(QK_NOPE), sin2, zeros(HEAD_PAD - QK_NOPE - QK_ROPE)], axis=1),
        "ck": jnp.concatenate([cos2, zeros(LANE - QK_ROPE)], axis=1),
        "sk": jnp.concatenate([zeros(QK_ROPE), sin2, zeros(LANE - 2 * QK_ROPE)], axis=1),
    }


def _swap_halves(w):
    half = w.shape[-1] // 2
    return jnp.concatenate([-w[..., half:], w[..., :half]], axis=-1)


def _block_diag(w):
    h, n, _ = w.shape
    eye = jnp.eye(h, dtype=w.dtype)
    return (eye[:, None, :, None] * w[:, :, None, :]).reshape(h * n, h * n)


def _layer_params(l, ln_g, w_in, rwkv_mu, rwkv_w0, rwkv_w2, rwkv_a0, rwkv_a2, rwkv_k_k, rwkv_k_a, rwkv_r_k,
                  rwkv_lnx_g, rwkv_lnx_b, lru_conv_w, lru_conv_b, lru_ga_w, lru_ga_b, lru_gx_w, lru_gx_b,
                  lru_lam, lru_out_g, mla_q_norm_g, mla_w_uq, mla_kv_norm_g, mla_w_ukv, mla_out_g, w_out):
    G = GROUP_W
    w = w_in[l]
    o_ua, o_za = 0, 3 * G + 2 * LORA
    o_ub = o_za + G
    o_zb = o_ub + G
    o_ql = o_zb + G
    o_kv = o_ql + Q_LORA
    o_kr = o_kv + KV_LORA
    o_zc = o_kr + QK_ROPE
    sl = lambda a, n: w[:, a:a + n]
    kr_w = sl(o_kr, QK_ROPE)
    w_re = jnp.concatenate([
        sl(o_ua, G), sl(o_ua + G, G), sl(o_ua + 2 * G, G), sl(o_za, G), sl(o_ub, G), sl(o_zb, G), sl(o_zc, G),
        sl(o_ql, Q_LORA), sl(o_kv, KV_LORA),
        kr_w, _swap_halves(kr_w), jnp.zeros((D_MODEL, LANE - 2 * QK_ROPE), F32),
        sl(o_ua + 3 * G, 2 * LORA)], axis=1).astype(BF16)
    row = lambda v: v.reshape(1, -1)
    mu = rwkv_mu[l]
    zl = jnp.zeros((LORA, G), F32)
    head_id = jnp.arange(RW_GROUP) // HEAD_DIM
    t_idx = jnp.arange(RW_CHUNK)
    scale = (QK_NOPE + QK_ROPE) ** -0.5 * LOG2E
    wq3 = mla_w_uq[l].reshape(Q_LORA, N_HEADS, QK_NOPE + QK_ROPE) * scale
    zq = lambda n: jnp.zeros((Q_LORA, N_HEADS, n), F32)
    wq = jnp.concatenate([wq3, zq(HEAD_PAD - QK_NOPE - QK_ROPE)], axis=2)
    wq_sw = jnp.concatenate([zq(QK_NOPE), _swap_halves(wq3[:, :, QK_NOPE:]),
                             zq(HEAD_PAD - QK_NOPE - QK_ROPE)], axis=2)
    wkv3 = mla_w_ukv[l].reshape(KV_LORA, N_HEADS, QK_NOPE + V_DIM)
    wk = jnp.concatenate([wkv3[:, :, :QK_NOPE], jnp.zeros((KV_LORA, N_HEADS, HEAD_PAD - QK_NOPE), F32)], axis=2)
    place = jnp.zeros((LANE, N_HEADS, HEAD_PAD), F32)
    place = place.at[jnp.arange(QK_ROPE), :, QK_NOPE + jnp.arange(QK_ROPE)].set(1.0)
    return {
        "ln_g": row(ln_g[l]), "w_in": w_re,
        "mu_rkv": row(mu[0:3 * G]),
        "mu_wa": row(mu[3 * G:3 * G + 2 * LORA]),
        "w0": row(rwkv_w0[l]), "w2p": jnp.concatenate([rwkv_w2[l], zl], axis=0).astype(BF16),
        "a0": row(rwkv_a0[l]), "a2p": jnp.concatenate([zl, rwkv_a2[l]], axis=0).astype(BF16),
        "k_k": row(rwkv_k_k[l]), "k_a": row(rwkv_k_a[l]), "r_k": row(rwkv_r_k[l]),
        "lnx_g": row(rwkv_lnx_g[l]), "lnx_b": row(rwkv_lnx_b[l]),
        "seg": (head_id[:, None] == head_id[None, :]).astype(BF16),
        "tri": (t_idx[None, :] <= t_idx[:, None]).astype(BF16),
        "conv_w": lru_conv_w[l], "conv_b": row(lru_conv_b[l]),
        "ga_w": _block_diag(lru_ga_w[l]).astype(BF16), "ga_b": row(lru_ga_b[l]),
        "gx_w": _block_diag(lru_gx_w[l]).astype(BF16), "gx_b": row(lru_gx_b[l]),
        "lam": row(lru_lam[l]), "lru_out_g": row(lru_out_g[l]),
        "q_norm_g": row(mla_q_norm_g[l]), "kv_norm_g": row(mla_kv_norm_g[l]),
        "wq": wq.reshape(Q_LORA, -1).astype(BF16), "wq_sw": wq_sw.reshape(Q_LORA, -1).astype(BF16),
        "wk": wk.reshape(KV_LORA, -1).astype(BF16),
        "wv_t": wkv3[:, :, QK_NOPE:].reshape(KV_LORA, -1).T.astype(BF16),
        "k_place": place.reshape(LANE, -1).astype(BF16),
        "mla_out_g": row(mla_out_g[l]), "w_out": w_out[l].astype(BF16),
    }


def kernel(x, ln_g, w_in, rwkv_mu, rwkv_w0, rwkv_w2, rwkv_a0, rwkv_a2, rwkv_k_k, rwkv_k_a, rwkv_r_k,
           rwkv_lnx_g, rwkv_lnx_b, lru_conv_w, lru_conv_b, lru_ga_w, lru_ga_b, lru_gx_w, lru_gx_b, lru_lam,
           lru_out_g, mla_q_norm_g, mla_w_uq, mla_kv_norm_g, mla_w_ukv, mla_out_g, w_out, final_g):
    batch, seq, _ = x.shape
    assert batch % RW_STREAMS == 0 and seq % RW_CHUNK == 0, (batch, seq)
    tm = min(512, seq)
    rope = _rope_tables(seq)
    fg = final_g.reshape(1, -1)
    h = x.reshape(batch * seq, D_MODEL)
    for l in range(DEPTH):
        prm = _layer_params(l, ln_g, w_in, rwkv_mu, rwkv_w0, rwkv_w2, rwkv_a0, rwkv_a2, rwkv_k_k, rwkv_k_a,
                            rwkv_r_k, rwkv_lnx_g, rwkv_lnx_b, lru_conv_w, lru_conv_b, lru_ga_w, lru_ga_b,
                            lru_gx_w, lru_gx_b, lru_lam, lru_out_g, mla_q_norm_g, mla_w_uq, mla_kv_norm_g,
                            mla_w_ukv, mla_out_g, w_out)
        proj2d = _inproj(h, prm, tm, seq)
        proj = proj2d.reshape(batch, seq, D_PROJ)
        ya = _rwkv(proj, prm, batch, seq)
        yb = _lru(proj, prm, batch, seq, min(256, seq))
        blk = min(512, seq)
        q, k, vt = _mla_pro(proj, prm, rope, batch, seq, blk, blk // 2)
        oc = _attn(q, k, vt, batch, seq, blk, blk // 2)
        h = _outproj(ya.reshape(batch * seq, GROUP_W), yb.reshape(batch * seq, GROUP_W),
                     oc.reshape(batch * seq, GROUP_W), proj2d, h, prm["mla_out_g"], prm["w_out"], fg,
                     l == DEPTH - 1, tm)
    return h.reshape(batch, seq, D_MODEL)
```

```python
import functools

import jax
import jax.numpy as jnp
from jax import lax
from jax.experimental import pallas as pl
from jax.experimental.pallas import tpu as pltpu

F32 = jnp.float32
BF16 = jnp.bfloat16

D_MODEL = 1024
DEPTH = 4
GROUP_W = 512
HEAD_DIM = 64
N_HEADS = 8
NORM_EPS = 1e-6
LORA = 64
GN_EPS = 64e-5
CONV_W = 4
LRU_C = 8.0
Q_LORA = 256
KV_LORA = 128
QK_NOPE = 64
QK_ROPE = 32
V_DIM = 64
ROPE_BASE = 10000.0

LANE = 128
SUBLANES = 8
HEAD_PAD = 128
RW_GROUP = 256
RW_CHUNK = 128
RW_STREAMS = 4
BF16_ROWS = 16
ATT_ONES_ROWS = BF16_ROWS
ATT_ACC_ROWS = V_DIM + ATT_ONES_ROWS
LOG2E = 1.4426950408889634
COL_R, COL_K, COL_V, COL_ZA, COL_UB, COL_ZB, COL_ZC = (i * GROUP_W for i in range(7))
COL_QLAT = 7 * GROUP_W
COL_KVLAT = COL_QLAT + Q_LORA
COL_KR = COL_KVLAT + KV_LORA
COL_WA = COL_KR + LANE
D_PROJ = COL_WA + LANE

VMEM_LIMIT = 56 * 1024 * 1024


def _cparams(sem):
    return pltpu.CompilerParams(dimension_semantics=sem, vmem_limit_bytes=VMEM_LIMIT)


def _bdot(a, b):
    return jnp.dot(a.astype(BF16), b.astype(BF16), preferred_element_type=F32)


def _bdot_nt(a, b):
    return lax.dot_general(a.astype(BF16), b.astype(BF16), (((1,), (1,)), ((), ())),
                           preferred_element_type=F32)


def _split_dot(x, m, passes):
    acc = None
    rem = x
    for i in range(passes):
        part = rem.astype(BF16)
        term = jnp.dot(part, m, preferred_element_type=F32)
        acc = term if acc is None else acc + term
        if i + 1 < passes:
            rem = rem - part.astype(F32)
    return acc


def _silu(z):
    return z * jax.nn.sigmoid(z)


def _shift_rows(x, tail, d):
    rolled = pltpu.roll(x, d, 0)
    row8 = lax.broadcasted_iota(jnp.int32, tail.shape, 0)
    first = jnp.where(row8 < d, pltpu.roll(tail, d, 0), rolled[0:SUBLANES, :])
    return jnp.concatenate([first, rolled[SUBLANES:, :]], axis=0)


def _inproj_kernel(x_ref, g_ref, w_ref, mu_ref, muwa_ref, cw_ref, cb_ref, o_ref, tail_ref, *, tiles_per_seq):
    i = pl.program_id(0)
    tm = x_ref.shape[0]
    x = x_ref[...]
    ms = jnp.mean(x * x, axis=-1, keepdims=True)
    xn = x * lax.rsqrt(ms + NORM_EPS) * g_ref[...]
    y = _bdot(xn, w_ref[...])
    tail = jnp.where(i % tiles_per_seq == 0, 0.0, tail_ref[...])
    tail_ref[...] = y[tm - SUBLANES:tm, :]

    def lerp(lo, width, mu):
        cur = y[:, lo:lo + width]
        return cur + mu * (_shift_rows(cur, tail[:, lo:lo + width], 1) - cur)

    o_ref[:, COL_R:COL_ZA] = lerp(COL_R, COL_ZA - COL_R, mu_ref[...])
    o_ref[:, COL_ZA:COL_UB] = y[:, COL_ZA:COL_UB]
    u = y[:, COL_UB:COL_ZB]
    u_tail = tail[:, COL_UB:COL_ZB]
    xc = cb_ref[...] + u * cw_ref[CONV_W - 1:CONV_W, :]
    for d in range(1, CONV_W):
        xc = xc + _shift_rows(u, u_tail, d) * cw_ref[CONV_W - 1 - d:CONV_W - d, :]
    o_ref[:, COL_UB:COL_ZB] = xc
    o_ref[:, COL_ZB:COL_WA] = y[:, COL_ZB:COL_WA]
    o_ref[:, COL_WA:D_PROJ] = lerp(COL_WA, D_PROJ - COL_WA, muwa_ref[...])


def _inproj(h2d, prm, tm, seq):
    m = h2d.shape[0]

    def whole(shape):
        return pl.BlockSpec(shape, lambda i: tuple(0 for _ in shape))

    return pl.pallas_call(
        functools.partial(_inproj_kernel, tiles_per_seq=seq // tm),
        grid=(m // tm,),
        in_specs=[pl.BlockSpec((tm, D_MODEL), lambda i: (i, 0)),
                  whole((1, D_MODEL)), whole((D_MODEL, D_PROJ)),
                  whole((1, COL_ZA - COL_R)), whole((1, LANE)), whole((CONV_W, GROUP_W)), whole((1, GROUP_W))],
        out_specs=pl.BlockSpec((tm, D_PROJ), lambda i: (i, 0)),
        out_shape=jax.ShapeDtypeStruct((m, D_PROJ), F32),
        scratch_shapes=[pltpu.VMEM((SUBLANES, D_PROJ), F32)],
        compiler_params=_cparams(("arbitrary",)),
        name="inproj",
    )(h2d, prm["ln_g"], prm["w_in"], prm["mu_rkv"], prm["mu_wa"], prm["conv_w"], prm["conv_b"])


def _split_dot_left(m, x, passes):
    acc = None
    rem = x
    for i in range(passes):
        part = rem.astype(BF16)
        term = jnp.dot(m, part, preferred_element_type=F32)
        acc = term if acc is None else acc + term
        if i + 1 < passes:
            rem = rem - part.astype(F32)
    return acc


def _seg_sum(x, seg, passes):
    parts = [_split_dot(x[:, i:i + RW_GROUP], seg, passes) for i in range(0, x.shape[1], RW_GROUP)]
    return jnp.concatenate(parts, axis=1)


class _RwkvStream:
    def __init__(self, bi, refs, shared):
        self.bi = bi
        self.refs = refs
        self.sh = shared

    def load(self):
        f = self.refs
        self.r, self.k, self.v, self.wa = f["r"][self.bi], f["k"][self.bi], f["v"][self.bi], f["wa"][self.bi]

    def mm_lora(self):
        f = self.refs
        self.lw = _bdot(jnp.tanh(self.wa), f["w2"][...])
        self.la = _bdot(self.wa, f["a2"][...])

    def decay(self):
        f = self.refs
        log_w = -jax.nn.softplus(-(f["w0"][...] + self.lw)) - 0.5
        self.ld = -jnp.exp(log_w)
        self.a = jax.nn.sigmoid(f["a0"][...] + self.la)
        self.kk = self.k * f["kk"][...]

    def mm_norm_cumsum(self):
        self.n2 = _seg_sum(self.kk * self.kk, self.sh["seg"], 1)
        self.g = _split_dot_left(self.sh["tri"], self.ld, 2)

    def build(self):
        f, C = self.refs, self.sh["C"]
        kk = self.kk / jnp.maximum(jnp.sqrt(self.n2), 1e-12)
        k = self.k * (1.0 + (self.a - 1.0) * f["ka"][...])
        bb = kk * self.a
        g, ld = self.g, self.ld
        self.g_last = g[C - 1:C, :]
        inv_g = jnp.exp(-g)
        to_end = jnp.exp(self.g_last - g)
        self.lhs = jnp.concatenate([self.r * jnp.exp(g), kk * jnp.exp(g - ld)], axis=0).astype(BF16)
        self.rhs = jnp.concatenate([k * inv_g, bb * inv_g], axis=0)
        self.kb_end = jnp.concatenate([k * to_end, -(bb * to_end)], axis=0)
        self.vb = self.v.astype(BF16)
        self.kmod = k

    @staticmethod
    def _grp(x, i):
        return x[:, i * RW_GROUP:(i + 1) * RW_GROUP]

    @staticmethod
    def _pair(x, i):
        return x[:, i * LANE:(i + 1) * LANE]

    def mm_state_in(self):
        st = self.refs["state"]
        p = [lax.dot_general(self._grp(self.lhs, i), st[self.bi, i].astype(BF16), (((1,), (1,)), ((), ())),
                             preferred_element_type=F32) for i in range(GROUP_W // RW_GROUP)]
        self.p = jnp.concatenate(p, axis=1)

    def mm_amat(self):
        lane2 = self.sh["lane2"]
        pair = self._pair

        def own2(hd, x):
            return jnp.where((lane2 < HEAD_DIM) == (hd % 2 == 0), x, 0.0)

        self.amat = [_bdot_nt(pair(self.lhs, hd // 2), own2(hd, pair(self.rhs, hd // 2)))
                     for hd in range(N_HEADS)]

    def mm_xinit(self):
        C, incl, strict, pair = self.sh["C"], self.sh["incl"], self.sh["strict"], self._pair
        self.a_y = [jnp.concatenate([jnp.where(incl, m[0:C, 0:C], 0.0), jnp.where(incl, -m[0:C, C:2 * C], 0.0)],
                                    axis=1).astype(BF16) for m in self.amat]
        self.xs = [pair(self.p, hd // 2)[C:2 * C, :]
                   + _bdot(jnp.where(strict, self.amat[hd][C:2 * C, 0:C], 0.0), pair(self.vb, hd // 2))
                   for hd in range(N_HEADS)]
        self.npow = [jnp.where(strict, -self.amat[hd][C:2 * C, C:2 * C], 0.0).astype(BF16)
                     for hd in range(N_HEADS)]
        self.span = 1

    def mm_level(self):
        C, span = self.sh["C"], self.span
        lo = span if span % BF16_ROWS == 0 else 0
        if 2 * span < C:
            outs = [_bdot(n[lo:, :], jnp.concatenate([n, x.astype(BF16)], axis=1))
                    for n, x in zip(self.npow, self.xs)]
            self.npow = [jnp.concatenate([jnp.zeros((lo, C), BF16), o[:, 0:C].astype(BF16)], axis=0) if lo
                         else o[:, 0:C].astype(BF16) for o in outs]
            upd = [o[:, C:] for o in outs]
        else:
            upd = [_bdot(n[lo:, :], x) for n, x in zip(self.npow, self.xs)]
        self.xs = [jnp.concatenate([x[0:lo, :], x[lo:, :] + d], axis=0) if lo else x + d
                   for x, d in zip(self.xs, upd)]
        self.span = 2 * span

    def mm_y(self):
        C, pair = self.sh["C"], self._pair
        y_h = [_bdot(self.a_y[hd], jnp.concatenate([pair(self.vb, hd // 2), self.xs[hd].astype(BF16)], axis=0))
               for hd in range(N_HEADS)]
        first = self.sh["lane1"] < HEAD_DIM
        n_pairs = GROUP_W // LANE
        y = jnp.concatenate([jnp.where(first, y_h[2 * i], y_h[2 * i + 1]) for i in range(n_pairs)], axis=1)
        self.u = jnp.concatenate([jnp.where(first, self.xs[2 * i], self.xs[2 * i + 1]) for i in range(n_pairs)],
                                 axis=1)
        self.y = y + self.p[0:C, :]

    def mm_state_out(self):
        st, seg = self.refs["state"], self.sh["seg"]
        vu_t = jnp.concatenate([self.v, self.u], axis=0).T
        decay_all = jnp.exp(self.g_last)
        for i in range(GROUP_W // RW_GROUP):
            s_new = (st[self.bi, i] * self._grp(decay_all, i)
                     + _bdot(vu_t[i * RW_GROUP:(i + 1) * RW_GROUP, :], self._grp(self.kb_end, i)))
            st[self.bi, i] = jnp.where(seg > 0.5, s_new, 0.0)

    def mm_mean(self):
        self.mean = _seg_sum(self.y, self.sh["seg"], 1) * (1.0 / HEAD_DIM)

    def mm_var_bonus(self):
        f, seg = self.refs, self.sh["seg"]
        self.yc = self.y - self.mean
        self.var = _seg_sum(self.yc * self.yc, seg, 1) * (1.0 / HEAD_DIM)
        self.bonus = _seg_sum(self.r * self.kmod * f["rk"][...], seg, 1) * self.v

    def finish(self):
        f = self.refs
        yn = self.yc * lax.rsqrt(self.var + GN_EPS) * f["lng"][...] + f["lnb"][...]
        f["o"][self.bi] = ((yn + self.bonus) * _silu(f["z"][self.bi])).astype(f["o"].dtype)


def _rwkv_kernel(r_ref, k_ref, v_ref, wa_ref, z_ref,
                 w0_ref, w2_ref, a0_ref, a2_ref, kk_ref, ka_ref, rk_ref, lng_ref, lnb_ref,
                 seg_ref, tri_ref,
                 o_ref,
                 state_ref):
    c = pl.program_id(1)
    C = r_ref.shape[1]

    @pl.when(c == 0)
    def _():
        state_ref[...] = jnp.zeros_like(state_ref)

    ti = lax.broadcasted_iota(jnp.int32, (C, C), 0)
    si = lax.broadcasted_iota(jnp.int32, (C, C), 1)
    shared = {"C": C, "seg": seg_ref[...], "tri": tri_ref[...], "incl": si <= ti, "strict": si < ti,
              "lane2": lax.broadcasted_iota(jnp.int32, (2 * C, LANE), 1),
              "lane1": lax.broadcasted_iota(jnp.int32, (C, LANE), 1)}
    refs = {"r": r_ref, "k": k_ref, "v": v_ref, "wa": wa_ref, "z": z_ref, "o": o_ref,
            "w0": w0_ref, "w2": w2_ref, "a0": a0_ref, "a2": a2_ref, "kk": kk_ref, "ka": ka_ref, "rk": rk_ref,
            "lng": lng_ref, "lnb": lnb_ref, "state": state_ref}
    streams = [_RwkvStream(bi, refs, shared) for bi in range(r_ref.shape[0])]
    n_levels = C.bit_length() - 1

    first = streams[0]
    first.load()
    first.mm_lora()
    first.decay()
    first.mm_norm_cumsum()
    first.build()
    for i, cur in enumerate(streams):
        prv = streams[i - 1] if i > 0 else None
        nxt = streams[i + 1] if i + 1 < len(streams) else None
        cur.mm_state_in()
        cur.mm_amat()
        if prv is not None:
            prv.mm_mean()
        if nxt is not None:
            nxt.load()
            nxt.mm_lora()
            nxt.decay()
        cur.mm_xinit()
        if prv is not None:
            prv.mm_var_bonus()
            prv.finish()
        if nxt is not None:
            nxt.mm_norm_cumsum()
            nxt.build()
        for _ in range(n_levels):
            cur.mm_level()
        cur.mm_y()
        cur.mm_state_out()
    last = streams[-1]
    last.mm_mean()
    last.mm_var_bonus()
    last.finish()


def _rwkv(proj, prm, batch, seq):
    C = RW_CHUNK
    nb = RW_STREAMS

    def col(base):
        return pl.BlockSpec((nb, C, GROUP_W), lambda b, c, base=base: (b, c, base // GROUP_W))

    def vec():
        return pl.BlockSpec((1, GROUP_W), lambda b, c: (0, 0))

    def whole(shape):
        return pl.BlockSpec(shape, lambda b, c: tuple(0 for _ in shape))

    in_specs = [col(COL_R), col(COL_K), col(COL_V),
                pl.BlockSpec((nb, C, LANE), lambda b, c: (b, c, COL_WA // LANE)),
                col(COL_ZA),
                vec(), whole((LANE, GROUP_W)), vec(), whole((LANE, GROUP_W)),
                vec(), vec(), vec(), vec(), vec(),
                whole((RW_GROUP, RW_GROUP)), whole((C, C))]
    return pl.pallas_call(
        _rwkv_kernel,
        grid=(batch // nb, seq // C),
        in_specs=in_specs,
        out_specs=pl.BlockSpec((nb, C, GROUP_W), lambda b, c: (b, c, 0)),
        out_shape=jax.ShapeDtypeStruct((batch, seq, GROUP_W), BF16),
        scratch_shapes=[pltpu.VMEM((nb, GROUP_W // RW_GROUP, RW_GROUP, RW_GROUP), F32)],
        compiler_params=_cparams(("parallel", "arbitrary")),
        name="rwkv7",
    )(proj, proj, proj, proj, proj,
      prm["w0"], prm["w2p"], prm["a0"], prm["a2p"], prm["k_k"], prm["k_a"], prm["r_k"],
      prm["lnx_g"], prm["lnx_b"], prm["seg"], prm["tri"])


def _lru_kernel(xc_ref, z_ref, gaw_ref, gab_ref, gxw_ref, gxb_ref, lam_ref, og_ref, o_ref, hprev_ref):
    c = pl.program_id(1)
    ct = xc_ref.shape[0]

    @pl.when(c == 0)
    def _():
        hprev_ref[...] = jnp.zeros_like(hprev_ref)

    xc = xc_ref[...]
    gate_r = jax.nn.sigmoid(_bdot(xc, gaw_ref[...]) + gab_ref[...])
    gate_i = jax.nn.sigmoid(_bdot(xc, gxw_ref[...]) + gxb_ref[...])
    log_a = -LRU_C * gate_r * jax.nn.softplus(-lam_ref[...])
    acc_a = jnp.exp(log_a)
    th = jnp.tanh(log_a)
    acc_b = jnp.sqrt(-2.0 * th / (1.0 - th)) * (gate_i * xc)
    n_groups = ct // SUBLANES
    acc_a = acc_a.reshape(n_groups, SUBLANES, GROUP_W)
    acc_b = acc_b.reshape(n_groups, SUBLANES, GROUP_W)
    sub = lax.broadcasted_iota(jnp.int32, acc_a.shape, 1)
    d = 1
    while d < SUBLANES:
        keep = sub >= d
        acc_b = acc_a * jnp.where(keep, pltpu.roll(acc_b, d, 1), 0.0) + acc_b
        acc_a = acc_a * jnp.where(keep, pltpu.roll(acc_a, d, 1), 1.0)
        d *= 2
    carry = hprev_ref[...]
    groups = []
    for i in range(n_groups):
        h_i = acc_a[i] * carry + acc_b[i]
        groups.append(h_i)
        carry = h_i[SUBLANES - 1:SUBLANES, :]
    hs = jnp.concatenate(groups, axis=0)
    hprev_ref[...] = hs[ct - 1:ct, :]
    ms = jnp.mean(hs * hs, axis=-1, keepdims=True)
    yb = hs * lax.rsqrt(ms + NORM_EPS) * og_ref[...]
    o_ref[...] = (yb * _silu(z_ref[...])).astype(o_ref.dtype)


def _lru(proj, prm, batch, seq, ct):
    def vec():
        return pl.BlockSpec((1, GROUP_W), lambda b, c: (0, 0))

    def mat(rows):
        return pl.BlockSpec((rows, GROUP_W), lambda b, c: (0, 0))

    return pl.pallas_call(
        _lru_kernel,
        grid=(batch, seq // ct),
        in_specs=[pl.BlockSpec((None, ct, GROUP_W), lambda b, c: (b, c, COL_UB // GROUP_W)),
                  pl.BlockSpec((None, ct, GROUP_W), lambda b, c: (b, c, COL_ZB // GROUP_W)),
                  mat(GROUP_W), vec(), mat(GROUP_W), vec(), vec(), vec()],
        out_specs=pl.BlockSpec((None, ct, GROUP_W), lambda b, c: (b, c, 0)),
        out_shape=jax.ShapeDtypeStruct((batch, seq, GROUP_W), BF16),
        scratch_shapes=[pltpu.VMEM((1, GROUP_W), F32)],
        compiler_params=_cparams(("parallel", "arbitrary")),
        name="rglru",
    )(proj, proj, prm["ga_w"], prm["ga_b"], prm["gx_w"], prm["gx_b"],
      prm["lam"], prm["lru_out_g"])


def _mla_pro_kernel(ql_ref, kvl_ref, kr_ref, qg_ref, kvg_ref, wq_ref, wqs_ref, wk_ref, wv_ref,
                    cq_ref, sq_ref, ck_ref, sk_ref, q_ref, k_ref, v_ref):
    def heads(tab_ref):
        return jnp.concatenate([tab_ref[...]] * N_HEADS, axis=1)

    ql = ql_ref[...]
    qn = ql * lax.rsqrt(jnp.mean(ql * ql, axis=-1, keepdims=True) + NORM_EPS) * qg_ref[...]
    qn = qn.astype(BF16)
    q = jnp.dot(qn, wq_ref[...], preferred_element_type=F32) * heads(cq_ref)
    q = q + jnp.dot(qn, wqs_ref[...], preferred_element_type=F32) * heads(sq_ref)
    q_ref[...] = q.astype(q_ref.dtype)

    kvl = kvl_ref[...]
    kvn = kvl * lax.rsqrt(jnp.mean(kvl * kvl, axis=-1, keepdims=True) + NORM_EPS) * kvg_ref[...]
    kvn = kvn.astype(BF16)
    kr = kr_ref[...]
    roped = kr * ck_ref[...] + pltpu.roll(kr * sk_ref[...], LANE - QK_ROPE, 1)
    placed = pltpu.roll(roped, QK_NOPE, 1)
    kfull = jnp.dot(kvn, wk_ref[...], preferred_element_type=F32) + jnp.concatenate([placed] * N_HEADS, axis=1)
    k_ref[...] = kfull.astype(k_ref.dtype)
    vt = lax.dot_general(wv_ref[...], kvn, (((1,), (1,)), ((), ())), preferred_element_type=F32)
    kblk = v_ref.shape[-1]
    for i in range(v_ref.shape[0]):
        v_ref[i] = vt[:, i * kblk:(i + 1) * kblk].astype(v_ref.dtype)


def _mla_pro(proj, prm, rope, batch, seq, tm, kblk):
    qk_w = N_HEADS * HEAD_PAD

    def whole(shape):
        return pl.BlockSpec(shape, lambda b, t: tuple(0 for _ in shape))

    def tab():
        return pl.BlockSpec((tm, LANE), lambda b, t: (t, 0))

    return pl.pallas_call(
        _mla_pro_kernel,
        grid=(batch, seq // tm),
        in_specs=[pl.BlockSpec((None, tm, Q_LORA), lambda b, t: (b, t, COL_QLAT // Q_LORA)),
                  pl.BlockSpec((None, tm, KV_LORA), lambda b, t: (b, t, COL_KVLAT // KV_LORA)),
                  pl.BlockSpec((None, tm, LANE), lambda b, t: (b, t, COL_KR // LANE)),
                  whole((1, Q_LORA)), whole((1, KV_LORA)),
                  whole((Q_LORA, qk_w)), whole((Q_LORA, qk_w)), whole((KV_LORA, qk_w)), whole((GROUP_W, KV_LORA)),
                  tab(), tab(), tab(), tab()],
        out_specs=[pl.BlockSpec((None, tm, qk_w), lambda b, t: (b, t, 0)),
                   pl.BlockSpec((None, tm, qk_w), lambda b, t: (b, t, 0)),
                   pl.BlockSpec((None, tm // kblk, GROUP_W, kblk), lambda b, t: (b, t, 0, 0))],
        out_shape=[jax.ShapeDtypeStruct((batch, seq, qk_w), BF16),
                   jax.ShapeDtypeStruct((batch, seq, qk_w), BF16),
                   jax.ShapeDtypeStruct((batch, seq // kblk, GROUP_W, kblk), BF16)],
        compiler_params=_cparams(("parallel", "parallel")),
        name="mla_pro",
    )(proj, proj, proj, prm["q_norm_g"], prm["kv_norm_g"], prm["wq"], prm["wq_sw"], prm["wk"], prm["wv_t"],
      rope["cq"], rope["sq"], rope["ck"], rope["sk"])


def _attn_kernel(q_ref, k_ref, vt_ref, o_ref, m_ref, acc_ref, sa_ref, sb_ref, mxa_ref, mxb_ref, *, blk, kblk):
    qi = pl.program_id(2)
    m_ref[...] = jnp.full_like(m_ref, -jnp.inf)
    acc_ref[...] = jnp.zeros_like(acc_ref)
    qs = [q_ref[:, hh * HEAD_PAD:(hh + 1) * HEAD_PAD] for hh in range(2)]
    ones = jnp.ones((ATT_ONES_ROWS, kblk), BF16)
    key = lax.broadcasted_iota(jnp.int32, (kblk, blk), 0)
    qry = lax.broadcasted_iota(jnp.int32, (kblk, blk), 1)

    def scores(t, s_ref, mx_ref):
        off = pl.multiple_of(t * kblk, kblk)
        for hh in range(2):
            s = lax.dot_general(k_ref[pl.ds(off, kblk), hh * HEAD_PAD:(hh + 1) * HEAD_PAD], qs[hh],
                                (((1,), (1,)), ((), ())), preferred_element_type=F32)
            s_ref[hh] = s
            mx_ref[hh:hh + 1, :] = jnp.max(s, axis=0, keepdims=True)

    def consume(t, s_ref, mx_ref, key_shift):
        for hh in range(2):
            s = s_ref[hh]
            if key_shift is None:
                mx = mx_ref[hh:hh + 1, :]
            else:
                s = jnp.where(key + key_shift <= qry, s, -jnp.inf)
                mx = jnp.max(s, axis=0, keepdims=True)
            m_prev = m_ref[hh:hh + 1, :]
            m_new = jnp.maximum(m_prev, mx)
            alpha = jnp.exp2(m_prev - m_new)
            p = jnp.exp2(s - m_new).astype(BF16)
            m_ref[hh:hh + 1, :] = m_new
            rows = slice(hh * ATT_ACC_ROWS, (hh + 1) * ATT_ACC_ROWS)
            vt_ext = jnp.concatenate([vt_ref[t, hh * V_DIM:(hh + 1) * V_DIM, :], ones], axis=0)
            acc_ref[rows, :] = alpha * acc_ref[rows, :] + jnp.dot(vt_ext, p, preferred_element_type=F32)

    scores(0, sa_ref, mxa_ref)

    def pair(u, carry):
        t = 2 * u
        scores(t + 1, sb_ref, mxb_ref)
        consume(t, sa_ref, mxa_ref, None)
        scores(t + 2, sa_ref, mxa_ref)
        consume(t + 1, sb_ref, mxb_ref, None)
        return carry

    lax.fori_loop(0, qi, pair, 0)
    t = 2 * qi
    scores(t + 1, sb_ref, mxb_ref)
    consume(t, sa_ref, mxa_ref, 0)
    consume(t + 1, sb_ref, mxb_ref, kblk)
    outs = []
    for hh in range(2):
        base = hh * ATT_ACC_ROWS
        outs.append(acc_ref[base:base + V_DIM, :] / acc_ref[base + V_DIM:base + V_DIM + 1, :])
    o_ref[...] = jnp.concatenate(outs, axis=0).T


def _attn(q, k, vt, batch, seq, blk, kblk):
    pair_w = 2 * HEAD_PAD
    s_buf = pltpu.VMEM((2, kblk, blk), F32)
    mx_buf = pltpu.VMEM((SUBLANES, blk), F32)
    return pl.pallas_call(
        functools.partial(_attn_kernel, blk=blk, kblk=kblk),
        grid=(batch, N_HEADS // 2, seq // blk),
        in_specs=[pl.BlockSpec((None, blk, pair_w), lambda b, h, i: (b, i, h)),
                  pl.BlockSpec((None, seq, pair_w), lambda b, h, i: (b, 0, h)),
                  pl.BlockSpec((None, seq // kblk, LANE, kblk), lambda b, h, i: (b, 0, h, 0))],
        out_specs=pl.BlockSpec((None, blk, LANE), lambda b, h, i: (b, i, h)),
        out_shape=jax.ShapeDtypeStruct((batch, seq, GROUP_W), F32),
        scratch_shapes=[pltpu.VMEM((SUBLANES, blk), F32), pltpu.VMEM((2 * ATT_ACC_ROWS, blk), F32),
                        s_buf, s_buf, mx_buf, mx_buf],
        compiler_params=_cparams(("parallel", "parallel", "arbitrary")),
        name="mla_attn",
    )(q, k, vt)


def _outproj_kernel(ya_ref, yb_ref, oc_ref, zc_ref, gc_ref, w_ref, h_ref, fg_ref, o_ref, *, final):
    oc = oc_ref[...]
    yc = oc * lax.rsqrt(jnp.mean(oc * oc, axis=-1, keepdims=True) + NORM_EPS) * gc_ref[...]
    yc = yc * _silu(zc_ref[...])
    acc = h_ref[...]
    acc = acc + jnp.dot(ya_ref[...], w_ref[0:GROUP_W, :], preferred_element_type=F32)
    acc = acc + jnp.dot(yb_ref[...], w_ref[GROUP_W:2 * GROUP_W, :], preferred_element_type=F32)
    acc = acc + _bdot(yc, w_ref[2 * GROUP_W:3 * GROUP_W, :])
    if final:
        acc = acc * lax.rsqrt(jnp.mean(acc * acc, axis=-1, keepdims=True) + NORM_EPS) * fg_ref[...]
    o_ref[...] = acc


def _outproj(ya, yb, oc, proj, h2d, mla_out_g, w_out, final_g, final, tm):
    m = h2d.shape[0]

    def rows(width, blk=0):
        return pl.BlockSpec((tm, width), lambda i, blk=blk: (i, blk))

    return pl.pallas_call(
        functools.partial(_outproj_kernel, final=final),
        grid=(m // tm,),
        in_specs=[rows(GROUP_W), rows(GROUP_W), rows(GROUP_W), rows(GROUP_W, COL_ZC // GROUP_W),
                  pl.BlockSpec((1, GROUP_W), lambda i: (0, 0)),
                  pl.BlockSpec((3 * GROUP_W, D_MODEL), lambda i: (0, 0)),
                  rows(D_MODEL),
                  pl.BlockSpec((1, D_MODEL), lambda i: (0, 0))],
        out_specs=rows(D_MODEL),
        out_shape=jax.ShapeDtypeStruct((m, D_MODEL), F32),
        compiler_params=_cparams(("parallel",)),
        name="outproj_final" if final else "outproj",
    )(ya, yb, oc, proj, mla_out_g, w_out, h2d, final_g)


def _rope_tables(seq):
    half = QK_ROPE // 2
    inv_freq = ROPE_BASE ** (-jnp.arange(half, dtype=F32) * 2.0 / QK_ROPE)
    ang = jnp.arange(seq, dtype=F32)[:, None] * inv_freq[None, :]
    cos2 = jnp.concatenate([jnp.cos(ang)] * 2, axis=1)
    sin2 = jnp.concatenate([jnp.sin(ang)] * 2, axis=1)
    zeros = lambda w: jnp.zeros((seq, w), F32)
    ones = jnp.ones((seq, QK_NOPE), F32)
    return {
        "cq": jnp.concatenate([ones, cos2, zeros(HEAD_PAD - QK_NOPE - QK_ROPE)], axis=1),
        "sq": jnp.concatenate([zeros(QK_NOPE), sin2, zeros(HEAD_PAD - QK_NOPE - QK_ROPE)], axis=1),
        "ck": jnp.concatenate([cos2, zeros(LANE - QK_ROPE)], axis=1),
        "sk": jnp.concatenate([zeros(QK_ROPE), sin2, zeros(LANE - 2 * QK_ROPE)], axis=1),
    }


def _swap_halves(w):
    half = w.shape[-1] // 2
    return jnp.concatenate([-w[..., half:], w[..., :half]], axis=-1)


def _block_diag(w):
    h, n, _ = w.shape
    eye = jnp.eye(h, dtype=w.dtype)
    return (eye[:, None, :, None] * w[:, :, None, :]).reshape(h * n, h * n)


def _layer_params(l, ln_g, w_in, rwkv_mu, rwkv_w0, rwkv_w2, rwkv_a0, rwkv_a2, rwkv_k_k, rwkv_k_a, rwkv_r_k,
                  rwkv_lnx_g, rwkv_lnx_b, lru_conv_w, lru_conv_b, lru_ga_w, lru_ga_b, lru_gx_w, lru_gx_b,
                  lru_lam, lru_out_g, mla_q_norm_g, mla_w_uq, mla_kv_norm_g, mla_w_ukv, mla_out_g, w_out):
    G = GROUP_W
    w = w_in[l]
    o_ua, o_za = 0, 3 * G + 2 * LORA
    o_ub = o_za + G
    o_zb = o_ub + G
    o_ql = o_zb + G
    o_kv = o_ql + Q_LORA
    o_kr = o_kv + KV_LORA
    o_zc = o_kr + QK_ROPE
    sl = lambda a, n: w[:, a:a + n]
    kr_w = sl(o_kr, QK_ROPE)
    w_re = jnp.concatenate([
        sl(o_ua, G), sl(o_ua + G, G), sl(o_ua + 2 * G, G), sl(o_za, G), sl(o_ub, G), sl(o_zb, G), sl(o_zc, G),
        sl(o_ql, Q_LORA), sl(o_kv, KV_LORA),
        kr_w, _swap_halves(kr_w), jnp.zeros((D_MODEL, LANE - 2 * QK_ROPE), F32),
        sl(o_ua + 3 * G, 2 * LORA)], axis=1).astype(BF16)
    row = lambda v: v.reshape(1, -1)
    mu = rwkv_mu[l]
    zl = jnp.zeros((LORA, G), F32)
    head_id = jnp.arange(RW_GROUP) // HEAD_DIM
    t_idx = jnp.arange(RW_CHUNK)
    scale = (QK_NOPE + QK_ROPE) ** -0.5 * LOG2E
    wq3 = mla_w_uq[l].reshape(Q_LORA, N_HEADS, QK_NOPE + QK_ROPE) * scale
    zq = lambda n: jnp.zeros((Q_LORA, N_HEADS, n), F32)
    wq = jnp.concatenate([wq3, zq(HEAD_PAD - QK_NOPE - QK_ROPE)], axis=2)
    wq_sw = jnp.concatenate([zq(QK_NOPE), _swap_halves(wq3[:, :, QK_NOPE:]),
                             zq(HEAD_PAD - QK_NOPE - QK_ROPE)], axis=2)
    wkv3 = mla_w_ukv[l].reshape(KV_LORA, N_HEADS, QK_NOPE + V_DIM)
    wk = jnp.concatenate([wkv3[:, :, :QK_NOPE], jnp.zeros((KV_LORA, N_HEADS, HEAD_PAD - QK_NOPE), F32)], axis=2)
    return {
        "ln_g": row(ln_g[l]), "w_in": w_re,
        "mu_rkv": row(mu[0:3 * G]),
        "mu_wa": row(mu[3 * G:3 * G + 2 * LORA]),
        "w0": row(rwkv_w0[l]), "w2p": jnp.concatenate([rwkv_w2[l], zl], axis=0).astype(BF16),
        "a0": row(rwkv_a0[l]), "a2p": jnp.concatenate([zl, rwkv_a2[l]], axis=0).astype(BF16),
        "k_k": row(rwkv_k_k[l]), "k_a": row(rwkv_k_a[l]), "r_k": row(rwkv_r_k[l]),
        "lnx_g": row(rwkv_lnx_g[l]), "lnx_b": row(rwkv_lnx_b[l]),
        "seg": (head_id[:, None] == head_id[None, :]).astype(BF16),
        "tri": (t_idx[None, :] <= t_idx[:, None]).astype(BF16),
        "conv_w": lru_conv_w[l], "conv_b": row(lru_conv_b[l]),
        "ga_w": _block_diag(lru_ga_w[l]).astype(BF16), "ga_b": row(lru_ga_b[l]),
        "gx_w": _block_diag(lru_gx_w[l]).astype(BF16), "gx_b": row(lru_gx_b[l]),
        "lam": row(lru_lam[l]), "lru_out_g": row(lru_out_g[l]),
        "q_norm_g": row(mla_q_norm_g[l]), "kv_norm_g": row(mla_kv_norm_g[l]),
        "wq": wq.reshape(Q_LORA, -1).astype(BF16), "wq_sw": wq_sw.reshape(Q_LORA, -1).astype(BF16),
        "wk": wk.reshape(KV_LORA, -1).astype(BF16),
        "wv_t": wkv3[:, :, QK_NOPE:].reshape(KV_LORA, -1).T.astype(BF16),
        "mla_out_g": row(mla_out_g[l]), "w_out": w_out[l].astype(BF16),
    }


def kernel(x, ln_g, w_in, rwkv_mu, rwkv_w0, rwkv_w2, rwkv_a0, rwkv_a2, rwkv_k_k, rwkv_k_a, rwkv_r_k,
           rwkv_lnx_g, rwkv_lnx_b, lru_conv_w, lru_conv_b, lru_ga_w, lru_ga_b, lru_gx_w, lru_gx_b, lru_lam,
           lru_out_g, mla_q_norm_g, mla_w_uq, mla_kv_norm_g, mla_w_ukv, mla_out_g, w_out, final_g):
    batch, seq, _ = x.shape
    assert batch % RW_STREAMS == 0 and seq % RW_CHUNK == 0, (batch, seq)
    tm = min(512, seq)
    rope = _rope_tables(seq)
    fg = final_g.reshape(1, -1)
    h = x.reshape(batch * seq, D_MODEL)
    for l in range(DEPTH):
        prm = _layer_params(l, ln_g, w_in, rwkv_mu, rwkv_w0, rwkv_w2, rwkv_a0, rwkv_a2, rwkv_k_k, rwkv_k_a,
                            rwkv_r_k, rwkv_lnx_g, rwkv_lnx_b, lru_conv_w, lru_conv_b, lru_ga_w, lru_ga_b,
                            lru_gx_w, lru_gx_b, lru_lam, lru_out_g, mla_q_norm_g, mla_w_uq, mla_kv_norm_g,
                            mla_w_ukv, mla_out_g, w_out)
        proj2d = _inproj(h, prm, tm, seq)
        proj = proj2d.reshape(batch, seq, D_PROJ)
        ya = _rwkv(proj, prm, batch, seq)
        yb = _lru(proj, prm, batch, seq, min(256, seq))
        blk = min(512, seq)
        q, k, vt = _mla_pro(proj, prm, rope, batch, seq, blk, blk // 2)
        oc = _attn(q, k, vt, batch, seq, blk, blk // 2)
        h = _outproj(ya.reshape(batch * seq, GROUP_W), yb.reshape(batch * seq, GROUP_W),
                     oc.reshape(batch * seq, GROUP_W), proj2d, h, prm["mla_out_g"], prm["w_out"], fg,
                     l == DEPTH - 1, tm)
    return h.reshape(batch, seq, D_MODEL)
```

```python
import functools

import jax
import jax.numpy as jnp
from jax import lax
from jax.experimental import pallas as pl
from jax.experimental.pallas import tpu as pltpu

F32 = jnp.float32
BF16 = jnp.bfloat16

D_MODEL = 1024
DEPTH = 4
GROUP_W = 512
HEAD_DIM = 64
N_HEADS = 8
NORM_EPS = 1e-6
LORA = 64
GN_EPS = 64e-5
CONV_W = 4
LRU_C = 8.0
Q_LORA = 256
KV_LORA = 128
QK_NOPE = 64
QK_ROPE = 32
V_DIM = 64
ROPE_BASE = 10000.0

LANE = 128
SUBLANES = 8
HEAD_PAD = 128
RW_GROUP = 256
RW_CHUNK = 128
RW_STREAMS = 4
BF16_ROWS = 16
ATT_ONES_ROWS = BF16_ROWS
ATT_ACC_ROWS = V_DIM + ATT_ONES_ROWS
ATT_AHEAD = 2
LOG2E = 1.4426950408889634
COL_R, COL_K, COL_V, COL_ZA, COL_UB, COL_ZB, COL_ZC = (i * GROUP_W for i in range(7))
COL_QLAT = 7 * GROUP_W
COL_KVLAT = COL_QLAT + Q_LORA
COL_KR = COL_KVLAT + KV_LORA
COL_WA = COL_KR + LANE
D_PROJ = COL_WA + LANE

VMEM_LIMIT = 56 * 1024 * 1024


def _cparams(sem):
    return pltpu.CompilerParams(dimension_semantics=sem, vmem_limit_bytes=VMEM_LIMIT)


def _bdot(a, b):
    return jnp.dot(a.astype(BF16), b.astype(BF16), preferred_element_type=F32)


def _bdot_nt(a, b):
    return lax.dot_general(a.astype(BF16), b.astype(BF16), (((1,), (1,)), ((), ())),
                           preferred_element_type=F32)


def _split_dot(x, m, passes):
    acc = None
    rem = x
    for i in range(passes):
        part = rem.astype(BF16)
        term = jnp.dot(part, m, preferred_element_type=F32)
        acc = term if acc is None else acc + term
        if i + 1 < passes:
            rem = rem - part.astype(F32)
    return acc


def _silu(z):
    return z * jax.nn.sigmoid(z)


def _shift_rows(x, tail, d):
    rolled = pltpu.roll(x, d, 0)
    row8 = lax.broadcasted_iota(jnp.int32, tail.shape, 0)
    first = jnp.where(row8 < d, pltpu.roll(tail, d, 0), rolled[0:SUBLANES, :])
    return jnp.concatenate([first, rolled[SUBLANES:, :]], axis=0)


def _inproj_kernel(x_ref, g_ref, w_ref, mu_ref, muwa_ref, cw_ref, cb_ref, o_ref, tail_ref, *, tiles_per_seq):
    i = pl.program_id(0)
    tm = x_ref.shape[0]
    x = x_ref[...]
    ms = jnp.mean(x * x, axis=-1, keepdims=True)
    xn = x * lax.rsqrt(ms + NORM_EPS) * g_ref[...]
    y = _bdot(xn, w_ref[...])
    tail = jnp.where(i % tiles_per_seq == 0, 0.0, tail_ref[...])
    tail_ref[...] = y[tm - SUBLANES:tm, :]

    def lerp(lo, width, mu):
        cur = y[:, lo:lo + width]
        return cur + mu * (_shift_rows(cur, tail[:, lo:lo + width], 1) - cur)

    o_ref[:, COL_R:COL_ZA] = lerp(COL_R, COL_ZA - COL_R, mu_ref[...])
    o_ref[:, COL_ZA:COL_UB] = y[:, COL_ZA:COL_UB]
    u = y[:, COL_UB:COL_ZB]
    u_tail = tail[:, COL_UB:COL_ZB]
    xc = cb_ref[...] + u * cw_ref[CONV_W - 1:CONV_W, :]
    for d in range(1, CONV_W):
        xc = xc + _shift_rows(u, u_tail, d) * cw_ref[CONV_W - 1 - d:CONV_W - d, :]
    o_ref[:, COL_UB:COL_ZB] = xc
    o_ref[:, COL_ZB:COL_WA] = y[:, COL_ZB:COL_WA]
    o_ref[:, COL_WA:D_PROJ] = lerp(COL_WA, D_PROJ - COL_WA, muwa_ref[...])


def _inproj(h2d, prm, tm, seq):
    m = h2d.shape[0]

    def whole(shape):
        return pl.BlockSpec(shape, lambda i: tuple(0 for _ in shape))

    return pl.pallas_call(
        functools.partial(_inproj_kernel, tiles_per_seq=seq // tm),
        grid=(m // tm,),
        in_specs=[pl.BlockSpec((tm, D_MODEL), lambda i: (i, 0)),
                  whole((1, D_MODEL)), whole((D_MODEL, D_PROJ)),
                  whole((1, COL_ZA - COL_R)), whole((1, LANE)), whole((CONV_W, GROUP_W)), whole((1, GROUP_W))],
        out_specs=pl.BlockSpec((tm, D_PROJ), lambda i: (i, 0)),
        out_shape=jax.ShapeDtypeStruct((m, D_PROJ), F32),
        scratch_shapes=[pltpu.VMEM((SUBLANES, D_PROJ), F32)],
        compiler_params=_cparams(("arbitrary",)),
        name="inproj",
    )(h2d, prm["ln_g"], prm["w_in"], prm["mu_rkv"], prm["mu_wa"], prm["conv_w"], prm["conv_b"])


def _split_dot_left(m, x, passes):
    acc = None
    rem = x
    for i in range(passes):
        part = rem.astype(BF16)
        term = jnp.dot(m, part, preferred_element_type=F32)
        acc = term if acc is None else acc + term
        if i + 1 < passes:
            rem = rem - part.astype(F32)
    return acc


def _seg_sum(x, seg, passes):
    parts = [_split_dot(x[:, i:i + RW_GROUP], seg, passes) for i in range(0, x.shape[1], RW_GROUP)]
    return jnp.concatenate(parts, axis=1)


class _RwkvStream:
    def __init__(self, bi, refs, shared):
        self.bi = bi
        self.refs = refs
        self.sh = shared

    def load(self):
        f = self.refs
        self.r, self.k, self.v, self.wa = f["r"][self.bi], f["k"][self.bi], f["v"][self.bi], f["wa"][self.bi]

    def mm_lora(self):
        f = self.refs
        self.lw = _bdot(jnp.tanh(self.wa), f["w2"][...])
        self.la = _bdot(self.wa, f["a2"][...])

    def decay(self):
        f = self.refs
        log_w = -jax.nn.softplus(-(f["w0"][...] + self.lw)) - 0.5
        self.ld = -jnp.exp(log_w)
        self.a = jax.nn.sigmoid(f["a0"][...] + self.la)
        self.kk = self.k * f["kk"][...]

    def mm_norm_cumsum(self):
        self.n2 = _seg_sum(self.kk * self.kk, self.sh["seg"], 1)
        self.g = _split_dot_left(self.sh["tri"], self.ld, 2)

    def build(self):
        f, C = self.refs, self.sh["C"]
        kk = self.kk / jnp.maximum(jnp.sqrt(self.n2), 1e-12)
        k = self.k * (1.0 + (self.a - 1.0) * f["ka"][...])
        bb = kk * self.a
        g, ld = self.g, self.ld
        self.g_last = g[C - 1:C, :]
        inv_g = jnp.exp(-g)
        to_end = jnp.exp(self.g_last - g)
        self.lhs = jnp.concatenate([self.r * jnp.exp(g), kk * jnp.exp(g - ld)], axis=0).astype(BF16)
        self.rhs = jnp.concatenate([k * inv_g, bb * inv_g], axis=0)
        self.kb_end = jnp.concatenate([k * to_end, -(bb * to_end)], axis=0)
        self.vb = self.v.astype(BF16)
        self.kmod = k

    @staticmethod
    def _grp(x, i):
        return x[:, i * RW_GROUP:(i + 1) * RW_GROUP]

    @staticmethod
    def _pair(x, i):
        return x[:, i * LANE:(i + 1) * LANE]

    def mm_state_in(self):
        st = self.refs["state"]
        p = [lax.dot_general(self._grp(self.lhs, i), st[self.bi, i].astype(BF16), (((1,), (1,)), ((), ())),
                             preferred_element_type=F32) for i in range(GROUP_W // RW_GROUP)]
        self.p = jnp.concatenate(p, axis=1)

    def mm_amat(self):
        lane2 = self.sh["lane2"]
        pair = self._pair

        def own2(hd, x):
            return jnp.where((lane2 < HEAD_DIM) == (hd % 2 == 0), x, 0.0)

        self.amat = [_bdot_nt(pair(self.lhs, hd // 2), own2(hd, pair(self.rhs, hd // 2)))
                     for hd in range(N_HEADS)]

    def mm_xinit(self):
        C, incl, strict, pair = self.sh["C"], self.sh["incl"], self.sh["strict"], self._pair
        self.a_y = [jnp.concatenate([jnp.where(incl, m[0:C, 0:C], 0.0), jnp.where(incl, -m[0:C, C:2 * C], 0.0)],
                                    axis=1).astype(BF16) for m in self.amat]
        self.xs = [pair(self.p, hd // 2)[C:2 * C, :]
                   + _bdot(jnp.where(strict, self.amat[hd][C:2 * C, 0:C], 0.0), pair(self.vb, hd // 2))
                   for hd in range(N_HEADS)]
        self.npow = [jnp.where(strict, -self.amat[hd][C:2 * C, C:2 * C], 0.0).astype(BF16)
                     for hd in range(N_HEADS)]
        self.span = 1

    def mm_level(self):
        C, span = self.sh["C"], self.span
        lo = span if span % BF16_ROWS == 0 else 0
        if 2 * span < C:
            outs = [_bdot(n[lo:, :], jnp.concatenate([n, x.astype(BF16)], axis=1))
                    for n, x in zip(self.npow, self.xs)]
            self.npow = [jnp.concatenate([jnp.zeros((lo, C), BF16), o[:, 0:C].astype(BF16)], axis=0) if lo
                         else o[:, 0:C].astype(BF16) for o in outs]
            upd = [o[:, C:] for o in outs]
        else:
            upd = [_bdot(n[lo:, :], x) for n, x in zip(self.npow, self.xs)]
        self.xs = [jnp.concatenate([x[0:lo, :], x[lo:, :] + d], axis=0) if lo else x + d
                   for x, d in zip(self.xs, upd)]
        self.span = 2 * span

    def mm_y(self):
        C, pair = self.sh["C"], self._pair
        y_h = [_bdot(self.a_y[hd], jnp.concatenate([pair(self.vb, hd // 2), self.xs[hd].astype(BF16)], axis=0))
               for hd in range(N_HEADS)]
        first = self.sh["lane1"] < HEAD_DIM
        n_pairs = GROUP_W // LANE
        y = jnp.concatenate([jnp.where(first, y_h[2 * i], y_h[2 * i + 1]) for i in range(n_pairs)], axis=1)
        self.u = jnp.concatenate([jnp.where(first, self.xs[2 * i], self.xs[2 * i + 1]) for i in range(n_pairs)],
                                 axis=1)
        self.y = y + self.p[0:C, :]

    def mm_state_out(self):
        st, seg = self.refs["state"], self.sh["seg"]
        vu_t = jnp.concatenate([self.v, self.u], axis=0).T
        decay_all = jnp.exp(self.g_last)
        for i in range(GROUP_W // RW_GROUP):
            s_new = (st[self.bi, i] * self._grp(decay_all, i)
                     + _bdot(vu_t[i * RW_GROUP:(i + 1) * RW_GROUP, :], self._grp(self.kb_end, i)))
            st[self.bi, i] = jnp.where(seg > 0.5, s_new, 0.0)

    def mm_mean(self):
        self.mean = _seg_sum(self.y, self.sh["seg"], 1) * (1.0 / HEAD_DIM)

    def mm_var_bonus(self):
        f, seg = self.refs, self.sh["seg"]
        self.yc = self.y - self.mean
        self.var = _seg_sum(self.yc * self.yc, seg, 1) * (1.0 / HEAD_DIM)
        self.bonus = _seg_sum(self.r * self.kmod * f["rk"][...], seg, 1) * self.v

    def finish(self):
        f = self.refs
        yn = self.yc * lax.rsqrt(self.var + GN_EPS) * f["lng"][...] + f["lnb"][...]
        f["o"][self.bi] = ((yn + self.bonus) * _silu(f["z"][self.bi])).astype(f["o"].dtype)


def _rwkv_kernel(r_ref, k_ref, v_ref, wa_ref, z_ref,
                 w0_ref, w2_ref, a0_ref, a2_ref, kk_ref, ka_ref, rk_ref, lng_ref, lnb_ref,
                 seg_ref, tri_ref,
                 o_ref,
                 state_ref):
    c = pl.program_id(1)
    C = r_ref.shape[1]

    @pl.when(c == 0)
    def _():
        state_ref[...] = jnp.zeros_like(state_ref)

    ti = lax.broadcasted_iota(jnp.int32, (C, C), 0)
    si = lax.broadcasted_iota(jnp.int32, (C, C), 1)
    shared = {"C": C, "seg": seg_ref[...], "tri": tri_ref[...], "incl": si <= ti, "strict": si < ti,
              "lane2": lax.broadcasted_iota(jnp.int32, (2 * C, LANE), 1),
              "lane1": lax.broadcasted_iota(jnp.int32, (C, LANE), 1)}
    refs = {"r": r_ref, "k": k_ref, "v": v_ref, "wa": wa_ref, "z": z_ref, "o": o_ref,
            "w0": w0_ref, "w2": w2_ref, "a0": a0_ref, "a2": a2_ref, "kk": kk_ref, "ka": ka_ref, "rk": rk_ref,
            "lng": lng_ref, "lnb": lnb_ref, "state": state_ref}
    streams = [_RwkvStream(bi, refs, shared) for bi in range(r_ref.shape[0])]
    n_levels = C.bit_length() - 1

    first = streams[0]
    first.load()
    first.mm_lora()
    first.decay()
    first.mm_norm_cumsum()
    first.build()
    for i, cur in enumerate(streams):
        prv = streams[i - 1] if i > 0 else None
        nxt = streams[i + 1] if i + 1 < len(streams) else None
        cur.mm_state_in()
        cur.mm_amat()
        if prv is not None:
            prv.mm_mean()
        if nxt is not None:
            nxt.load()
            nxt.mm_lora()
            nxt.decay()
        cur.mm_xinit()
        if prv is not None:
            prv.mm_var_bonus()
            prv.finish()
        if nxt is not None:
            nxt.mm_norm_cumsum()
            nxt.build()
        for _ in range(n_levels):
            cur.mm_level()
        cur.mm_y()
        cur.mm_state_out()
    last = streams[-1]
    last.mm_mean()
    last.mm_var_bonus()
    last.finish()


def _rwkv(proj, prm, batch, seq):
    C = RW_CHUNK
    nb = RW_STREAMS

    def col(base):
        return pl.BlockSpec((nb, C, GROUP_W), lambda b, c, base=base: (b, c, base // GROUP_W))

    def vec():
        return pl.BlockSpec((1, GROUP_W), lambda b, c: (0, 0))

    def whole(shape):
        return pl.BlockSpec(shape, lambda b, c: tuple(0 for _ in shape))

    in_specs = [col(COL_R), col(COL_K), col(COL_V),
                pl.BlockSpec((nb, C, LANE), lambda b, c: (b, c, COL_WA // LANE)),
                col(COL_ZA),
                vec(), whole((LANE, GROUP_W)), vec(), whole((LANE, GROUP_W)),
                vec(), vec(), vec(), vec(), vec(),
                whole((RW_GROUP, RW_GROUP)), whole((C, C))]
    return pl.pallas_call(
        _rwkv_kernel,
        grid=(batch // nb, seq // C),
        in_specs=in_specs,
        out_specs=pl.BlockSpec((nb, C, GROUP_W), lambda b, c: (b, c, 0)),
        out_shape=jax.ShapeDtypeStruct((batch, seq, GROUP_W), BF16),
        scratch_shapes=[pltpu.VMEM((nb, GROUP_W // RW_GROUP, RW_GROUP, RW_GROUP), F32)],
        compiler_params=_cparams(("parallel", "arbitrary")),
        name="rwkv7",
    )(proj, proj, proj, proj, proj,
      prm["w0"], prm["w2p"], prm["a0"], prm["a2p"], prm["k_k"], prm["k_a"], prm["r_k"],
      prm["lnx_g"], prm["lnx_b"], prm["seg"], prm["tri"])


def _lru_kernel(xc_ref, z_ref, gaw_ref, gab_ref, gxw_ref, gxb_ref, lam_ref, og_ref, o_ref, hprev_ref):
    c = pl.program_id(1)
    ct = xc_ref.shape[0]

    @pl.when(c == 0)
    def _():
        hprev_ref[...] = jnp.zeros_like(hprev_ref)

    xc = xc_ref[...]
    gate_r = jax.nn.sigmoid(_bdot(xc, gaw_ref[...]) + gab_ref[...])
    gate_i = jax.nn.sigmoid(_bdot(xc, gxw_ref[...]) + gxb_ref[...])
    log_a = -LRU_C * gate_r * jax.nn.softplus(-lam_ref[...])
    acc_a = jnp.exp(log_a)
    th = jnp.tanh(log_a)
    acc_b = jnp.sqrt(-2.0 * th / (1.0 - th)) * (gate_i * xc)
    n_groups = ct // SUBLANES
    acc_a = acc_a.reshape(n_groups, SUBLANES, GROUP_W)
    acc_b = acc_b.reshape(n_groups, SUBLANES, GROUP_W)
    sub = lax.broadcasted_iota(jnp.int32, acc_a.shape, 1)
    d = 1
    while d < SUBLANES:
        keep = sub >= d
        acc_b = acc_a * jnp.where(keep, pltpu.roll(acc_b, d, 1), 0.0) + acc_b
        acc_a = acc_a * jnp.where(keep, pltpu.roll(acc_a, d, 1), 1.0)
        d *= 2
    carry = hprev_ref[...]
    groups = []
    for i in range(n_groups):
        h_i = acc_a[i] * carry + acc_b[i]
        groups.append(h_i)
        carry = h_i[SUBLANES - 1:SUBLANES, :]
    hs = jnp.concatenate(groups, axis=0)
    hprev_ref[...] = hs[ct - 1:ct, :]
    ms = jnp.mean(hs * hs, axis=-1, keepdims=True)
    yb = hs * lax.rsqrt(ms + NORM_EPS) * og_ref[...]
    o_ref[...] = (yb * _silu(z_ref[...])).astype(o_ref.dtype)


def _lru(proj, prm, batch, seq, ct):
    def vec():
        return pl.BlockSpec((1, GROUP_W), lambda b, c: (0, 0))

    def mat(rows):
        return pl.BlockSpec((rows, GROUP_W), lambda b, c: (0, 0))

    return pl.pallas_call(
        _lru_kernel,
        grid=(batch, seq // ct),
        in_specs=[pl.BlockSpec((None, ct, GROUP_W), lambda b, c: (b, c, COL_UB // GROUP_W)),
                  pl.BlockSpec((None, ct, GROUP_W), lambda b, c: (b, c, COL_ZB // GROUP_W)),
                  mat(GROUP_W), vec(), mat(GROUP_W), vec(), vec(), vec()],
        out_specs=pl.BlockSpec((None, ct, GROUP_W), lambda b, c: (b, c, 0)),
        out_shape=jax.ShapeDtypeStruct((batch, seq, GROUP_W), BF16),
        scratch_shapes=[pltpu.VMEM((1, GROUP_W), F32)],
        compiler_params=_cparams(("parallel", "arbitrary")),
        name="rglru",
    )(proj, proj, prm["ga_w"], prm["ga_b"], prm["gx_w"], prm["gx_b"],
      prm["lam"], prm["lru_out_g"])


def _mla_pro_kernel(ql_ref, kvl_ref, kr_ref, qg_ref, kvg_ref, wq_ref, wqs_ref, wk_ref, wv_ref,
                    cq_ref, sq_ref, ck_ref, sk_ref, q_ref, k_ref, v_ref):
    def heads(tab_ref):
        return jnp.concatenate([tab_ref[...]] * N_HEADS, axis=1)

    ql = ql_ref[...]
    qn = ql * lax.rsqrt(jnp.mean(ql * ql, axis=-1, keepdims=True) + NORM_EPS) * qg_ref[...]
    qn = qn.astype(BF16)
    q = jnp.dot(qn, wq_ref[...], preferred_element_type=F32) * heads(cq_ref)
    q = q + jnp.dot(qn, wqs_ref[...], preferred_element_type=F32) * heads(sq_ref)
    q_ref[...] = q.astype(q_ref.dtype)

    kvl = kvl_ref[...]
    kvn = kvl * lax.rsqrt(jnp.mean(kvl * kvl, axis=-1, keepdims=True) + NORM_EPS) * kvg_ref[...]
    kvn = kvn.astype(BF16)
    kr = kr_ref[...]
    roped = kr * ck_ref[...] + pltpu.roll(kr * sk_ref[...], LANE - QK_ROPE, 1)
    placed = pltpu.roll(roped, QK_NOPE, 1)
    kfull = jnp.dot(kvn, wk_ref[...], preferred_element_type=F32) + jnp.concatenate([placed] * N_HEADS, axis=1)
    k_ref[...] = kfull.astype(k_ref.dtype)
    vt = lax.dot_general(wv_ref[...], kvn, (((1,), (1,)), ((), ())), preferred_element_type=F32)
    kblk = v_ref.shape[-1]
    for i in range(v_ref.shape[0]):
        v_ref[i] = vt[:, i * kblk:(i + 1) * kblk].astype(v_ref.dtype)


def _mla_pro(proj, prm, rope, batch, seq, tm, kblk):
    qk_w = N_HEADS * HEAD_PAD

    def whole(shape):
        return pl.BlockSpec(shape, lambda b, t: tuple(0 for _ in shape))

    def tab():
        return pl.BlockSpec((tm, LANE), lambda b, t: (t, 0))

    return pl.pallas_call(
        _mla_pro_kernel,
        grid=(batch, seq // tm),
        in_specs=[pl.BlockSpec((None, tm, Q_LORA), lambda b, t: (b, t, COL_QLAT // Q_LORA)),
                  pl.BlockSpec((None, tm, KV_LORA), lambda b, t: (b, t, COL_KVLAT // KV_LORA)),
                  pl.BlockSpec((None, tm, LANE), lambda b, t: (b, t, COL_KR // LANE)),
                  whole((1, Q_LORA)), whole((1, KV_LORA)),
                  whole((Q_LORA, qk_w)), whole((Q_LORA, qk_w)), whole((KV_LORA, qk_w)), whole((GROUP_W, KV_LORA)),
                  tab(), tab(), tab(), tab()],
        out_specs=[pl.BlockSpec((None, tm, qk_w), lambda b, t: (b, t, 0)),
                   pl.BlockSpec((None, tm, qk_w), lambda b, t: (b, t, 0)),
                   pl.BlockSpec((None, tm // kblk, GROUP_W, kblk), lambda b, t: (b, t, 0, 0))],
        out_shape=[jax.ShapeDtypeStruct((batch, seq, qk_w), BF16),
                   jax.ShapeDtypeStruct((batch, seq, qk_w), BF16),
                   jax.ShapeDtypeStruct((batch, seq // kblk, GROUP_W, kblk), BF16)],
        compiler_params=_cparams(("parallel", "parallel")),
        name="mla_pro",
    )(proj, proj, proj, prm["q_norm_g"], prm["kv_norm_g"], prm["wq"], prm["wq_sw"], prm["wk"], prm["wv_t"],
      rope["cq"], rope["sq"], rope["ck"], rope["sk"])


def _attn_kernel(q_ref, k_ref, vt_ref, o_ref, m_ref, acc_ref, s_ref, mx_ref, *, blk, kblk, n_qblocks):
    qi = pl.program_id(2)
    qs = [q_ref[:, hh * HEAD_PAD:(hh + 1) * HEAD_PAD] for hh in range(2)]
    ones = jnp.ones((ATT_ONES_ROWS, kblk), BF16)
    key = lax.broadcasted_iota(jnp.int32, (kblk, blk), 0)
    qry = lax.broadcasted_iota(jnp.int32, (kblk, blk), 1)
    n_buf = s_ref.shape[0]

    def scores(t):
        buf = t % n_buf
        for hh in range(2):
            s = lax.dot_general(k_ref[t * kblk:(t + 1) * kblk, hh * HEAD_PAD:(hh + 1) * HEAD_PAD], qs[hh],
                                (((1,), (1,)), ((), ())), preferred_element_type=F32)
            s_ref[buf, hh] = s
            mx_ref[buf, hh:hh + 1, :] = jnp.max(s, axis=0, keepdims=True)

    def consume(t, key_shift):
        buf = t % n_buf
        for hh in range(2):
            s = s_ref[buf, hh]
            if key_shift is None:
                mx = mx_ref[buf, hh:hh + 1, :]
            else:
                s = jnp.where(key + key_shift <= qry, s, -jnp.inf)
                mx = jnp.max(s, axis=0, keepdims=True)
            m_prev = m_ref[hh:hh + 1, :]
            m_new = jnp.maximum(m_prev, mx)
            alpha = jnp.exp2(m_prev - m_new)
            p = jnp.exp2(s - m_new).astype(BF16)
            m_ref[hh:hh + 1, :] = m_new
            rows = slice(hh * ATT_ACC_ROWS, (hh + 1) * ATT_ACC_ROWS)
            vt_ext = jnp.concatenate([vt_ref[t, hh * V_DIM:(hh + 1) * V_DIM, :], ones], axis=0)
            acc_ref[rows, :] = alpha * acc_ref[rows, :] + jnp.dot(vt_ext, p, preferred_element_type=F32)

    def query_block(i):
        n_keys = 2 * i + 2
        m_ref[...] = jnp.full_like(m_ref, -jnp.inf)
        acc_ref[...] = jnp.zeros_like(acc_ref)
        for t in range(min(ATT_AHEAD, n_keys)):
            scores(t)
        for t in range(n_keys):
            if t + ATT_AHEAD < n_keys:
                scores(t + ATT_AHEAD)
            consume(t, None if t < 2 * i else (t - 2 * i) * kblk)
        outs = []
        for hh in range(2):
            base = hh * ATT_ACC_ROWS
            outs.append(acc_ref[base:base + V_DIM, :] / acc_ref[base + V_DIM:base + V_DIM + 1, :])
        o_ref[...] = jnp.concatenate(outs, axis=0).T

    for i in range(n_qblocks):
        pl.when(qi == i)(functools.partial(query_block, i))


def _attn(q, k, vt, batch, seq, blk, kblk):
    pair_w = 2 * HEAD_PAD
    n_buf = ATT_AHEAD + 1
    return pl.pallas_call(
        functools.partial(_attn_kernel, blk=blk, kblk=kblk, n_qblocks=seq // blk),
        grid=(batch, N_HEADS // 2, seq // blk),
        in_specs=[pl.BlockSpec((None, blk, pair_w), lambda b, h, i: (b, i, h)),
                  pl.BlockSpec((None, seq, pair_w), lambda b, h, i: (b, 0, h)),
                  pl.BlockSpec((None, seq // kblk, LANE, kblk), lambda b, h, i: (b, 0, h, 0))],
        out_specs=pl.BlockSpec((None, blk, LANE), lambda b, h, i: (b, i, h)),
        out_shape=jax.ShapeDtypeStruct((batch, seq, GROUP_W), F32),
        scratch_shapes=[pltpu.VMEM((SUBLANES, blk), F32), pltpu.VMEM((2 * ATT_ACC_ROWS, blk), F32),
                        pltpu.VMEM((n_buf, 2, kblk, blk), F32), pltpu.VMEM((n_buf, SUBLANES, blk), F32)],
        compiler_params=_cparams(("parallel", "parallel", "arbitrary")),
        name="mla_attn",
    )(q, k, vt)


def _outproj_kernel(ya_ref, yb_ref, oc_ref, zc_ref, gc_ref, w_ref, h_ref, fg_ref, o_ref, *, final):
    oc = oc_ref[...]
    yc = oc * lax.rsqrt(jnp.mean(oc * oc, axis=-1, keepdims=True) + NORM_EPS) * gc_ref[...]
    yc = yc * _silu(zc_ref[...])
    acc = h_ref[...]
    acc = acc + jnp.dot(ya_ref[...], w_ref[0:GROUP_W, :], preferred_element_type=F32)
    acc = acc + jnp.dot(yb_ref[...], w_ref[GROUP_W:2 * GROUP_W, :], preferred_element_type=F32)
    acc = acc + _bdot(yc, w_ref[2 * GROUP_W:3 * GROUP_W, :])
    if final:
        acc = acc * lax.rsqrt(jnp.mean(acc * acc, axis=-1, keepdims=True) + NORM_EPS) * fg_ref[...]
    o_ref[...] = acc


def _outproj(ya, yb, oc, proj, h2d, mla_out_g, w_out, final_g, final, tm):
    m = h2d.shape[0]

    def rows(width, blk=0):
        return pl.BlockSpec((tm, width), lambda i, blk=blk: (i, blk))

    return pl.pallas_call(
        functools.partial(_outproj_kernel, final=final),
        grid=(m // tm,),
        in_specs=[rows(GROUP_W), rows(GROUP_W), rows(GROUP_W), rows(GROUP_W, COL_ZC // GROUP_W),
                  pl.BlockSpec((1, GROUP_W), lambda i: (0, 0)),
                  pl.BlockSpec((3 * GROUP_W, D_MODEL), lambda i: (0, 0)),
                  rows(D_MODEL),
                  pl.BlockSpec((1, D_MODEL), lambda i: (0, 0))],
        out_specs=rows(D_MODEL),
        out_shape=jax.ShapeDtypeStruct((m, D_MODEL), F32),
        compiler_params=_cparams(("parallel",)),
        name="outproj_final" if final else "outproj",
    )(ya, yb, oc, proj, mla_out_g, w_out, h2d, final_g)


def _rope_tables(seq):
    half = QK_ROPE // 2
    inv_freq = ROPE_BASE ** (-jnp.arange(half, dtype=F32) * 2.0 / QK_ROPE)
    ang = jnp.arange(seq, dtype=F32)[:, None] * inv_freq[None, :]
    cos2 = jnp.concatenate([jnp.cos(ang)] * 2, axis=1)
    sin2 = jnp.concatenate([jnp.sin(ang)] * 2, axis=1)
    zeros = lambda w: jnp.zeros((seq, w), F32)
    ones = jnp.ones((seq, QK_NOPE), F32)
    return {
        "cq": jnp.concatenate([ones, cos2, zeros(HEAD_PAD - QK_NOPE - QK_ROPE)], axis=1),
        "sq": jnp.concatenate([zeros(QK_NOPE), sin2, zeros(HEAD_PAD - QK_NOPE - QK_ROPE)], axis=1),
        "ck": jnp.concatenate([cos2, zeros(LANE - QK_ROPE)], axis=1),
        "sk": jnp.concatenate([zeros(QK_ROPE), sin2, zeros(LANE - 2 * QK_ROPE)], axis=1),
    }


def _swap_halves(w):
    half = w.shape[-1] // 2
    return jnp.concatenate([-w[..., half:], w[..., :half]], axis=-1)


def _block_diag(w):
    h, n, _ = w.shape
    eye = jnp.eye(h, dtype=w.dtype)
    return (eye[:, None, :, None] * w[:, :, None, :]).reshape(h * n, h * n)


def _layer_params(l, ln_g, w_in, rwkv_mu, rwkv_w0, rwkv_w2, rwkv_a0, rwkv_a2, rwkv_k_k, rwkv_k_a, rwkv_r_k,
                  rwkv_lnx_g, rwkv_lnx_b, lru_conv_w, lru_conv_b, lru_ga_w, lru_ga_b, lru_gx_w, lru_gx_b,
                  lru_lam, lru_out_g, mla_q_norm_g, mla_w_uq, mla_kv_norm_g, mla_w_ukv, mla_out_g, w_out):
    G = GROUP_W
    w = w_in[l]
    o_ua, o_za = 0, 3 * G + 2 * LORA
    o_ub = o_za + G
    o_zb = o_ub + G
    o_ql = o_zb + G
    o_kv = o_ql + Q_LORA
    o_kr = o_kv + KV_LORA
    o_zc = o_kr + QK_ROPE
    sl = lambda a, n: w[:, a:a + n]
    kr_w = sl(o_kr, QK_ROPE)
    w_re = jnp.concatenate([
        sl(o_ua, G), sl(o_ua + G, G), sl(o_ua + 2 * G, G), sl(o_za, G), sl(o_ub, G), sl(o_zb, G), sl(o_zc, G),
        sl(o_ql, Q_LORA), sl(o_kv, KV_LORA),
        kr_w, _swap_halves(kr_w), jnp.zeros((D_MODEL, LANE - 2 * QK_ROPE), F32),
        sl(o_ua + 3 * G, 2 * LORA)], axis=1).astype(BF16)
    row = lambda v: v.reshape(1, -1)
    mu = rwkv_mu[l]
    zl = jnp.zeros((LORA, G), F32)
    head_id = jnp.arange(RW_GROUP) // HEAD_DIM
    t_idx = jnp.arange(RW_CHUNK)
    scale = (QK_NOPE + QK_ROPE) ** -0.5 * LOG2E
    wq3 = mla_w_uq[l].reshape(Q_LORA, N_HEADS, QK_NOPE + QK_ROPE) * scale
    zq = lambda n: jnp.zeros((Q_LORA, N_HEADS, n), F32)
    wq = jnp.concatenate([wq3, zq(HEAD_PAD - QK_NOPE - QK_ROPE)], axis=2)
    wq_sw = jnp.concatenate([zq(QK_NOPE), _swap_halves(wq3[:, :, QK_NOPE:]),
                             zq(HEAD_PAD - QK_NOPE - QK_ROPE)], axis=2)
    wkv3 = mla_w_ukv[l].reshape(KV_LORA, N_HEADS, QK_NOPE + V_DIM)
    wk = jnp.concatenate([wkv3[:, :, :QK_NOPE], jnp.zeros((KV_LORA, N_HEADS, HEAD_PAD - QK_NOPE), F32)], axis=2)
    return {
        "ln_g": row(ln_g[l]), "w_in": w_re,
        "mu_rkv": row(mu[0:3 * G]),
        "mu_wa": row(mu[3 * G:3 * G + 2 * LORA]),
        "w0": row(rwkv_w0[l]), "w2p": jnp.concatenate([rwkv_w2[l], zl], axis=0).astype(BF16),
        "a0": row(rwkv_a0[l]), "a2p": jnp.concatenate([zl, rwkv_a2[l]], axis=0).astype(BF16),
        "k_k": row(rwkv_k_k[l]), "k_a": row(rwkv_k_a[l]), "r_k": row(rwkv_r_k[l]),
        "lnx_g": row(rwkv_lnx_g[l]), "lnx_b": row(rwkv_lnx_b[l]),
        "seg": (head_id[:, None] == head_id[None, :]).astype(BF16),
        "tri": (t_idx[None, :] <= t_idx[:, None]).astype(BF16),
        "conv_w": lru_conv_w[l], "conv_b": row(lru_conv_b[l]),
        "ga_w": _block_diag(lru_ga_w[l]).astype(BF16), "ga_b": row(lru_ga_b[l]),
        "gx_w": _block_diag(lru_gx_w[l]).astype(BF16), "gx_b": row(lru_gx_b[l]),
        "lam": row(lru_lam[l]), "lru_out_g": row(lru_out_g[l]),
        "q_norm_g": row(mla_q_norm_g[l]), "kv_norm_g": row(mla_kv_norm_g[l]),
        "wq": wq.reshape(Q_LORA, -1).astype(BF16), "wq_sw": wq_sw.reshape(Q_LORA, -1).astype(BF16),
        "wk": wk.reshape(KV_LORA, -1).astype(BF16),
        "wv_t": wkv3[:, :, QK_NOPE:].reshape(KV_LORA, -1).T.astype(BF16),
        "mla_out_g": row(mla_out_g[l]), "w_out": w_out[l].astype(BF16),
    }


def kernel(x, ln_g, w_in, rwkv_mu, rwkv_w0, rwkv_w2, rwkv_a0, rwkv_a2, rwkv_k_k, rwkv_k_a, rwkv_r_k,
           rwkv_lnx_g, rwkv_lnx_b, lru_conv_w, lru_conv_b, lru_ga_w, lru_ga_b, lru_gx_w, lru_gx_b, lru_lam,
           lru_out_g, mla_q_norm_g, mla_w_uq, mla_kv_norm_g, mla_w_ukv, mla_out_g, w_out, final_g):
    batch, seq, _ = x.shape
    assert batch % RW_STREAMS == 0 and seq % RW_CHUNK == 0, (batch, seq)
    tm = min(512, seq)
    rope = _rope_tables(seq)
    fg = final_g.reshape(1, -1)
    h = x.reshape(batch * seq, D_MODEL)
    for l in range(DEPTH):
        prm = _layer_params(l, ln_g, w_in, rwkv_mu, rwkv_w0, rwkv_w2, rwkv_a0, rwkv_a2, rwkv_k_k, rwkv_k_a,
                            rwkv_r_k, rwkv_lnx_g, rwkv_lnx_b, lru_conv_w, lru_conv_b, lru_ga_w, lru_ga_b,
                            lru_gx_w, lru_gx_b, lru_lam, lru_out_g, mla_q_norm_g, mla_w_uq, mla_kv_norm_g,
                            mla_w_ukv, mla_out_g, w_out)
        proj2d = _inproj(h, prm, tm, seq)
        proj = proj2d.reshape(batch, seq, D_PROJ)
        ya = _rwkv(proj, prm, batch, seq)
        yb = _lru(proj, prm, batch, seq, min(256, seq))
        blk = min(512, seq)
        q, k, vt = _mla_pro(proj, prm, rope, batch, seq, blk, blk // 2)
        oc = _attn(q, k, vt, batch, seq, blk, blk // 2)
        h = _outproj(ya.reshape(batch * seq, GROUP_W), yb.reshape(batch * seq, GROUP_W),
                     oc.reshape(batch * seq, GROUP_W), proj2d, h, prm["mla_out_g"], prm["w_out"], fg,
                     l == DEPTH - 1, tm)
    return h.reshape(batch, seq, D_MODEL)
```

```python
import functools

import jax
import jax.numpy as jnp
from jax import lax
from jax.experimental import pallas as pl
from jax.experimental.pallas import tpu as pltpu

F32 = jnp.float32
BF16 = jnp.bfloat16

D_MODEL = 1024
DEPTH = 4
GROUP_W = 512
HEAD_DIM = 64
N_HEADS = 8
NORM_EPS = 1e-6
LORA = 64
GN_EPS = 64e-5
CONV_W = 4
LRU_C = 8.0
Q_LORA = 256
KV_LORA = 128
QK_NOPE = 64
QK_ROPE = 32
V_DIM = 64
ROPE_BASE = 10000.0

LANE = 128
SUBLANES = 8
HEAD_PAD = 128
RW_GROUP = 256
RW_CHUNK = 128
RW_STREAMS = 4
BF16_ROWS = 16
ATT_ONES_ROWS = BF16_ROWS
ATT_ACC_ROWS = V_DIM + ATT_ONES_ROWS
ATT_AHEAD = 2
ATT_SUB = 2
LOG2E = 1.4426950408889634
COL_R, COL_K, COL_V, COL_ZA, COL_UB, COL_ZB, COL_ZC = (i * GROUP_W for i in range(7))
COL_QLAT = 7 * GROUP_W
COL_KVLAT = COL_QLAT + Q_LORA
COL_KR = COL_KVLAT + KV_LORA
COL_WA = COL_KR + LANE
D_PROJ = COL_WA + LANE

VMEM_LIMIT = 56 * 1024 * 1024


def _cparams(sem):
    return pltpu.CompilerParams(dimension_semantics=sem, vmem_limit_bytes=VMEM_LIMIT)


def _bdot(a, b):
    return jnp.dot(a.astype(BF16), b.astype(BF16), preferred_element_type=F32)


def _bdot_nt(a, b):
    return lax.dot_general(a.astype(BF16), b.astype(BF16), (((1,), (1,)), ((), ())),
                           preferred_element_type=F32)


def _split_dot(x, m, passes):
    acc = None
    rem = x
    for i in range(passes):
        part = rem.astype(BF16)
        term = jnp.dot(part, m, preferred_element_type=F32)
        acc = term if acc is None else acc + term
        if i + 1 < passes:
            rem = rem - part.astype(F32)
    return acc


def _sigmoid(x):
    return 0.5 * jnp.tanh(0.5 * x) + 0.5


def _silu(z):
    return z * _sigmoid(z)


def _shift_rows(x, tail, d):
    rolled = pltpu.roll(x, d, 0)
    row8 = lax.broadcasted_iota(jnp.int32, tail.shape, 0)
    first = jnp.where(row8 < d, pltpu.roll(tail, d, 0), rolled[0:SUBLANES, :])
    return jnp.concatenate([first, rolled[SUBLANES:, :]], axis=0)


def _inproj_kernel(x_ref, g_ref, w_ref, mu_ref, muwa_ref, cw_ref, cb_ref, o_ref, tail_ref, *, tiles_per_seq):
    i = pl.program_id(0)
    tm = x_ref.shape[0]
    x = x_ref[...]
    ms = jnp.mean(x * x, axis=-1, keepdims=True)
    xn = x * lax.rsqrt(ms + NORM_EPS) * g_ref[...]
    y = _bdot(xn, w_ref[...])
    tail = jnp.where(i % tiles_per_seq == 0, 0.0, tail_ref[...])
    tail_ref[...] = y[tm - SUBLANES:tm, :]

    def lerp(lo, width, mu):
        cur = y[:, lo:lo + width]
        return cur + mu * (_shift_rows(cur, tail[:, lo:lo + width], 1) - cur)

    o_ref[:, COL_R:COL_ZA] = lerp(COL_R, COL_ZA - COL_R, mu_ref[...])
    o_ref[:, COL_ZA:COL_UB] = y[:, COL_ZA:COL_UB]
    u = y[:, COL_UB:COL_ZB]
    u_tail = tail[:, COL_UB:COL_ZB]
    xc = cb_ref[...] + u * cw_ref[CONV_W - 1:CONV_W, :]
    for d in range(1, CONV_W):
        xc = xc + _shift_rows(u, u_tail, d) * cw_ref[CONV_W - 1 - d:CONV_W - d, :]
    o_ref[:, COL_UB:COL_ZB] = xc
    o_ref[:, COL_ZB:COL_WA] = y[:, COL_ZB:COL_WA]
    o_ref[:, COL_WA:D_PROJ] = lerp(COL_WA, D_PROJ - COL_WA, muwa_ref[...])


def _inproj(h2d, prm, tm, seq):
    m = h2d.shape[0]

    def whole(shape):
        return pl.BlockSpec(shape, lambda i: tuple(0 for _ in shape))

    return pl.pallas_call(
        functools.partial(_inproj_kernel, tiles_per_seq=seq // tm),
        grid=(m // tm,),
        in_specs=[pl.BlockSpec((tm, D_MODEL), lambda i: (i, 0)),
                  whole((1, D_MODEL)), whole((D_MODEL, D_PROJ)),
                  whole((1, COL_ZA - COL_R)), whole((1, LANE)), whole((CONV_W, GROUP_W)), whole((1, GROUP_W))],
        out_specs=pl.BlockSpec((tm, D_PROJ), lambda i: (i, 0)),
        out_shape=jax.ShapeDtypeStruct((m, D_PROJ), F32),
        scratch_shapes=[pltpu.VMEM((SUBLANES, D_PROJ), F32)],
        compiler_params=_cparams(("arbitrary",)),
        name="inproj",
    )(h2d, prm["ln_g"], prm["w_in"], prm["mu_rkv"], prm["mu_wa"], prm["conv_w"], prm["conv_b"])


def _split_dot_left(m, x, passes):
    acc = None
    rem = x
    for i in range(passes):
        part = rem.astype(BF16)
        term = jnp.dot(m, part, preferred_element_type=F32)
        acc = term if acc is None else acc + term
        if i + 1 < passes:
            rem = rem - part.astype(F32)
    return acc


def _seg_sum(x, seg, passes):
    parts = [_split_dot(x[:, i:i + RW_GROUP], seg, passes) for i in range(0, x.shape[1], RW_GROUP)]
    return jnp.concatenate(parts, axis=1)


class _RwkvStream:
    def __init__(self, bi, refs, shared):
        self.bi = bi
        self.refs = refs
        self.sh = shared

    def load(self):
        f = self.refs
        self.r, self.k, self.v, self.wa = f["r"][self.bi], f["k"][self.bi], f["v"][self.bi], f["wa"][self.bi]

    def mm_lora(self):
        f = self.refs
        self.lw = _bdot(jnp.tanh(self.wa), f["w2"][...])
        self.la = _bdot(self.wa, f["a2"][...])

    def decay(self):
        f = self.refs
        log_w = -jax.nn.softplus(-(f["w0"][...] + self.lw)) - 0.5
        self.ld = -jnp.exp(log_w)
        self.a = _sigmoid(f["a0"][...] + self.la)
        self.kk = self.k * f["kk"][...]

    def mm_norm_cumsum(self):
        self.n2 = _seg_sum(self.kk * self.kk, self.sh["seg"], 1)
        self.g = _split_dot_left(self.sh["tri"], self.ld, 2)

    def build(self):
        f, C = self.refs, self.sh["C"]
        kk = self.kk / jnp.maximum(jnp.sqrt(self.n2), 1e-12)
        k = self.k * (1.0 + (self.a - 1.0) * f["ka"][...])
        bb = kk * self.a
        g, ld = self.g, self.ld
        self.g_last = g[C - 1:C, :]
        inv_g = jnp.exp(-g)
        to_end = jnp.exp(self.g_last - g)
        self.lhs = jnp.concatenate([self.r * jnp.exp(g), kk * jnp.exp(g - ld)], axis=0).astype(BF16)
        self.rhs = jnp.concatenate([k * inv_g, bb * inv_g], axis=0)
        self.kb_end = jnp.concatenate([k * to_end, -(bb * to_end)], axis=0)
        self.vb = self.v.astype(BF16)
        self.kmod = k

    @staticmethod
    def _grp(x, i):
        return x[:, i * RW_GROUP:(i + 1) * RW_GROUP]

    @staticmethod
    def _pair(x, i):
        return x[:, i * LANE:(i + 1) * LANE]

    def mm_state_in(self):
        st = self.refs["state"]
        p = [lax.dot_general(self._grp(self.lhs, i), st[self.bi, i].astype(BF16), (((1,), (1,)), ((), ())),
                             preferred_element_type=F32) for i in range(GROUP_W // RW_GROUP)]
        self.p = jnp.concatenate(p, axis=1)

    def mm_amat(self):
        lane2 = self.sh["lane2"]
        pair = self._pair

        def own2(hd, x):
            return jnp.where((lane2 < HEAD_DIM) == (hd % 2 == 0), x, 0.0)

        self.amat = [_bdot_nt(pair(self.lhs, hd // 2), own2(hd, pair(self.rhs, hd // 2)))
                     for hd in range(N_HEADS)]

    def mm_xinit(self):
        C, incl, strict, pair = self.sh["C"], self.sh["incl"], self.sh["strict"], self._pair
        self.a_y = [jnp.concatenate([jnp.where(incl, m[0:C, 0:C], 0.0), jnp.where(incl, -m[0:C, C:2 * C], 0.0)],
                                    axis=1).astype(BF16) for m in self.amat]
        self.xs = [pair(self.p, hd // 2)[C:2 * C, :]
                   + _bdot(jnp.where(strict, self.amat[hd][C:2 * C, 0:C], 0.0), pair(self.vb, hd // 2))
                   for hd in range(N_HEADS)]
        self.npow = [jnp.where(strict, -self.amat[hd][C:2 * C, C:2 * C], 0.0).astype(BF16)
                     for hd in range(N_HEADS)]
        self.span = 1

    def mm_level(self):
        C, span = self.sh["C"], self.span
        lo = span if span % BF16_ROWS == 0 else 0
        if 2 * span < C:
            outs = [_bdot(n[lo:, :], jnp.concatenate([n, x.astype(BF16)], axis=1))
                    for n, x in zip(self.npow, self.xs)]
            self.npow = [jnp.concatenate([jnp.zeros((lo, C), BF16), o[:, 0:C].astype(BF16)], axis=0) if lo
                         else o[:, 0:C].astype(BF16) for o in outs]
            upd = [o[:, C:] for o in outs]
        else:
            upd = [_bdot(n[lo:, :], x) for n, x in zip(self.npow, self.xs)]
        self.xs = [jnp.concatenate([x[0:lo, :], x[lo:, :] + d], axis=0) if lo else x + d
                   for x, d in zip(self.xs, upd)]
        self.span = 2 * span

    def mm_y(self):
        C, pair = self.sh["C"], self._pair
        y_h = [_bdot(self.a_y[hd], jnp.concatenate([pair(self.vb, hd // 2), self.xs[hd].astype(BF16)], axis=0))
               for hd in range(N_HEADS)]
        first = self.sh["lane1"] < HEAD_DIM
        n_pairs = GROUP_W // LANE
        y = jnp.concatenate([jnp.where(first, y_h[2 * i], y_h[2 * i + 1]) for i in range(n_pairs)], axis=1)
        self.u = jnp.concatenate([jnp.where(first, self.xs[2 * i], self.xs[2 * i + 1]) for i in range(n_pairs)],
                                 axis=1)
        self.y = y + self.p[0:C, :]

    def mm_state_out(self):
        st, seg = self.refs["state"], self.sh["seg"]
        vu_t = jnp.concatenate([self.v, self.u], axis=0).T
        decay_all = jnp.exp(self.g_last)
        for i in range(GROUP_W // RW_GROUP):
            s_new = (st[self.bi, i] * self._grp(decay_all, i)
                     + _bdot(vu_t[i * RW_GROUP:(i + 1) * RW_GROUP, :], self._grp(self.kb_end, i)))
            st[self.bi, i] = jnp.where(seg > 0.5, s_new, 0.0)

    def mm_mean(self):
        self.mean = _seg_sum(self.y, self.sh["seg"], 1) * (1.0 / HEAD_DIM)

    def mm_var_bonus(self):
        f, seg = self.refs, self.sh["seg"]
        self.yc = self.y - self.mean
        self.var = _seg_sum(self.yc * self.yc, seg, 1) * (1.0 / HEAD_DIM)
        self.bonus = _seg_sum(self.r * self.kmod * f["rk"][...], seg, 1) * self.v

    def finish(self):
        f = self.refs
        yn = self.yc * lax.rsqrt(self.var + GN_EPS) * f["lng"][...] + f["lnb"][...]
        f["o"][self.bi] = ((yn + self.bonus) * _silu(f["z"][self.bi])).astype(f["o"].dtype)


def _rwkv_kernel(r_ref, k_ref, v_ref, wa_ref, z_ref,
                 w0_ref, w2_ref, a0_ref, a2_ref, kk_ref, ka_ref, rk_ref, lng_ref, lnb_ref,
                 seg_ref, tri_ref,
                 o_ref,
                 state_ref):
    c = pl.program_id(1)
    C = r_ref.shape[1]

    @pl.when(c == 0)
    def _():
        state_ref[...] = jnp.zeros_like(state_ref)

    ti = lax.broadcasted_iota(jnp.int32, (C, C), 0)
    si = lax.broadcasted_iota(jnp.int32, (C, C), 1)
    shared = {"C": C, "seg": seg_ref[...], "tri": tri_ref[...], "incl": si <= ti, "strict": si < ti,
              "lane2": lax.broadcasted_iota(jnp.int32, (2 * C, LANE), 1),
              "lane1": lax.broadcasted_iota(jnp.int32, (C, LANE), 1)}
    refs = {"r": r_ref, "k": k_ref, "v": v_ref, "wa": wa_ref, "z": z_ref, "o": o_ref,
            "w0": w0_ref, "w2": w2_ref, "a0": a0_ref, "a2": a2_ref, "kk": kk_ref, "ka": ka_ref, "rk": rk_ref,
            "lng": lng_ref, "lnb": lnb_ref, "state": state_ref}
    streams = [_RwkvStream(bi, refs, shared) for bi in range(r_ref.shape[0])]
    n_levels = C.bit_length() - 1

    first = streams[0]
    first.load()
    first.mm_lora()
    first.decay()
    first.mm_norm_cumsum()
    first.build()
    for i, cur in enumerate(streams):
        prv = streams[i - 1] if i > 0 else None
        nxt = streams[i + 1] if i + 1 < len(streams) else None
        cur.mm_state_in()
        cur.mm_amat()
        if prv is not None:
            prv.mm_mean()
        if nxt is not None:
            nxt.load()
            nxt.mm_lora()
            nxt.decay()
        cur.mm_xinit()
        if prv is not None:
            prv.mm_var_bonus()
            prv.finish()
        if nxt is not None:
            nxt.mm_norm_cumsum()
            nxt.build()
        for _ in range(n_levels):
            cur.mm_level()
        cur.mm_y()
        cur.mm_state_out()
    last = streams[-1]
    last.mm_mean()
    last.mm_var_bonus()
    last.finish()


def _rwkv(proj, prm, batch, seq):
    C = RW_CHUNK
    nb = RW_STREAMS

    def col(base):
        return pl.BlockSpec((nb, C, GROUP_W), lambda b, c, base=base: (b, c, base // GROUP_W))

    def vec():
        return pl.BlockSpec((1, GROUP_W), lambda b, c: (0, 0))

    def whole(shape):
        return pl.BlockSpec(shape, lambda b, c: tuple(0 for _ in shape))

    in_specs = [col(COL_R), col(COL_K), col(COL_V),
                pl.BlockSpec((nb, C, LANE), lambda b, c: (b, c, COL_WA // LANE)),
                col(COL_ZA),
                vec(), whole((LANE, GROUP_W)), vec(), whole((LANE, GROUP_W)),
                vec(), vec(), vec(), vec(), vec(),
                whole((RW_GROUP, RW_GROUP)), whole((C, C))]
    return pl.pallas_call(
        _rwkv_kernel,
        grid=(batch // nb, seq // C),
        in_specs=in_specs,
        out_specs=pl.BlockSpec((nb, C, GROUP_W), lambda b, c: (b, c, 0)),
        out_shape=jax.ShapeDtypeStruct((batch, seq, GROUP_W), BF16),
        scratch_shapes=[pltpu.VMEM((nb, GROUP_W // RW_GROUP, RW_GROUP, RW_GROUP), F32)],
        compiler_params=_cparams(("parallel", "arbitrary")),
        name="rwkv7",
    )(proj, proj, proj, proj, proj,
      prm["w0"], prm["w2p"], prm["a0"], prm["a2p"], prm["k_k"], prm["k_a"], prm["r_k"],
      prm["lnx_g"], prm["lnx_b"], prm["seg"], prm["tri"])


def _lru_kernel(xc_ref, z_ref, gaw_ref, gab_ref, gxw_ref, gxb_ref, lam_ref, og_ref, o_ref, hprev_ref):
    c = pl.program_id(1)
    ct = xc_ref.shape[0]

    @pl.when(c == 0)
    def _():
        hprev_ref[...] = jnp.zeros_like(hprev_ref)

    xc = xc_ref[...]
    gate_r = _sigmoid(_bdot(xc, gaw_ref[...]) + gab_ref[...])
    gate_i = _sigmoid(_bdot(xc, gxw_ref[...]) + gxb_ref[...])
    log_a = -LRU_C * gate_r * jax.nn.softplus(-lam_ref[...])
    acc_a = jnp.exp(log_a)
    th = jnp.tanh(log_a)
    acc_b = jnp.sqrt(-2.0 * th / (1.0 - th)) * (gate_i * xc)
    n_groups = ct // SUBLANES
    acc_a = acc_a.reshape(n_groups, SUBLANES, GROUP_W)
    acc_b = acc_b.reshape(n_groups, SUBLANES, GROUP_W)
    sub = lax.broadcasted_iota(jnp.int32, acc_a.shape, 1)
    d = 1
    while d < SUBLANES:
        keep = sub >= d
        acc_b = acc_a * jnp.where(keep, pltpu.roll(acc_b, d, 1), 0.0) + acc_b
        acc_a = acc_a * jnp.where(keep, pltpu.roll(acc_a, d, 1), 1.0)
        d *= 2
    carry = hprev_ref[...]
    groups = []
    for i in range(n_groups):
        h_i = acc_a[i] * carry + acc_b[i]
        groups.append(h_i)
        carry = h_i[SUBLANES - 1:SUBLANES, :]
    hs = jnp.concatenate(groups, axis=0)
    hprev_ref[...] = hs[ct - 1:ct, :]
    ms = jnp.mean(hs * hs, axis=-1, keepdims=True)
    yb = hs * lax.rsqrt(ms + NORM_EPS) * og_ref[...]
    o_ref[...] = (yb * _silu(z_ref[...])).astype(o_ref.dtype)


def _lru(proj, prm, batch, seq, ct):
    def vec():
        return pl.BlockSpec((1, GROUP_W), lambda b, c: (0, 0))

    def mat(rows):
        return pl.BlockSpec((rows, GROUP_W), lambda b, c: (0, 0))

    return pl.pallas_call(
        _lru_kernel,
        grid=(batch, seq // ct),
        in_specs=[pl.BlockSpec((None, ct, GROUP_W), lambda b, c: (b, c, COL_UB // GROUP_W)),
                  pl.BlockSpec((None, ct, GROUP_W), lambda b, c: (b, c, COL_ZB // GROUP_W)),
                  mat(GROUP_W), vec(), mat(GROUP_W), vec(), vec(), vec()],
        out_specs=pl.BlockSpec((None, ct, GROUP_W), lambda b, c: (b, c, 0)),
        out_shape=jax.ShapeDtypeStruct((batch, seq, GROUP_W), BF16),
        scratch_shapes=[pltpu.VMEM((1, GROUP_W), F32)],
        compiler_params=_cparams(("parallel", "arbitrary")),
        name="rglru",
    )(proj, proj, prm["ga_w"], prm["ga_b"], prm["gx_w"], prm["gx_b"],
      prm["lam"], prm["lru_out_g"])


def _mla_pro_kernel(ql_ref, kvl_ref, kr_ref, qg_ref, kvg_ref, wq_ref, wqs_ref, wk_ref, wv_ref,
                    cq_ref, sq_ref, ck_ref, sk_ref, q_ref, k_ref, v_ref):
    def heads(tab_ref):
        return jnp.concatenate([tab_ref[...]] * N_HEADS, axis=1)

    ql = ql_ref[...]
    qn = ql * lax.rsqrt(jnp.mean(ql * ql, axis=-1, keepdims=True) + NORM_EPS) * qg_ref[...]
    qn = qn.astype(BF16)
    q = jnp.dot(qn, wq_ref[...], preferred_element_type=F32) * heads(cq_ref)
    q = q + jnp.dot(qn, wqs_ref[...], preferred_element_type=F32) * heads(sq_ref)
    q_ref[...] = q.astype(q_ref.dtype)

    kvl = kvl_ref[...]
    kvn = kvl * lax.rsqrt(jnp.mean(kvl * kvl, axis=-1, keepdims=True) + NORM_EPS) * kvg_ref[...]
    kvn = kvn.astype(BF16)
    kr = kr_ref[...]
    roped = kr * ck_ref[...] + pltpu.roll(kr * sk_ref[...], LANE - QK_ROPE, 1)
    placed = pltpu.roll(roped, QK_NOPE, 1)
    kfull = jnp.dot(kvn, wk_ref[...], preferred_element_type=F32) + jnp.concatenate([placed] * N_HEADS, axis=1)
    k_ref[...] = kfull.astype(k_ref.dtype)
    vt = lax.dot_general(wv_ref[...], kvn, (((1,), (1,)), ((), ())), preferred_element_type=F32)
    kblk = v_ref.shape[-1]
    for i in range(v_ref.shape[0]):
        v_ref[i] = vt[:, i * kblk:(i + 1) * kblk].astype(v_ref.dtype)


def _mla_pro(proj, prm, rope, batch, seq, tm, kblk):
    qk_w = N_HEADS * HEAD_PAD

    def whole(shape):
        return pl.BlockSpec(shape, lambda b, t: tuple(0 for _ in shape))

    def tab():
        return pl.BlockSpec((tm, LANE), lambda b, t: (t, 0))

    return pl.pallas_call(
        _mla_pro_kernel,
        grid=(batch, seq // tm),
        in_specs=[pl.BlockSpec((None, tm, Q_LORA), lambda b, t: (b, t, COL_QLAT // Q_LORA)),
                  pl.BlockSpec((None, tm, KV_LORA), lambda b, t: (b, t, COL_KVLAT // KV_LORA)),
                  pl.BlockSpec((None, tm, LANE), lambda b, t: (b, t, COL_KR // LANE)),
                  whole((1, Q_LORA)), whole((1, KV_LORA)),
                  whole((Q_LORA, qk_w)), whole((Q_LORA, qk_w)), whole((KV_LORA, qk_w)), whole((GROUP_W, KV_LORA)),
                  tab(), tab(), tab(), tab()],
        out_specs=[pl.BlockSpec((None, tm, qk_w), lambda b, t: (b, t, 0)),
                   pl.BlockSpec((None, tm, qk_w), lambda b, t: (b, t, 0)),
                   pl.BlockSpec((None, tm // kblk, GROUP_W, kblk), lambda b, t: (b, t, 0, 0))],
        out_shape=[jax.ShapeDtypeStruct((batch, seq, qk_w), BF16),
                   jax.ShapeDtypeStruct((batch, seq, qk_w), BF16),
                   jax.ShapeDtypeStruct((batch, seq // kblk, GROUP_W, kblk), BF16)],
        compiler_params=_cparams(("parallel", "parallel")),
        name="mla_pro",
    )(proj, proj, proj, prm["q_norm_g"], prm["kv_norm_g"], prm["wq"], prm["wq_sw"], prm["wk"], prm["wv_t"],
      rope["cq"], rope["sq"], rope["ck"], rope["sk"])


def _attn_kernel(q_ref, k_ref, vt_ref, o_ref, m_ref, acc_ref, s_ref, mx_ref, *, blk, kblk, n_steps):
    step = pl.program_id(2)
    ones = jnp.ones((ATT_ONES_ROWS, kblk), BF16)
    key = lax.broadcasted_iota(jnp.int32, (kblk, blk), 0)
    qry = lax.broadcasted_iota(jnp.int32, (kblk, blk), 1)
    n_buf = s_ref.shape[1]

    def stages(slot, i):
        n_keys = 2 * i + 2
        row0 = slot * blk

        def init():
            m_ref[slot] = jnp.full(m_ref.shape[1:], -jnp.inf, F32)
            acc_ref[slot] = jnp.zeros(acc_ref.shape[1:], F32)

        def scores(t):
            buf = t % n_buf
            for hh in range(2):
                lanes = slice(hh * HEAD_PAD, (hh + 1) * HEAD_PAD)
                s = lax.dot_general(k_ref[t * kblk:(t + 1) * kblk, lanes], q_ref[row0:row0 + blk, lanes],
                                    (((1,), (1,)), ((), ())), preferred_element_type=F32)
                s_ref[slot, buf, hh] = s
                if t < 2 * i:
                    mx_ref[slot, buf, hh:hh + 1, :] = jnp.max(s, axis=0, keepdims=True)

        def consume(t):
            buf = t % n_buf
            for hh in range(2):
                s = s_ref[slot, buf, hh]
                if t < 2 * i:
                    mx = mx_ref[slot, buf, hh:hh + 1, :]
                else:
                    s = jnp.where(key + (t - 2 * i) * kblk <= qry, s, -jnp.inf)
                    mx = jnp.max(s, axis=0, keepdims=True)
                m_prev = m_ref[slot, hh:hh + 1, :]
                m_new = jnp.maximum(m_prev, mx)
                alpha = jnp.exp2(m_prev - m_new)
                p = jnp.exp2(s - m_new).astype(BF16)
                m_ref[slot, hh:hh + 1, :] = m_new
                rows = slice(hh * ATT_ACC_ROWS, (hh + 1) * ATT_ACC_ROWS)
                vt_ext = jnp.concatenate([vt_ref[t, hh * V_DIM:(hh + 1) * V_DIM, :], ones], axis=0)
                acc_ref[slot, rows, :] = (alpha * acc_ref[slot, rows, :]
                                          + jnp.dot(vt_ext, p, preferred_element_type=F32))

        def finalize():
            outs = []
            for hh in range(2):
                base = hh * ATT_ACC_ROWS
                outs.append(acc_ref[slot, base:base + V_DIM, :] / acc_ref[slot, base + V_DIM:base + V_DIM + 1, :])
            o_ref[row0:row0 + blk, :] = jnp.concatenate(outs, axis=0).T

        ops = [init] + [functools.partial(scores, t) for t in range(min(ATT_AHEAD, n_keys))]
        for t in range(n_keys):
            if t + ATT_AHEAD < n_keys:
                ops.append(functools.partial(scores, t + ATT_AHEAD))
            ops.append(functools.partial(consume, t))
        return ops + [finalize]

    def run_step(j):
        queues = [stages(slot, ATT_SUB * j + slot) for slot in range(ATT_SUB)]
        for pos in range(max(len(ops) for ops in queues)):
            for ops in queues:
                if pos < len(ops):
                    ops[pos]()

    for j in range(n_steps):
        pl.when(step == j)(functools.partial(run_step, j))


def _attn(q, k, vt, batch, seq, blk, kblk):
    pair_w = 2 * HEAD_PAD
    n_buf = ATT_AHEAD + 1
    rows = ATT_SUB * blk
    return pl.pallas_call(
        functools.partial(_attn_kernel, blk=blk, kblk=kblk, n_steps=seq // rows),
        grid=(batch, N_HEADS // 2, seq // rows),
        in_specs=[pl.BlockSpec((None, rows, pair_w), lambda b, h, i: (b, i, h)),
                  pl.BlockSpec((None, seq, pair_w), lambda b, h, i: (b, 0, h)),
                  pl.BlockSpec((None, seq // kblk, LANE, kblk), lambda b, h, i: (b, 0, h, 0))],
        out_specs=pl.BlockSpec((None, rows, LANE), lambda b, h, i: (b, i, h)),
        out_shape=jax.ShapeDtypeStruct((batch, seq, GROUP_W), F32),
        scratch_shapes=[pltpu.VMEM((ATT_SUB, SUBLANES, blk), F32), pltpu.VMEM((ATT_SUB, 2 * ATT_ACC_ROWS, blk), F32),
                        pltpu.VMEM((ATT_SUB, n_buf, 2, kblk, blk), F32),
                        pltpu.VMEM((ATT_SUB, n_buf, SUBLANES, blk), F32)],
        compiler_params=_cparams(("parallel", "parallel", "arbitrary")),
        name="mla_attn",
    )(q, k, vt)


def _outproj_kernel(ya_ref, yb_ref, oc_ref, zc_ref, gc_ref, w_ref, h_ref, fg_ref, o_ref, *, final):
    oc = oc_ref[...]
    yc = oc * lax.rsqrt(jnp.mean(oc * oc, axis=-1, keepdims=True) + NORM_EPS) * gc_ref[...]
    yc = yc * _silu(zc_ref[...])
    acc = h_ref[...]
    acc = acc + jnp.dot(ya_ref[...], w_ref[0:GROUP_W, :], preferred_element_type=F32)
    acc = acc + jnp.dot(yb_ref[...], w_ref[GROUP_W:2 * GROUP_W, :], preferred_element_type=F32)
    acc = acc + _bdot(yc, w_ref[2 * GROUP_W:3 * GROUP_W, :])
    if final:
        acc = acc * lax.rsqrt(jnp.mean(acc * acc, axis=-1, keepdims=True) + NORM_EPS) * fg_ref[...]
    o_ref[...] = acc


def _outproj(ya, yb, oc, proj, h2d, mla_out_g, w_out, final_g, final, tm):
    m = h2d.shape[0]

    def rows(width, blk=0):
        return pl.BlockSpec((tm, width), lambda i, blk=blk: (i, blk))

    return pl.pallas_call(
        functools.partial(_outproj_kernel, final=final),
        grid=(m // tm,),
        in_specs=[rows(GROUP_W), rows(GROUP_W), rows(GROUP_W), rows(GROUP_W, COL_ZC // GROUP_W),
                  pl.BlockSpec((1, GROUP_W), lambda i: (0, 0)),
                  pl.BlockSpec((3 * GROUP_W, D_MODEL), lambda i: (0, 0)),
                  rows(D_MODEL),
                  pl.BlockSpec((1, D_MODEL), lambda i: (0, 0))],
        out_specs=rows(D_MODEL),
        out_shape=jax.ShapeDtypeStruct((m, D_MODEL), F32),
        compiler_params=_cparams(("parallel",)),
        name="outproj_final" if final else "outproj",
    )(ya, yb, oc, proj, mla_out_g, w_out, h2d, final_g)


def _rope_tables(seq):
    half = QK_ROPE // 2
    inv_freq = ROPE_BASE ** (-jnp.arange(half, dtype=F32) * 2.0 / QK_ROPE)
    ang = jnp.arange(seq, dtype=F32)[:, None] * inv_freq[None, :]
    cos2 = jnp.concatenate([jnp.cos(ang)] * 2, axis=1)
    sin2 = jnp.concatenate([jnp.sin(ang)] * 2, axis=1)
    zeros = lambda w: jnp.zeros((seq, w), F32)
    ones = jnp.ones((seq, QK_NOPE), F32)
    return {
        "cq": jnp.concatenate([ones, cos2, zeros(HEAD_PAD - QK_NOPE - QK_ROPE)], axis=1),
        "sq": jnp.concatenate([zeros(QK_NOPE), sin2, zeros(HEAD_PAD - QK_NOPE - QK_ROPE)], axis=1),
        "ck": jnp.concatenate([cos2, zeros(LANE - QK_ROPE)], axis=1),
        "sk": jnp.concatenate([zeros(QK_ROPE), sin2, zeros(LANE - 2 * QK_ROPE)], axis=1),
    }


def _swap_halves(w):
    half = w.shape[-1] // 2
    return jnp.concatenate([-w[..., half:], w[..., :half]], axis=-1)


def _block_diag(w):
    h, n, _ = w.shape
    eye = jnp.eye(h, dtype=w.dtype)
    return (eye[:, None, :, None] * w[:, :, None, :]).reshape(h * n, h * n)


def _layer_params(l, ln_g, w_in, rwkv_mu, rwkv_w0, rwkv_w2, rwkv_a0, rwkv_a2, rwkv_k_k, rwkv_k_a, rwkv_r_k,
                  rwkv_lnx_g, rwkv_lnx_b, lru_conv_w, lru_conv_b, lru_ga_w, lru_ga_b, lru_gx_w, lru_gx_b,
                  lru_lam, lru_out_g, mla_q_norm_g, mla_w_uq, mla_kv_norm_g, mla_w_ukv, mla_out_g, w_out):
    G = GROUP_W
    w = w_in[l]
    o_ua, o_za = 0, 3 * G + 2 * LORA
    o_ub = o_za + G
    o_zb = o_ub + G
    o_ql = o_zb + G
    o_kv = o_ql + Q_LORA
    o_kr = o_kv + KV_LORA
    o_zc = o_kr + QK_ROPE
    sl = lambda a, n: w[:, a:a + n]
    kr_w = sl(o_kr, QK_ROPE)
    w_re = jnp.concatenate([
        sl(o_ua, G), sl(o_ua + G, G), sl(o_ua + 2 * G, G), sl(o_za, G), sl(o_ub, G), sl(o_zb, G), sl(o_zc, G),
        sl(o_ql, Q_LORA), sl(o_kv, KV_LORA),
        kr_w, _swap_halves(kr_w), jnp.zeros((D_MODEL, LANE - 2 * QK_ROPE), F32),
        sl(o_ua + 3 * G, 2 * LORA)], axis=1).astype(BF16)
    row = lambda v: v.reshape(1, -1)
    mu = rwkv_mu[l]
    zl = jnp.zeros((LORA, G), F32)
    head_id = jnp.arange(RW_GROUP) // HEAD_DIM
    t_idx = jnp.arange(RW_CHUNK)
    scale = (QK_NOPE + QK_ROPE) ** -0.5 * LOG2E
    wq3 = mla_w_uq[l].reshape(Q_LORA, N_HEADS, QK_NOPE + QK_ROPE) * scale
    zq = lambda n: jnp.zeros((Q_LORA, N_HEADS, n), F32)
    wq = jnp.concatenate([wq3, zq(HEAD_PAD - QK_NOPE - QK_ROPE)], axis=2)
    wq_sw = jnp.concatenate([zq(QK_NOPE), _swap_halves(wq3[:, :, QK_NOPE:]),
                             zq(HEAD_PAD - QK_NOPE - QK_ROPE)], axis=2)
    wkv3 = mla_w_ukv[l].reshape(KV_LORA, N_HEADS, QK_NOPE + V_DIM)
    wk = jnp.concatenate([wkv3[:, :, :QK_NOPE], jnp.zeros((KV_LORA, N_HEADS, HEAD_PAD - QK_NOPE), F32)], axis=2)
    return {
        "ln_g": row(ln_g[l]), "w_in": w_re,
        "mu_rkv": row(mu[0:3 * G]),
        "mu_wa": row(mu[3 * G:3 * G + 2 * LORA]),
        "w0": row(rwkv_w0[l]), "w2p": jnp.concatenate([rwkv_w2[l], zl], axis=0).astype(BF16),
        "a0": row(rwkv_a0[l]), "a2p": jnp.concatenate([zl, rwkv_a2[l]], axis=0).astype(BF16),
        "k_k": row(rwkv_k_k[l]), "k_a": row(rwkv_k_a[l]), "r_k": row(rwkv_r_k[l]),
        "lnx_g": row(rwkv_lnx_g[l]), "lnx_b": row(rwkv_lnx_b[l]),
        "seg": (head_id[:, None] == head_id[None, :]).astype(BF16),
        "tri": (t_idx[None, :] <= t_idx[:, None]).astype(BF16),
        "conv_w": lru_conv_w[l], "conv_b": row(lru_conv_b[l]),
        "ga_w": _block_diag(lru_ga_w[l]).astype(BF16), "ga_b": row(lru_ga_b[l]),
        "gx_w": _block_diag(lru_gx_w[l]).astype(BF16), "gx_b": row(lru_gx_b[l]),
        "lam": row(lru_lam[l]), "lru_out_g": row(lru_out_g[l]),
        "q_norm_g": row(mla_q_norm_g[l]), "kv_norm_g": row(mla_kv_norm_g[l]),
        "wq": wq.reshape(Q_LORA, -1).astype(BF16), "wq_sw": wq_sw.reshape(Q_LORA, -1).astype(BF16),
        "wk": wk.reshape(KV_LORA, -1).astype(BF16),
        "wv_t": wkv3[:, :, QK_NOPE:].reshape(KV_LORA, -1).T.astype(BF16),
        "mla_out_g": row(mla_out_g[l]), "w_out": w_out[l].astype(BF16),
    }


def kernel(x, ln_g, w_in, rwkv_mu, rwkv_w0, rwkv_w2, rwkv_a0, rwkv_a2, rwkv_k_k, rwkv_k_a, rwkv_r_k,
           rwkv_lnx_g, rwkv_lnx_b, lru_conv_w, lru_conv_b, lru_ga_w, lru_ga_b, lru_gx_w, lru_gx_b, lru_lam,
           lru_out_g, mla_q_norm_g, mla_w_uq, mla_kv_norm_g, mla_w_ukv, mla_out_g, w_out, final_g):
    batch, seq, _ = x.shape
    assert batch % RW_STREAMS == 0 and seq % RW_CHUNK == 0, (batch, seq)
    tm = min(512, seq)
    rope = _rope_tables(seq)
    fg = final_g.reshape(1, -1)
    h = x.reshape(batch * seq, D_MODEL)
    for l in range(DEPTH):
        prm = _layer_params(l, ln_g, w_in, rwkv_mu, rwkv_w0, rwkv_w2, rwkv_a0, rwkv_a2, rwkv_k_k, rwkv_k_a,
                            rwkv_r_k, rwkv_lnx_g, rwkv_lnx_b, lru_conv_w, lru_conv_b, lru_ga_w, lru_ga_b,
                            lru_gx_w, lru_gx_b, lru_lam, lru_out_g, mla_q_norm_g, mla_w_uq, mla_kv_norm_g,
                            mla_w_ukv, mla_out_g, w_out)
        proj2d = _inproj(h, prm, tm, seq)
        proj = proj2d.reshape(batch, seq, D_PROJ)
        ya = _rwkv(proj, prm, batch, seq)
        yb = _lru(proj, prm, batch, seq, min(256, seq))
        blk = min(512, seq // ATT_SUB)
        q, k, vt = _mla_pro(proj, prm, rope, batch, seq, blk, blk // 2)
        oc = _attn(q, k, vt, batch, seq, blk, blk // 2)
        h = _outproj(ya.reshape(batch * seq, GROUP_W), yb.reshape(batch * seq, GROUP_W),
                     oc.reshape(batch * seq, GROUP_W), proj2d, h, prm["mla_out_g"], prm["w_out"], fg,
                     l == DEPTH - 1, tm)
    return h.reshape(batch, seq, D_MODEL)
```

```python
import functools

import jax
import jax.numpy as jnp
from jax import lax
from jax.experimental import pallas as pl
from jax.experimental.pallas import tpu as pltpu

F32 = jnp.float32
BF16 = jnp.bfloat16

D_MODEL = 1024
DEPTH = 4
GROUP_W = 512
HEAD_DIM = 64
N_HEADS = 8
NORM_EPS = 1e-6
LORA = 64
GN_EPS = 64e-5
CONV_W = 4
LRU_C = 8.0
Q_LORA = 256
KV_LORA = 128
QK_NOPE = 64
QK_ROPE = 32
V_DIM = 64
ROPE_BASE = 10000.0

LANE = 128
SUBLANES = 8
HEAD_PAD = 128
ROW_TILE = 512
OUT_ROW_TILE = 1024
LRU_CHUNK = 256
ATT_Q_BLOCK = 512
RW_GROUP = 256
RW_CHUNK = 128
RW_STREAMS = 4
BF16_ROWS = 16
ATT_ONES_ROWS = BF16_ROWS
ATT_ACC_ROWS = V_DIM + ATT_ONES_ROWS
ATT_AHEAD = 2
ATT_SUB = 2
LOG2E = 1.4426950408889634
COL_R, COL_K, COL_V, COL_ZA, COL_UB, COL_ZB, COL_ZC = (i * GROUP_W for i in range(7))
COL_QLAT = 7 * GROUP_W
COL_KVLAT = COL_QLAT + Q_LORA
COL_KR = COL_KVLAT + KV_LORA
COL_WA = COL_KR + LANE
D_PROJ = COL_WA + LANE

VMEM_LIMIT = 56 * 1024 * 1024


def _cparams(sem):
    return pltpu.CompilerParams(dimension_semantics=sem, vmem_limit_bytes=VMEM_LIMIT)


def _bdot(a, b):
    return jnp.dot(a.astype(BF16), b.astype(BF16), preferred_element_type=F32)


def _bdot_nt(a, b):
    return lax.dot_general(a.astype(BF16), b.astype(BF16), (((1,), (1,)), ((), ())),
                           preferred_element_type=F32)


def _split_dot(x, m, passes):
    acc = None
    rem = x
    for i in range(passes):
        part = rem.astype(BF16)
        term = jnp.dot(part, m, preferred_element_type=F32)
        acc = term if acc is None else acc + term
        if i + 1 < passes:
            rem = rem - part.astype(F32)
    return acc


def _sigmoid(x):
    return 0.5 * jnp.tanh(0.5 * x) + 0.5


def _silu(z):
    return z * _sigmoid(z)


def _shift_rows(x, tail, d):
    rolled = pltpu.roll(x, d, 0)
    row8 = lax.broadcasted_iota(jnp.int32, tail.shape, 0)
    first = jnp.where(row8 < d, pltpu.roll(tail, d, 0), rolled[0:SUBLANES, :])
    return jnp.concatenate([first, rolled[SUBLANES:, :]], axis=0)


def _inproj_kernel(x_ref, g_ref, w_ref, mu_ref, muwa_ref, cw_ref, cb_ref, o_ref, tail_ref, *, tiles_per_seq):
    i = pl.program_id(0)
    tm = x_ref.shape[0]
    x = x_ref[...]
    ms = jnp.mean(x * x, axis=-1, keepdims=True)
    xn = x * lax.rsqrt(ms + NORM_EPS) * g_ref[...]
    y = _bdot(xn, w_ref[...])
    tail = jnp.where(i % tiles_per_seq == 0, 0.0, tail_ref[...])
    tail_ref[...] = y[tm - SUBLANES:tm, :]

    def lerp(lo, width, mu):
        cur = y[:, lo:lo + width]
        return cur + mu * (_shift_rows(cur, tail[:, lo:lo + width], 1) - cur)

    o_ref[:, COL_R:COL_ZA] = lerp(COL_R, COL_ZA - COL_R, mu_ref[...])
    o_ref[:, COL_ZA:COL_UB] = y[:, COL_ZA:COL_UB]
    u = y[:, COL_UB:COL_ZB]
    u_tail = tail[:, COL_UB:COL_ZB]
    xc = cb_ref[...] + u * cw_ref[CONV_W - 1:CONV_W, :]
    for d in range(1, CONV_W):
        xc = xc + _shift_rows(u, u_tail, d) * cw_ref[CONV_W - 1 - d:CONV_W - d, :]
    o_ref[:, COL_UB:COL_ZB] = xc
    o_ref[:, COL_ZB:COL_WA] = y[:, COL_ZB:COL_WA]
    o_ref[:, COL_WA:D_PROJ] = lerp(COL_WA, D_PROJ - COL_WA, muwa_ref[...])


def _inproj(h2d, prm, tm, seq):
    m = h2d.shape[0]

    def whole(shape):
        return pl.BlockSpec(shape, lambda i: tuple(0 for _ in shape))

    return pl.pallas_call(
        functools.partial(_inproj_kernel, tiles_per_seq=seq // tm),
        grid=(m // tm,),
        in_specs=[pl.BlockSpec((tm, D_MODEL), lambda i: (i, 0)),
                  whole((1, D_MODEL)), whole((D_MODEL, D_PROJ)),
                  whole((1, COL_ZA - COL_R)), whole((1, LANE)), whole((CONV_W, GROUP_W)), whole((1, GROUP_W))],
        out_specs=pl.BlockSpec((tm, D_PROJ), lambda i: (i, 0)),
        out_shape=jax.ShapeDtypeStruct((m, D_PROJ), F32),
        scratch_shapes=[pltpu.VMEM((SUBLANES, D_PROJ), F32)],
        compiler_params=_cparams(("arbitrary",)),
        name="inproj",
    )(h2d, prm["ln_g"], prm["w_in"], prm["mu_rkv"], prm["mu_wa"], prm["conv_w"], prm["conv_b"])


def _split_dot_left(m, x, passes):
    acc = None
    rem = x
    for i in range(passes):
        part = rem.astype(BF16)
        term = jnp.dot(m, part, preferred_element_type=F32)
        acc = term if acc is None else acc + term
        if i + 1 < passes:
            rem = rem - part.astype(F32)
    return acc


def _seg_sum(x, seg, passes):
    parts = [_split_dot(x[:, i:i + RW_GROUP], seg, passes) for i in range(0, x.shape[1], RW_GROUP)]
    return jnp.concatenate(parts, axis=1)


class _RwkvStream:
    def __init__(self, bi, refs, shared):
        self.bi = bi
        self.refs = refs
        self.sh = shared

    def load(self):
        f = self.refs
        self.r, self.k, self.v, self.wa = f["r"][self.bi], f["k"][self.bi], f["v"][self.bi], f["wa"][self.bi]

    def mm_lora(self):
        f = self.refs
        self.lw = _bdot(jnp.tanh(self.wa), f["w2"][...])
        self.la = _bdot(self.wa, f["a2"][...])

    def decay(self):
        f = self.refs
        log_w = -jax.nn.softplus(-(f["w0"][...] + self.lw)) - 0.5
        self.ld = -jnp.exp(log_w)
        self.a = _sigmoid(f["a0"][...] + self.la)
        self.kk = self.k * f["kk"][...]

    def mm_norm_cumsum(self):
        self.n2 = _seg_sum(self.kk * self.kk, self.sh["seg"], 1)
        self.g = _split_dot_left(self.sh["tri"], self.ld, 2)

    def build(self):
        f, C = self.refs, self.sh["C"]
        kk = self.kk / jnp.maximum(jnp.sqrt(self.n2), 1e-12)
        k = self.k * (1.0 + (self.a - 1.0) * f["ka"][...])
        bb = kk * self.a
        g, ld = self.g, self.ld
        self.g_last = g[C - 1:C, :]
        inv_g = jnp.exp(-g)
        to_end = jnp.exp(self.g_last - g)
        self.lhs = jnp.concatenate([self.r * jnp.exp(g), kk * jnp.exp(g - ld)], axis=0).astype(BF16)
        self.rhs = jnp.concatenate([k * inv_g, bb * inv_g], axis=0)
        self.kb_end = jnp.concatenate([k * to_end, -(bb * to_end)], axis=0)
        self.vb = self.v.astype(BF16)
        self.kmod = k

    @staticmethod
    def _grp(x, i):
        return x[:, i * RW_GROUP:(i + 1) * RW_GROUP]

    @staticmethod
    def _pair(x, i):
        return x[:, i * LANE:(i + 1) * LANE]

    def mm_state_in(self):
        st = self.refs["state"]
        p = [lax.dot_general(self._grp(self.lhs, i), st[self.bi, i].astype(BF16), (((1,), (1,)), ((), ())),
                             preferred_element_type=F32) for i in range(GROUP_W // RW_GROUP)]
        self.p = jnp.concatenate(p, axis=1)

    def mm_amat(self):
        lane2 = self.sh["lane2"]
        pair = self._pair

        def own2(hd, x):
            return jnp.where((lane2 < HEAD_DIM) == (hd % 2 == 0), x, 0.0)

        self.amat = [_bdot_nt(pair(self.lhs, hd // 2), own2(hd, pair(self.rhs, hd // 2)))
                     for hd in range(N_HEADS)]

    def mm_xinit(self):
        C, incl, strict, pair = self.sh["C"], self.sh["incl"], self.sh["strict"], self._pair
        self.a_y = [jnp.concatenate([jnp.where(incl, m[0:C, 0:C], 0.0), jnp.where(incl, -m[0:C, C:2 * C], 0.0)],
                                    axis=1).astype(BF16) for m in self.amat]
        self.xs = [pair(self.p, hd // 2)[C:2 * C, :]
                   + _bdot(jnp.where(strict, self.amat[hd][C:2 * C, 0:C], 0.0), pair(self.vb, hd // 2))
                   for hd in range(N_HEADS)]
        self.npow = [jnp.where(strict, -self.amat[hd][C:2 * C, C:2 * C], 0.0).astype(BF16)
                     for hd in range(N_HEADS)]
        self.span = 1

    def mm_level(self):
        C, span = self.sh["C"], self.span
        lo = span if span % BF16_ROWS == 0 else 0
        if 2 * span < C:
            outs = [_bdot(n[lo:, :], jnp.concatenate([n, x.astype(BF16)], axis=1))
                    for n, x in zip(self.npow, self.xs)]
            self.npow = [jnp.concatenate([jnp.zeros((lo, C), BF16), o[:, 0:C].astype(BF16)], axis=0) if lo
                         else o[:, 0:C].astype(BF16) for o in outs]
            upd = [o[:, C:] for o in outs]
        else:
            upd = [_bdot(n[lo:, :], x) for n, x in zip(self.npow, self.xs)]
        self.xs = [jnp.concatenate([x[0:lo, :], x[lo:, :] + d], axis=0) if lo else x + d
                   for x, d in zip(self.xs, upd)]
        self.span = 2 * span

    def mm_y(self):
        C, pair = self.sh["C"], self._pair
        y_h = [_bdot(self.a_y[hd], jnp.concatenate([pair(self.vb, hd // 2), self.xs[hd].astype(BF16)], axis=0))
               for hd in range(N_HEADS)]
        first = self.sh["lane1"] < HEAD_DIM
        n_pairs = GROUP_W // LANE
        y = jnp.concatenate([jnp.where(first, y_h[2 * i], y_h[2 * i + 1]) for i in range(n_pairs)], axis=1)
        self.u = jnp.concatenate([jnp.where(first, self.xs[2 * i], self.xs[2 * i + 1]) for i in range(n_pairs)],
                                 axis=1)
        self.y = y + self.p[0:C, :]

    def mm_state_out(self):
        st, seg = self.refs["state"], self.sh["seg"]
        vu_t = jnp.concatenate([self.v, self.u], axis=0).T
        decay_all = jnp.exp(self.g_last)
        for i in range(GROUP_W // RW_GROUP):
            s_new = (st[self.bi, i] * self._grp(decay_all, i)
                     + _bdot(vu_t[i * RW_GROUP:(i + 1) * RW_GROUP, :], self._grp(self.kb_end, i)))
            st[self.bi, i] = jnp.where(seg > 0.5, s_new, 0.0)

    def mm_mean(self):
        self.mean = _seg_sum(self.y, self.sh["seg"], 1) * (1.0 / HEAD_DIM)

    def mm_var_bonus(self):
        f, seg = self.refs, self.sh["seg"]
        self.yc = self.y - self.mean
        self.var = _seg_sum(self.yc * self.yc, seg, 1) * (1.0 / HEAD_DIM)
        self.bonus = _seg_sum(self.r * self.kmod * f["rk"][...], seg, 1) * self.v

    def finish(self):
        f = self.refs
        yn = self.yc * lax.rsqrt(self.var + GN_EPS) * f["lng"][...] + f["lnb"][...]
        f["o"][self.bi] = ((yn + self.bonus) * _silu(f["z"][self.bi])).astype(f["o"].dtype)


def _rwkv_kernel(r_ref, k_ref, v_ref, wa_ref, z_ref,
                 w0_ref, w2_ref, a0_ref, a2_ref, kk_ref, ka_ref, rk_ref, lng_ref, lnb_ref,
                 seg_ref, tri_ref,
                 o_ref,
                 state_ref):
    c = pl.program_id(1)
    C = r_ref.shape[1]

    @pl.when(c == 0)
    def _():
        state_ref[...] = jnp.zeros_like(state_ref)

    ti = lax.broadcasted_iota(jnp.int32, (C, C), 0)
    si = lax.broadcasted_iota(jnp.int32, (C, C), 1)
    shared = {"C": C, "seg": seg_ref[...], "tri": tri_ref[...], "incl": si <= ti, "strict": si < ti,
              "lane2": lax.broadcasted_iota(jnp.int32, (2 * C, LANE), 1),
              "lane1": lax.broadcasted_iota(jnp.int32, (C, LANE), 1)}
    refs = {"r": r_ref, "k": k_ref, "v": v_ref, "wa": wa_ref, "z": z_ref, "o": o_ref,
            "w0": w0_ref, "w2": w2_ref, "a0": a0_ref, "a2": a2_ref, "kk": kk_ref, "ka": ka_ref, "rk": rk_ref,
            "lng": lng_ref, "lnb": lnb_ref, "state": state_ref}
    streams = [_RwkvStream(bi, refs, shared) for bi in range(r_ref.shape[0])]
    n_levels = C.bit_length() - 1

    first = streams[0]
    first.load()
    first.mm_lora()
    first.decay()
    first.mm_norm_cumsum()
    first.build()
    for i, cur in enumerate(streams):
        prv = streams[i - 1] if i > 0 else None
        nxt = streams[i + 1] if i + 1 < len(streams) else None
        cur.mm_state_in()
        cur.mm_amat()
        if prv is not None:
            prv.mm_mean()
        if nxt is not None:
            nxt.load()
            nxt.mm_lora()
            nxt.decay()
        cur.mm_xinit()
        if prv is not None:
            prv.mm_var_bonus()
            prv.finish()
        if nxt is not None:
            nxt.mm_norm_cumsum()
            nxt.build()
        for _ in range(n_levels):
            cur.mm_level()
        cur.mm_y()
        cur.mm_state_out()
    last = streams[-1]
    last.mm_mean()
    last.mm_var_bonus()
    last.finish()


def _rwkv(proj, prm, batch, seq):
    C = RW_CHUNK
    nb = RW_STREAMS

    def col(base):
        return pl.BlockSpec((nb, C, GROUP_W), lambda b, c, base=base: (b, c, base // GROUP_W))

    def vec():
        return pl.BlockSpec((1, GROUP_W), lambda b, c: (0, 0))

    def whole(shape):
        return pl.BlockSpec(shape, lambda b, c: tuple(0 for _ in shape))

    in_specs = [col(COL_R), col(COL_K), col(COL_V),
                pl.BlockSpec((nb, C, LANE), lambda b, c: (b, c, COL_WA // LANE)),
                col(COL_ZA),
                vec(), whole((LANE, GROUP_W)), vec(), whole((LANE, GROUP_W)),
                vec(), vec(), vec(), vec(), vec(),
                whole((RW_GROUP, RW_GROUP)), whole((C, C))]
    return pl.pallas_call(
        _rwkv_kernel,
        grid=(batch // nb, seq // C),
        in_specs=in_specs,
        out_specs=pl.BlockSpec((nb, C, GROUP_W), lambda b, c: (b, c, 0)),
        out_shape=jax.ShapeDtypeStruct((batch, seq, GROUP_W), BF16),
        scratch_shapes=[pltpu.VMEM((nb, GROUP_W // RW_GROUP, RW_GROUP, RW_GROUP), F32)],
        compiler_params=_cparams(("parallel", "arbitrary")),
        name="rwkv7",
    )(proj, proj, proj, proj, proj,
      prm["w0"], prm["w2p"], prm["a0"], prm["a2p"], prm["k_k"], prm["k_a"], prm["r_k"],
      prm["lnx_g"], prm["lnx_b"], prm["seg"], prm["tri"])


def _lru_kernel(xc_ref, z_ref, gaw_ref, gab_ref, gxw_ref, gxb_ref, lam_ref, og_ref, o_ref, hprev_ref):
    c = pl.program_id(1)
    ct = xc_ref.shape[0]

    @pl.when(c == 0)
    def _():
        hprev_ref[...] = jnp.zeros_like(hprev_ref)

    xc = xc_ref[...]
    gate_r = _sigmoid(_bdot(xc, gaw_ref[...]) + gab_ref[...])
    gate_i = _sigmoid(_bdot(xc, gxw_ref[...]) + gxb_ref[...])
    log_a = -LRU_C * gate_r * jax.nn.softplus(-lam_ref[...])
    acc_a = jnp.exp(log_a)
    th = jnp.tanh(log_a)
    acc_b = jnp.sqrt(-2.0 * th / (1.0 - th)) * (gate_i * xc)
    n_groups = ct // SUBLANES
    acc_a = acc_a.reshape(n_groups, SUBLANES, GROUP_W)
    acc_b = acc_b.reshape(n_groups, SUBLANES, GROUP_W)
    sub = lax.broadcasted_iota(jnp.int32, acc_a.shape, 1)
    d = 1
    while d < SUBLANES:
        keep = sub >= d
        acc_b = acc_a * jnp.where(keep, pltpu.roll(acc_b, d, 1), 0.0) + acc_b
        acc_a = acc_a * jnp.where(keep, pltpu.roll(acc_a, d, 1), 1.0)
        d *= 2
    carry = hprev_ref[...]
    groups = []
    for i in range(n_groups):
        h_i = acc_a[i] * carry + acc_b[i]
        groups.append(h_i)
        carry = h_i[SUBLANES - 1:SUBLANES, :]
    hs = jnp.concatenate(groups, axis=0)
    hprev_ref[...] = hs[ct - 1:ct, :]
    ms = jnp.mean(hs * hs, axis=-1, keepdims=True)
    yb = hs * lax.rsqrt(ms + NORM_EPS) * og_ref[...]
    o_ref[...] = (yb * _silu(z_ref[...])).astype(o_ref.dtype)


def _lru(proj, prm, batch, seq, ct):
    def vec():
        return pl.BlockSpec((1, GROUP_W), lambda b, c: (0, 0))

    def mat(rows):
        return pl.BlockSpec((rows, GROUP_W), lambda b, c: (0, 0))

    return pl.pallas_call(
        _lru_kernel,
        grid=(batch, seq // ct),
        in_specs=[pl.BlockSpec((None, ct, GROUP_W), lambda b, c: (b, c, COL_UB // GROUP_W)),
                  pl.BlockSpec((None, ct, GROUP_W), lambda b, c: (b, c, COL_ZB // GROUP_W)),
                  mat(GROUP_W), vec(), mat(GROUP_W), vec(), vec(), vec()],
        out_specs=pl.BlockSpec((None, ct, GROUP_W), lambda b, c: (b, c, 0)),
        out_shape=jax.ShapeDtypeStruct((batch, seq, GROUP_W), BF16),
        scratch_shapes=[pltpu.VMEM((1, GROUP_W), F32)],
        compiler_params=_cparams(("parallel", "arbitrary")),
        name="rglru",
    )(proj, proj, prm["ga_w"], prm["ga_b"], prm["gx_w"], prm["gx_b"],
      prm["lam"], prm["lru_out_g"])


def _mla_pro_kernel(ql_ref, kvl_ref, kr_ref, qg_ref, kvg_ref, wq_ref, wqs_ref, wk_ref, wv_ref,
                    cq_ref, sq_ref, ck_ref, sk_ref, q_ref, k_ref, v_ref):
    def heads(tab_ref):
        return jnp.concatenate([tab_ref[...]] * N_HEADS, axis=1)

    ql = ql_ref[...]
    qn = ql * lax.rsqrt(jnp.mean(ql * ql, axis=-1, keepdims=True) + NORM_EPS) * qg_ref[...]
    qn = qn.astype(BF16)
    q = jnp.dot(qn, wq_ref[...], preferred_element_type=F32) * heads(cq_ref)
    q = q + jnp.dot(qn, wqs_ref[...], preferred_element_type=F32) * heads(sq_ref)
    q_ref[...] = q.astype(q_ref.dtype)

    kvl = kvl_ref[...]
    kvn = kvl * lax.rsqrt(jnp.mean(kvl * kvl, axis=-1, keepdims=True) + NORM_EPS) * kvg_ref[...]
    kvn = kvn.astype(BF16)
    kr = kr_ref[...]
    roped = kr * ck_ref[...] + pltpu.roll(kr * sk_ref[...], LANE - QK_ROPE, 1)
    placed = pltpu.roll(roped, QK_NOPE, 1)
    kfull = jnp.dot(kvn, wk_ref[...], preferred_element_type=F32) + jnp.concatenate([placed] * N_HEADS, axis=1)
    k_ref[...] = kfull.astype(k_ref.dtype)
    vt = lax.dot_general(wv_ref[...], kvn, (((1,), (1,)), ((), ())), preferred_element_type=F32)
    kblk = v_ref.shape[-1]
    for i in range(v_ref.shape[0]):
        v_ref[i] = vt[:, i * kblk:(i + 1) * kblk].astype(v_ref.dtype)


def _mla_pro(proj, prm, rope, batch, seq, tm, kblk):
    qk_w = N_HEADS * HEAD_PAD

    def whole(shape):
        return pl.BlockSpec(shape, lambda b, t: tuple(0 for _ in shape))

    def tab():
        return pl.BlockSpec((tm, LANE), lambda b, t: (t, 0))

    return pl.pallas_call(
        _mla_pro_kernel,
        grid=(batch, seq // tm),
        in_specs=[pl.BlockSpec((None, tm, Q_LORA), lambda b, t: (b, t, COL_QLAT // Q_LORA)),
                  pl.BlockSpec((None, tm, KV_LORA), lambda b, t: (b, t, COL_KVLAT // KV_LORA)),
                  pl.BlockSpec((None, tm, LANE), lambda b, t: (b, t, COL_KR // LANE)),
                  whole((1, Q_LORA)), whole((1, KV_LORA)),
                  whole((Q_LORA, qk_w)), whole((Q_LORA, qk_w)), whole((KV_LORA, qk_w)), whole((GROUP_W, KV_LORA)),
                  tab(), tab(), tab(), tab()],
        out_specs=[pl.BlockSpec((None, tm, qk_w), lambda b, t: (b, t, 0)),
                   pl.BlockSpec((None, tm, qk_w), lambda b, t: (b, t, 0)),
                   pl.BlockSpec((None, tm // kblk, GROUP_W, kblk), lambda b, t: (b, t, 0, 0))],
        out_shape=[jax.ShapeDtypeStruct((batch, seq, qk_w), BF16),
                   jax.ShapeDtypeStruct((batch, seq, qk_w), BF16),
                   jax.ShapeDtypeStruct((batch, seq // kblk, GROUP_W, kblk), BF16)],
        compiler_params=_cparams(("parallel", "parallel")),
        name="mla_pro",
    )(proj, proj, proj, prm["q_norm_g"], prm["kv_norm_g"], prm["wq"], prm["wq_sw"], prm["wk"], prm["wv_t"],
      rope["cq"], rope["sq"], rope["ck"], rope["sk"])


def _attn_kernel(q_ref, k_ref, vt_ref, o_ref, m_ref, acc_ref, s_ref, mx_ref, *, blk, kblk, n_steps):
    step = pl.program_id(2)
    ones = jnp.ones((ATT_ONES_ROWS, kblk), BF16)
    key = lax.broadcasted_iota(jnp.int32, (kblk, blk), 0)
    qry = lax.broadcasted_iota(jnp.int32, (kblk, blk), 1)
    n_buf = s_ref.shape[1]

    def stages(slot, i):
        n_keys = 2 * i + 2
        row0 = slot * blk

        def init():
            m_ref[slot] = jnp.full(m_ref.shape[1:], -jnp.inf, F32)
            acc_ref[slot] = jnp.zeros(acc_ref.shape[1:], F32)

        def scores(t):
            buf = t % n_buf
            for hh in range(2):
                lanes = slice(hh * HEAD_PAD, (hh + 1) * HEAD_PAD)
                s = lax.dot_general(k_ref[t * kblk:(t + 1) * kblk, lanes], q_ref[row0:row0 + blk, lanes],
                                    (((1,), (1,)), ((), ())), preferred_element_type=F32)
                s_ref[slot, buf, hh] = s
                if t < 2 * i:
                    mx_ref[slot, buf, hh:hh + 1, :] = jnp.max(s, axis=0, keepdims=True)

        def consume(t):
            buf = t % n_buf
            for hh in range(2):
                s = s_ref[slot, buf, hh]
                if t < 2 * i:
                    mx = mx_ref[slot, buf, hh:hh + 1, :]
                else:
                    s = jnp.where(key + (t - 2 * i) * kblk <= qry, s, -jnp.inf)
                    mx = jnp.max(s, axis=0, keepdims=True)
                m_prev = m_ref[slot, hh:hh + 1, :]
                m_new = jnp.maximum(m_prev, mx)
                alpha = jnp.exp2(m_prev - m_new)
                p = jnp.exp2(s - m_new).astype(BF16)
                m_ref[slot, hh:hh + 1, :] = m_new
                rows = slice(hh * ATT_ACC_ROWS, (hh + 1) * ATT_ACC_ROWS)
                vt_ext = jnp.concatenate([vt_ref[t, hh * V_DIM:(hh + 1) * V_DIM, :], ones], axis=0)
                acc_ref[slot, rows, :] = (alpha * acc_ref[slot, rows, :]
                                          + jnp.dot(vt_ext, p, preferred_element_type=F32))

        def finalize():
            outs = []
            for hh in range(2):
                base = hh * ATT_ACC_ROWS
                outs.append(acc_ref[slot, base:base + V_DIM, :] / acc_ref[slot, base + V_DIM:base + V_DIM + 1, :])
            o_ref[row0:row0 + blk, :] = jnp.concatenate(outs, axis=0).T

        ops = [init] + [functools.partial(scores, t) for t in range(min(ATT_AHEAD, n_keys))]
        for t in range(n_keys):
            if t + ATT_AHEAD < n_keys:
                ops.append(functools.partial(scores, t + ATT_AHEAD))
            ops.append(functools.partial(consume, t))
        return ops + [finalize]

    def run_step(j):
        queues = [stages(slot, ATT_SUB * j + slot) for slot in range(ATT_SUB)]
        for pos in range(max(len(ops) for ops in queues)):
            for ops in queues:
                if pos < len(ops):
                    ops[pos]()

    for j in range(n_steps):
        pl.when(step == j)(functools.partial(run_step, j))


def _attn(q, k, vt, batch, seq, blk, kblk):
    pair_w = 2 * HEAD_PAD
    n_buf = ATT_AHEAD + 1
    rows = ATT_SUB * blk
    return pl.pallas_call(
        functools.partial(_attn_kernel, blk=blk, kblk=kblk, n_steps=seq // rows),
        grid=(batch, N_HEADS // 2, seq // rows),
        in_specs=[pl.BlockSpec((None, rows, pair_w), lambda b, h, i: (b, i, h)),
                  pl.BlockSpec((None, seq, pair_w), lambda b, h, i: (b, 0, h)),
                  pl.BlockSpec((None, seq // kblk, LANE, kblk), lambda b, h, i: (b, 0, h, 0))],
        out_specs=pl.BlockSpec((None, rows, LANE), lambda b, h, i: (b, i, h)),
        out_shape=jax.ShapeDtypeStruct((batch, seq, GROUP_W), F32),
        scratch_shapes=[pltpu.VMEM((ATT_SUB, SUBLANES, blk), F32), pltpu.VMEM((ATT_SUB, 2 * ATT_ACC_ROWS, blk), F32),
                        pltpu.VMEM((ATT_SUB, n_buf, 2, kblk, blk), F32),
                        pltpu.VMEM((ATT_SUB, n_buf, SUBLANES, blk), F32)],
        compiler_params=_cparams(("parallel", "parallel", "arbitrary")),
        name="mla_attn",
    )(q, k, vt)


def _outproj_kernel(ya_ref, yb_ref, oc_ref, zc_ref, gc_ref, w_ref, h_ref, fg_ref, o_ref, *, final):
    oc = oc_ref[...]
    yc = oc * lax.rsqrt(jnp.mean(oc * oc, axis=-1, keepdims=True) + NORM_EPS) * gc_ref[...]
    yc = yc * _silu(zc_ref[...])
    acc = h_ref[...]
    acc = acc + jnp.dot(ya_ref[...], w_ref[0:GROUP_W, :], preferred_element_type=F32)
    acc = acc + jnp.dot(yb_ref[...], w_ref[GROUP_W:2 * GROUP_W, :], preferred_element_type=F32)
    acc = acc + _bdot(yc, w_ref[2 * GROUP_W:3 * GROUP_W, :])
    if final:
        acc = acc * lax.rsqrt(jnp.mean(acc * acc, axis=-1, keepdims=True) + NORM_EPS) * fg_ref[...]
    o_ref[...] = acc


def _outproj(ya, yb, oc, proj, h2d, mla_out_g, w_out, final_g, final, tm):
    m = h2d.shape[0]

    def rows(width, blk=0):
        return pl.BlockSpec((tm, width), lambda i, blk=blk: (i, blk))

    return pl.pallas_call(
        functools.partial(_outproj_kernel, final=final),
        grid=(m // tm,),
        in_specs=[rows(GROUP_W), rows(GROUP_W), rows(GROUP_W), rows(GROUP_W, COL_ZC // GROUP_W),
                  pl.BlockSpec((1, GROUP_W), lambda i: (0, 0)),
                  pl.BlockSpec((3 * GROUP_W, D_MODEL), lambda i: (0, 0)),
                  rows(D_MODEL),
                  pl.BlockSpec((1, D_MODEL), lambda i: (0, 0))],
        out_specs=rows(D_MODEL),
        out_shape=jax.ShapeDtypeStruct((m, D_MODEL), F32),
        compiler_params=_cparams(("parallel",)),
        name="outproj_final" if final else "outproj",
    )(ya, yb, oc, proj, mla_out_g, w_out, h2d, final_g)


def _rope_tables(seq):
    half = QK_ROPE // 2
    inv_freq = ROPE_BASE ** (-jnp.arange(half, dtype=F32) * 2.0 / QK_ROPE)
    ang = jnp.arange(seq, dtype=F32)[:, None] * inv_freq[None, :]
    cos2 = jnp.concatenate([jnp.cos(ang)] * 2, axis=1)
    sin2 = jnp.concatenate([jnp.sin(ang)] * 2, axis=1)
    zeros = lambda w: jnp.zeros((seq, w), F32)
    ones = jnp.ones((seq, QK_NOPE), F32)
    return {
        "cq": jnp.concatenate([ones, cos2, zeros(HEAD_PAD - QK_NOPE - QK_ROPE)], axis=1),
        "sq": jnp.concatenate([zeros(QK_NOPE), sin2, zeros(HEAD_PAD - QK_NOPE - QK_ROPE)], axis=1),
        "ck": jnp.concatenate([cos2, zeros(LANE - QK_ROPE)], axis=1),
        "sk": jnp.concatenate([zeros(QK_ROPE), sin2, zeros(LANE - 2 * QK_ROPE)], axis=1),
    }


def _swap_halves(w):
    half = w.shape[-1] // 2
    return jnp.concatenate([-w[..., half:], w[..., :half]], axis=-1)


def _block_diag(w):
    h, n, _ = w.shape
    eye = jnp.eye(h, dtype=w.dtype)
    return (eye[:, None, :, None] * w[:, :, None, :]).reshape(h * n, h * n)


def _layer_params(l, ln_g, w_in, rwkv_mu, rwkv_w0, rwkv_w2, rwkv_a0, rwkv_a2, rwkv_k_k, rwkv_k_a, rwkv_r_k,
                  rwkv_lnx_g, rwkv_lnx_b, lru_conv_w, lru_conv_b, lru_ga_w, lru_ga_b, lru_gx_w, lru_gx_b,
                  lru_lam, lru_out_g, mla_q_norm_g, mla_w_uq, mla_kv_norm_g, mla_w_ukv, mla_out_g, w_out):
    G = GROUP_W
    w = w_in[l]
    o_ua, o_za = 0, 3 * G + 2 * LORA
    o_ub = o_za + G
    o_zb = o_ub + G
    o_ql = o_zb + G
    o_kv = o_ql + Q_LORA
    o_kr = o_kv + KV_LORA
    o_zc = o_kr + QK_ROPE
    sl = lambda a, n: w[:, a:a + n]
    kr_w = sl(o_kr, QK_ROPE)
    w_re = jnp.concatenate([
        sl(o_ua, G), sl(o_ua + G, G), sl(o_ua + 2 * G, G), sl(o_za, G), sl(o_ub, G), sl(o_zb, G), sl(o_zc, G),
        sl(o_ql, Q_LORA), sl(o_kv, KV_LORA),
        kr_w, _swap_halves(kr_w), jnp.zeros((D_MODEL, LANE - 2 * QK_ROPE), F32),
        sl(o_ua + 3 * G, 2 * LORA)], axis=1).astype(BF16)
    row = lambda v: v.reshape(1, -1)
    mu = rwkv_mu[l]
    zl = jnp.zeros((LORA, G), F32)
    head_id = jnp.arange(RW_GROUP) // HEAD_DIM
    t_idx = jnp.arange(RW_CHUNK)
    scale = (QK_NOPE + QK_ROPE) ** -0.5 * LOG2E
    wq3 = mla_w_uq[l].reshape(Q_LORA, N_HEADS, QK_NOPE + QK_ROPE) * scale
    zq = lambda n: jnp.zeros((Q_LORA, N_HEADS, n), F32)
    wq = jnp.concatenate([wq3, zq(HEAD_PAD - QK_NOPE - QK_ROPE)], axis=2)
    wq_sw = jnp.concatenate([zq(QK_NOPE), _swap_halves(wq3[:, :, QK_NOPE:]),
                             zq(HEAD_PAD - QK_NOPE - QK_ROPE)], axis=2)
    wkv3 = mla_w_ukv[l].reshape(KV_LORA, N_HEADS, QK_NOPE + V_DIM)
    wk = jnp.concatenate([wkv3[:, :, :QK_NOPE], jnp.zeros((KV_LORA, N_HEADS, HEAD_PAD - QK_NOPE), F32)], axis=2)
    return {
        "ln_g": row(ln_g[l]), "w_in": w_re,
        "mu_rkv": row(mu[0:3 * G]),
        "mu_wa": row(mu[3 * G:3 * G + 2 * LORA]),
        "w0": row(rwkv_w0[l]), "w2p": jnp.concatenate([rwkv_w2[l], zl], axis=0).astype(BF16),
        "a0": row(rwkv_a0[l]), "a2p": jnp.concatenate([zl, rwkv_a2[l]], axis=0).astype(BF16),
        "k_k": row(rwkv_k_k[l]), "k_a": row(rwkv_k_a[l]), "r_k": row(rwkv_r_k[l]),
        "lnx_g": row(rwkv_lnx_g[l]), "lnx_b": row(rwkv_lnx_b[l]),
        "seg": (head_id[:, None] == head_id[None, :]).astype(BF16),
        "tri": (t_idx[None, :] <= t_idx[:, None]).astype(BF16),
        "conv_w": lru_conv_w[l], "conv_b": row(lru_conv_b[l]),
        "ga_w": _block_diag(lru_ga_w[l]).astype(BF16), "ga_b": row(lru_ga_b[l]),
        "gx_w": _block_diag(lru_gx_w[l]).astype(BF16), "gx_b": row(lru_gx_b[l]),
        "lam": row(lru_lam[l]), "lru_out_g": row(lru_out_g[l]),
        "q_norm_g": row(mla_q_norm_g[l]), "kv_norm_g": row(mla_kv_norm_g[l]),
        "wq": wq.reshape(Q_LORA, -1).astype(BF16), "wq_sw": wq_sw.reshape(Q_LORA, -1).astype(BF16),
        "wk": wk.reshape(KV_LORA, -1).astype(BF16),
        "wv_t": wkv3[:, :, QK_NOPE:].reshape(KV_LORA, -1).T.astype(BF16),
        "mla_out_g": row(mla_out_g[l]), "w_out": w_out[l].astype(BF16),
    }


def _tiles(seq):
    q_blk = min(ATT_Q_BLOCK, seq // ATT_SUB)
    return {"rows": min(ROW_TILE, seq),
            "out_rows": min(OUT_ROW_TILE, seq),
            "lru": min(LRU_CHUNK, seq),
            "q": q_blk, "key": q_blk // 2}


def kernel(x, ln_g, w_in, rwkv_mu, rwkv_w0, rwkv_w2, rwkv_a0, rwkv_a2, rwkv_k_k, rwkv_k_a, rwkv_r_k,
           rwkv_lnx_g, rwkv_lnx_b, lru_conv_w, lru_conv_b, lru_ga_w, lru_ga_b, lru_gx_w, lru_gx_b, lru_lam,
           lru_out_g, mla_q_norm_g, mla_w_uq, mla_kv_norm_g, mla_w_ukv, mla_out_g, w_out, final_g):
    batch, seq, _ = x.shape
    assert batch % RW_STREAMS == 0 and seq % RW_CHUNK == 0, (batch, seq)
    tiles = _tiles(seq)
    rope = _rope_tables(seq)
    fg = final_g.reshape(1, -1)
    h = x.reshape(batch * seq, D_MODEL)
    for l in range(DEPTH):
        prm = _layer_params(l, ln_g, w_in, rwkv_mu, rwkv_w0, rwkv_w2, rwkv_a0, rwkv_a2, rwkv_k_k, rwkv_k_a,
                            rwkv_r_k, rwkv_lnx_g, rwkv_lnx_b, lru_conv_w, lru_conv_b, lru_ga_w, lru_ga_b,
                            lru_gx_w, lru_gx_b, lru_lam, lru_out_g, mla_q_norm_g, mla_w_uq, mla_kv_norm_g,
                            mla_w_ukv, mla_out_g, w_out)
        proj2d = _inproj(h, prm, tiles["rows"], seq)
        proj = proj2d.reshape(batch, seq, D_PROJ)
        ya = _rwkv(proj, prm, batch, seq)
        yb = _lru(proj, prm, batch, seq, tiles["lru"])
        q, k, vt = _mla_pro(proj, prm, rope, batch, seq, tiles["q"], tiles["key"])
        oc = _attn(q, k, vt, batch, seq, tiles["q"], tiles["key"])
        h = _outproj(ya.reshape(batch * seq, GROUP_W), yb.reshape(batch * seq, GROUP_W),
                     oc.reshape(batch * seq, GROUP_W), proj2d, h, prm["mla_out_g"], prm["w_out"], fg,
                     l == DEPTH - 1, tiles["out_rows"])
    return h.reshape(batch, seq, D_MODEL)
```

```python
import functools

import jax
import jax.numpy as jnp
from jax import lax
from jax.experimental import pallas as pl
from jax.experimental.pallas import tpu as pltpu

F32 = jnp.float32
BF16 = jnp.bfloat16

D_MODEL = 1024
DEPTH = 4
GROUP_W = 512
HEAD_DIM = 64
N_HEADS = 8
NORM_EPS = 1e-6
LORA = 64
GN_EPS = 64e-5
CONV_W = 4
LRU_C = 8.0
Q_LORA = 256
KV_LORA = 128
QK_NOPE = 64
QK_ROPE = 32
V_DIM = 64
ROPE_BASE = 10000.0

LANE = 128
SUBLANES = 8
HEAD_PAD = 128
ROW_TILE = 512
OUT_ROW_TILE = 1024
LRU_CHUNK = 256
ATT_Q_BLOCK = 512
RW_GROUP = 256
RW_CHUNK = 128
RW_STREAMS = 4
RW_LOCKSTEP = 2
BF16_ROWS = 16
ATT_ONES_ROWS = BF16_ROWS
ATT_ACC_ROWS = V_DIM + ATT_ONES_ROWS
ATT_AHEAD = 2
ATT_SUB = 2
LOG2E = 1.4426950408889634
COL_R, COL_K, COL_V, COL_ZA, COL_UB, COL_ZB, COL_ZC = (i * GROUP_W for i in range(7))
COL_QLAT = 7 * GROUP_W
COL_KVLAT = COL_QLAT + Q_LORA
COL_KR = COL_KVLAT + KV_LORA
COL_WA = COL_KR + LANE
D_PROJ = COL_WA + LANE

VMEM_LIMIT = 56 * 1024 * 1024


def _cparams(sem):
    return pltpu.CompilerParams(dimension_semantics=sem, vmem_limit_bytes=VMEM_LIMIT)


def _bdot(a, b):
    return jnp.dot(a.astype(BF16), b.astype(BF16), preferred_element_type=F32)


def _bdot_nt(a, b):
    return lax.dot_general(a.astype(BF16), b.astype(BF16), (((1,), (1,)), ((), ())),
                           preferred_element_type=F32)


def _split_dot(x, m, passes):
    acc = None
    rem = x
    for i in range(passes):
        part = rem.astype(BF16)
        term = jnp.dot(part, m, preferred_element_type=F32)
        acc = term if acc is None else acc + term
        if i + 1 < passes:
            rem = rem - part.astype(F32)
    return acc


def _sigmoid(x):
    return 0.5 * jnp.tanh(0.5 * x) + 0.5


def _silu(z):
    return z * _sigmoid(z)


def _shift_rows(x, tail, d):
    rolled = pltpu.roll(x, d, 0)
    row8 = lax.broadcasted_iota(jnp.int32, tail.shape, 0)
    first = jnp.where(row8 < d, pltpu.roll(tail, d, 0), rolled[0:SUBLANES, :])
    return jnp.concatenate([first, rolled[SUBLANES:, :]], axis=0)


def _inproj_kernel(x_ref, g_ref, w_ref, mu_ref, muwa_ref, cw_ref, cb_ref, o_ref, tail_ref, *, tiles_per_seq):
    i = pl.program_id(0)
    tm = x_ref.shape[0]
    x = x_ref[...]
    ms = jnp.mean(x * x, axis=-1, keepdims=True)
    xn = x * lax.rsqrt(ms + NORM_EPS) * g_ref[...]
    y = _bdot(xn, w_ref[...])
    tail = jnp.where(i % tiles_per_seq == 0, 0.0, tail_ref[...])
    tail_ref[...] = y[tm - SUBLANES:tm, :]

    def lerp(lo, width, mu):
        cur = y[:, lo:lo + width]
        return cur + mu * (_shift_rows(cur, tail[:, lo:lo + width], 1) - cur)

    o_ref[:, COL_R:COL_ZA] = lerp(COL_R, COL_ZA - COL_R, mu_ref[...])
    o_ref[:, COL_ZA:COL_UB] = y[:, COL_ZA:COL_UB]
    u = y[:, COL_UB:COL_ZB]
    u_tail = tail[:, COL_UB:COL_ZB]
    xc = cb_ref[...] + u * cw_ref[CONV_W - 1:CONV_W, :]
    for d in range(1, CONV_W):
        xc = xc + _shift_rows(u, u_tail, d) * cw_ref[CONV_W - 1 - d:CONV_W - d, :]
    o_ref[:, COL_UB:COL_ZB] = xc
    o_ref[:, COL_ZB:COL_WA] = y[:, COL_ZB:COL_WA]
    o_ref[:, COL_WA:D_PROJ] = lerp(COL_WA, D_PROJ - COL_WA, muwa_ref[...])


def _inproj(h2d, prm, tm, seq):
    m = h2d.shape[0]

    def whole(shape):
        return pl.BlockSpec(shape, lambda i: tuple(0 for _ in shape))

    return pl.pallas_call(
        functools.partial(_inproj_kernel, tiles_per_seq=seq // tm),
        grid=(m // tm,),
        in_specs=[pl.BlockSpec((tm, D_MODEL), lambda i: (i, 0)),
                  whole((1, D_MODEL)), whole((D_MODEL, D_PROJ)),
                  whole((1, COL_ZA - COL_R)), whole((1, LANE)), whole((CONV_W, GROUP_W)), whole((1, GROUP_W))],
        out_specs=pl.BlockSpec((tm, D_PROJ), lambda i: (i, 0)),
        out_shape=jax.ShapeDtypeStruct((m, D_PROJ), F32),
        scratch_shapes=[pltpu.VMEM((SUBLANES, D_PROJ), F32)],
        compiler_params=_cparams(("arbitrary",)),
        name="inproj",
    )(h2d, prm["ln_g"], prm["w_in"], prm["mu_rkv"], prm["mu_wa"], prm["conv_w"], prm["conv_b"])


def _split_dot_left(m, x, passes):
    acc = None
    rem = x
    for i in range(passes):
        part = rem.astype(BF16)
        term = jnp.dot(m, part, preferred_element_type=F32)
        acc = term if acc is None else acc + term
        if i + 1 < passes:
            rem = rem - part.astype(F32)
    return acc


def _seg_sum(x, seg, passes):
    parts = [_split_dot(x[:, i:i + RW_GROUP], seg, passes) for i in range(0, x.shape[1], RW_GROUP)]
    return jnp.concatenate(parts, axis=1)


class _RwkvStream:
    def __init__(self, bi, refs, shared):
        self.bi = bi
        self.refs = refs
        self.sh = shared

    def load(self):
        f = self.refs
        self.r, self.k, self.v, self.wa = f["r"][self.bi], f["k"][self.bi], f["v"][self.bi], f["wa"][self.bi]

    def mm_lora(self):
        f = self.refs
        self.lw = _bdot(jnp.tanh(self.wa), f["w2"][...])
        self.la = _bdot(self.wa, f["a2"][...])

    def decay(self):
        f = self.refs
        log_w = -jax.nn.softplus(-(f["w0"][...] + self.lw)) - 0.5
        self.ld = -jnp.exp(log_w)
        self.a = _sigmoid(f["a0"][...] + self.la)
        self.kk = self.k * f["kk"][...]

    def mm_norm_cumsum(self):
        self.n2 = _seg_sum(self.kk * self.kk, self.sh["seg"], 1)
        self.g = _split_dot_left(self.sh["tri"], self.ld, 2)

    def build(self):
        f, C = self.refs, self.sh["C"]
        kk = self.kk / jnp.maximum(jnp.sqrt(self.n2), 1e-12)
        k = self.k * (1.0 + (self.a - 1.0) * f["ka"][...])
        bb = kk * self.a
        g, ld = self.g, self.ld
        self.g_last = g[C - 1:C, :]
        inv_g = jnp.exp(-g)
        to_end = jnp.exp(self.g_last - g)
        self.lhs = jnp.concatenate([self.r * jnp.exp(g), kk * jnp.exp(g - ld)], axis=0).astype(BF16)
        self.rhs = jnp.concatenate([k * inv_g, bb * inv_g], axis=0)
        self.kb_end = jnp.concatenate([k * to_end, -(bb * to_end)], axis=0)
        self.vb = self.v.astype(BF16)
        self.kmod = k

    @staticmethod
    def _grp(x, i):
        return x[:, i * RW_GROUP:(i + 1) * RW_GROUP]

    @staticmethod
    def _pair(x, i):
        return x[:, i * LANE:(i + 1) * LANE]

    def mm_state_in(self):
        st = self.refs["state"]
        p = [lax.dot_general(self._grp(self.lhs, i), st[self.bi, i].astype(BF16), (((1,), (1,)), ((), ())),
                             preferred_element_type=F32) for i in range(GROUP_W // RW_GROUP)]
        self.p = jnp.concatenate(p, axis=1)

    def mm_amat(self):
        lane2 = self.sh["lane2"]
        pair = self._pair

        def own2(hd, x):
            return jnp.where((lane2 < HEAD_DIM) == (hd % 2 == 0), x, 0.0)

        self.amat = [_bdot_nt(pair(self.lhs, hd // 2), own2(hd, pair(self.rhs, hd // 2)))
                     for hd in range(N_HEADS)]

    def mm_xinit(self):
        C, incl, strict, pair = self.sh["C"], self.sh["incl"], self.sh["strict"], self._pair
        self.a_y = [jnp.concatenate([jnp.where(incl, m[0:C, 0:C], 0.0), jnp.where(incl, -m[0:C, C:2 * C], 0.0)],
                                    axis=1).astype(BF16) for m in self.amat]
        self.xs = [pair(self.p, hd // 2)[C:2 * C, :]
                   + _bdot(jnp.where(strict, self.amat[hd][C:2 * C, 0:C], 0.0), pair(self.vb, hd // 2))
                   for hd in range(N_HEADS)]
        self.npow = [jnp.where(strict, -self.amat[hd][C:2 * C, C:2 * C], 0.0).astype(BF16)
                     for hd in range(N_HEADS)]
        self.span = 1

    def mm_level(self):
        C, span = self.sh["C"], self.span
        lo = span if span % BF16_ROWS == 0 else 0
        if 2 * span < C:
            outs = [_bdot(n[lo:, :], jnp.concatenate([n, x.astype(BF16)], axis=1))
                    for n, x in zip(self.npow, self.xs)]
            self.npow = [jnp.concatenate([jnp.zeros((lo, C), BF16), o[:, 0:C].astype(BF16)], axis=0) if lo
                         else o[:, 0:C].astype(BF16) for o in outs]
            upd = [o[:, C:] for o in outs]
        else:
            upd = [_bdot(n[lo:, :], x) for n, x in zip(self.npow, self.xs)]
        self.xs = [jnp.concatenate([x[0:lo, :], x[lo:, :] + d], axis=0) if lo else x + d
                   for x, d in zip(self.xs, upd)]
        self.span = 2 * span

    def mm_y(self):
        C, pair = self.sh["C"], self._pair
        y_h = [_bdot(self.a_y[hd], jnp.concatenate([pair(self.vb, hd // 2), self.xs[hd].astype(BF16)], axis=0))
               for hd in range(N_HEADS)]
        first = self.sh["lane1"] < HEAD_DIM
        n_pairs = GROUP_W // LANE
        y = jnp.concatenate([jnp.where(first, y_h[2 * i], y_h[2 * i + 1]) for i in range(n_pairs)], axis=1)
        self.u = jnp.concatenate([jnp.where(first, self.xs[2 * i], self.xs[2 * i + 1]) for i in range(n_pairs)],
                                 axis=1)
        self.y = y + self.p[0:C, :]

    def mm_state_out(self):
        st, seg = self.refs["state"], self.sh["seg"]
        vu_t = jnp.concatenate([self.v, self.u], axis=0).T
        decay_all = jnp.exp(self.g_last)
        for i in range(GROUP_W // RW_GROUP):
            s_new = (st[self.bi, i] * self._grp(decay_all, i)
                     + _bdot(vu_t[i * RW_GROUP:(i + 1) * RW_GROUP, :], self._grp(self.kb_end, i)))
            st[self.bi, i] = jnp.where(seg > 0.5, s_new, 0.0)

    def mm_mean(self):
        self.mean = _seg_sum(self.y, self.sh["seg"], 1) * (1.0 / HEAD_DIM)

    def mm_var_bonus(self):
        f, seg = self.refs, self.sh["seg"]
        self.yc = self.y - self.mean
        self.var = _seg_sum(self.yc * self.yc, seg, 1) * (1.0 / HEAD_DIM)
        self.bonus = _seg_sum(self.r * self.kmod * f["rk"][...], seg, 1) * self.v

    def finish(self):
        f = self.refs
        yn = self.yc * lax.rsqrt(self.var + GN_EPS) * f["lng"][...] + f["lnb"][...]
        f["o"][self.bi] = ((yn + self.bonus) * _silu(f["z"][self.bi])).astype(f["o"].dtype)


class _Lockstep:
    def __init__(self, members):
        self.members = members

    def __getattr__(self, stage):
        def run():
            for m in self.members:
                getattr(m, stage)()
        return run


def _rwkv_kernel(r_ref, k_ref, v_ref, wa_ref, z_ref,
                 w0_ref, w2_ref, a0_ref, a2_ref, kk_ref, ka_ref, rk_ref, lng_ref, lnb_ref,
                 seg_ref, tri_ref,
                 o_ref,
                 state_ref):
    c = pl.program_id(1)
    C = r_ref.shape[1]

    @pl.when(c == 0)
    def _():
        state_ref[...] = jnp.zeros_like(state_ref)

    ti = lax.broadcasted_iota(jnp.int32, (C, C), 0)
    si = lax.broadcasted_iota(jnp.int32, (C, C), 1)
    shared = {"C": C, "seg": seg_ref[...], "tri": tri_ref[...], "incl": si <= ti, "strict": si < ti,
              "lane2": lax.broadcasted_iota(jnp.int32, (2 * C, LANE), 1),
              "lane1": lax.broadcasted_iota(jnp.int32, (C, LANE), 1)}
    refs = {"r": r_ref, "k": k_ref, "v": v_ref, "wa": wa_ref, "z": z_ref, "o": o_ref,
            "w0": w0_ref, "w2": w2_ref, "a0": a0_ref, "a2": a2_ref, "kk": kk_ref, "ka": ka_ref, "rk": rk_ref,
            "lng": lng_ref, "lnb": lnb_ref, "state": state_ref}
    rows = [_RwkvStream(bi, refs, shared) for bi in range(r_ref.shape[0])]
    streams = [_Lockstep(rows[i:i + RW_LOCKSTEP]) for i in range(0, len(rows), RW_LOCKSTEP)]
    n_levels = C.bit_length() - 1

    first = streams[0]
    first.load()
    first.mm_lora()
    first.decay()
    first.mm_norm_cumsum()
    first.build()
    for i, cur in enumerate(streams):
        prv = streams[i - 1] if i > 0 else None
        nxt = streams[i + 1] if i + 1 < len(streams) else None
        cur.mm_state_in()
        cur.mm_amat()
        if prv is not None:
            prv.mm_mean()
        if nxt is not None:
            nxt.load()
            nxt.mm_lora()
            nxt.decay()
        cur.mm_xinit()
        if prv is not None:
            prv.mm_var_bonus()
            prv.finish()
        if nxt is not None:
            nxt.mm_norm_cumsum()
            nxt.build()
        for _ in range(n_levels):
            cur.mm_level()
        cur.mm_y()
        cur.mm_state_out()
    last = streams[-1]
    last.mm_mean()
    last.mm_var_bonus()
    last.finish()


def _rwkv(proj, prm, batch, seq):
    C = RW_CHUNK
    nb = RW_STREAMS

    def col(base):
        return pl.BlockSpec((nb, C, GROUP_W), lambda b, c, base=base: (b, c, base // GROUP_W))

    def vec():
        return pl.BlockSpec((1, GROUP_W), lambda b, c: (0, 0))

    def whole(shape):
        return pl.BlockSpec(shape, lambda b, c: tuple(0 for _ in shape))

    in_specs = [col(COL_R), col(COL_K), col(COL_V),
                pl.BlockSpec((nb, C, LANE), lambda b, c: (b, c, COL_WA // LANE)),
                col(COL_ZA),
                vec(), whole((LANE, GROUP_W)), vec(), whole((LANE, GROUP_W)),
                vec(), vec(), vec(), vec(), vec(),
                whole((RW_GROUP, RW_GROUP)), whole((C, C))]
    return pl.pallas_call(
        _rwkv_kernel,
        grid=(batch // nb, seq // C),
        in_specs=in_specs,
        out_specs=pl.BlockSpec((nb, C, GROUP_W), lambda b, c: (b, c, 0)),
        out_shape=jax.ShapeDtypeStruct((batch, seq, GROUP_W), BF16),
        scratch_shapes=[pltpu.VMEM((nb, GROUP_W // RW_GROUP, RW_GROUP, RW_GROUP), F32)],
        compiler_params=_cparams(("parallel", "arbitrary")),
        name="rwkv7",
    )(proj, proj, proj, proj, proj,
      prm["w0"], prm["w2p"], prm["a0"], prm["a2p"], prm["k_k"], prm["k_a"], prm["r_k"],
      prm["lnx_g"], prm["lnx_b"], prm["seg"], prm["tri"])


def _lru_kernel(xc_ref, z_ref, gaw_ref, gab_ref, gxw_ref, gxb_ref, lam_ref, og_ref, o_ref, hprev_ref):
    c = pl.program_id(1)
    ct = xc_ref.shape[0]

    @pl.when(c == 0)
    def _():
        hprev_ref[...] = jnp.zeros_like(hprev_ref)

    xc = xc_ref[...]
    gate_r = _sigmoid(_bdot(xc, gaw_ref[...]) + gab_ref[...])
    gate_i = _sigmoid(_bdot(xc, gxw_ref[...]) + gxb_ref[...])
    log_a = -LRU_C * gate_r * jax.nn.softplus(-lam_ref[...])
    acc_a = jnp.exp(log_a)
    th = jnp.tanh(log_a)
    acc_b = jnp.sqrt(-2.0 * th / (1.0 - th)) * (gate_i * xc)
    n_groups = ct // SUBLANES
    acc_a = acc_a.reshape(n_groups, SUBLANES, GROUP_W)
    acc_b = acc_b.reshape(n_groups, SUBLANES, GROUP_W)
    sub = lax.broadcasted_iota(jnp.int32, acc_a.shape, 1)
    d = 1
    while d < SUBLANES:
        keep = sub >= d
        acc_b = acc_a * jnp.where(keep, pltpu.roll(acc_b, d, 1), 0.0) + acc_b
        acc_a = acc_a * jnp.where(keep, pltpu.roll(acc_a, d, 1), 1.0)
        d *= 2
    carry = hprev_ref[...]
    groups = []
    for i in range(n_groups):
        h_i = acc_a[i] * carry + acc_b[i]
        groups.append(h_i)
        carry = h_i[SUBLANES - 1:SUBLANES, :]
    hs = jnp.concatenate(groups, axis=0)
    hprev_ref[...] = hs[ct - 1:ct, :]
    ms = jnp.mean(hs * hs, axis=-1, keepdims=True)
    yb = hs * lax.rsqrt(ms + NORM_EPS) * og_ref[...]
    o_ref[...] = (yb * _silu(z_ref[...])).astype(o_ref.dtype)


def _lru(proj, prm, batch, seq, ct):
    def vec():
        return pl.BlockSpec((1, GROUP_W), lambda b, c: (0, 0))

    def mat(rows):
        return pl.BlockSpec((rows, GROUP_W), lambda b, c: (0, 0))

    return pl.pallas_call(
        _lru_kernel,
        grid=(batch, seq // ct),
        in_specs=[pl.BlockSpec((None, ct, GROUP_W), lambda b, c: (b, c, COL_UB // GROUP_W)),
                  pl.BlockSpec((None, ct, GROUP_W), lambda b, c: (b, c, COL_ZB // GROUP_W)),
                  mat(GROUP_W), vec(), mat(GROUP_W), vec(), vec(), vec()],
        out_specs=pl.BlockSpec((None, ct, GROUP_W), lambda b, c: (b, c, 0)),
        out_shape=jax.ShapeDtypeStruct((batch, seq, GROUP_W), BF16),
        scratch_shapes=[pltpu.VMEM((1, GROUP_W), F32)],
        compiler_params=_cparams(("parallel", "arbitrary")),
        name="rglru",
    )(proj, proj, prm["ga_w"], prm["ga_b"], prm["gx_w"], prm["gx_b"],
      prm["lam"], prm["lru_out_g"])


def _mla_pro_kernel(ql_ref, kvl_ref, kr_ref, qg_ref, kvg_ref, wq_ref, wqs_ref, wk_ref, wv_ref,
                    cq_ref, sq_ref, ck_ref, sk_ref, q_ref, k_ref, v_ref):
    def heads(tab_ref):
        return jnp.concatenate([tab_ref[...]] * N_HEADS, axis=1)

    ql = ql_ref[...]
    qn = ql * lax.rsqrt(jnp.mean(ql * ql, axis=-1, keepdims=True) + NORM_EPS) * qg_ref[...]
    qn = qn.astype(BF16)
    q = jnp.dot(qn, wq_ref[...], preferred_element_type=F32) * heads(cq_ref)
    q = q + jnp.dot(qn, wqs_ref[...], preferred_element_type=F32) * heads(sq_ref)
    q_ref[...] = q.astype(q_ref.dtype)

    kvl = kvl_ref[...]
    kvn = kvl * lax.rsqrt(jnp.mean(kvl * kvl, axis=-1, keepdims=True) + NORM_EPS) * kvg_ref[...]
    kvn = kvn.astype(BF16)
    kr = kr_ref[...]
    roped = kr * ck_ref[...] + pltpu.roll(kr * sk_ref[...], LANE - QK_ROPE, 1)
    placed = pltpu.roll(roped, QK_NOPE, 1)
    kfull = jnp.dot(kvn, wk_ref[...], preferred_element_type=F32) + jnp.concatenate([placed] * N_HEADS, axis=1)
    k_ref[...] = kfull.astype(k_ref.dtype)
    vt = lax.dot_general(wv_ref[...], kvn, (((1,), (1,)), ((), ())), preferred_element_type=F32)
    kblk = v_ref.shape[-1]
    for i in range(v_ref.shape[0]):
        v_ref[i] = vt[:, i * kblk:(i + 1) * kblk].astype(v_ref.dtype)


def _mla_pro(proj, prm, rope, batch, seq, tm, kblk):
    qk_w = N_HEADS * HEAD_PAD

    def whole(shape):
        return pl.BlockSpec(shape, lambda b, t: tuple(0 for _ in shape))

    def tab():
        return pl.BlockSpec((tm, LANE), lambda b, t: (t, 0))

    return pl.pallas_call(
        _mla_pro_kernel,
        grid=(batch, seq // tm),
        in_specs=[pl.BlockSpec((None, tm, Q_LORA), lambda b, t: (b, t, COL_QLAT // Q_LORA)),
                  pl.BlockSpec((None, tm, KV_LORA), lambda b, t: (b, t, COL_KVLAT // KV_LORA)),
                  pl.BlockSpec((None, tm, LANE), lambda b, t: (b, t, COL_KR // LANE)),
                  whole((1, Q_LORA)), whole((1, KV_LORA)),
                  whole((Q_LORA, qk_w)), whole((Q_LORA, qk_w)), whole((KV_LORA, qk_w)), whole((GROUP_W, KV_LORA)),
                  tab(), tab(), tab(), tab()],
        out_specs=[pl.BlockSpec((None, tm, qk_w), lambda b, t: (b, t, 0)),
                   pl.BlockSpec((None, tm, qk_w), lambda b, t: (b, t, 0)),
                   pl.BlockSpec((None, tm // kblk, GROUP_W, kblk), lambda b, t: (b, t, 0, 0))],
        out_shape=[jax.ShapeDtypeStruct((batch, seq, qk_w), BF16),
                   jax.ShapeDtypeStruct((batch, seq, qk_w), BF16),
                   jax.ShapeDtypeStruct((batch, seq // kblk, GROUP_W, kblk), BF16)],
        compiler_params=_cparams(("parallel", "parallel")),
        name="mla_pro",
    )(proj, proj, proj, prm["q_norm_g"], prm["kv_norm_g"], prm["wq"], prm["wq_sw"], prm["wk"], prm["wv_t"],
      rope["cq"], rope["sq"], rope["ck"], rope["sk"])


def _attn_kernel(q_ref, k_ref, vt_ref, o_ref, m_ref, acc_ref, s_ref, mx_ref, *, blk, kblk, n_steps):
    step = pl.program_id(2)
    ones = jnp.ones((ATT_ONES_ROWS, kblk), BF16)
    key = lax.broadcasted_iota(jnp.int32, (kblk, blk), 0)
    qry = lax.broadcasted_iota(jnp.int32, (kblk, blk), 1)
    n_buf = s_ref.shape[1]

    def stages(slot, i):
        n_keys = 2 * i + 2
        row0 = slot * blk

        def init():
            m_ref[slot] = jnp.full(m_ref.shape[1:], -jnp.inf, F32)
            acc_ref[slot] = jnp.zeros(acc_ref.shape[1:], F32)

        def scores(t):
            buf = t % n_buf
            for hh in range(2):
                lanes = slice(hh * HEAD_PAD, (hh + 1) * HEAD_PAD)
                s = lax.dot_general(k_ref[t * kblk:(t + 1) * kblk, lanes], q_ref[row0:row0 + blk, lanes],
                                    (((1,), (1,)), ((), ())), preferred_element_type=F32)
                s_ref[slot, buf, hh] = s
                if t < 2 * i:
                    mx_ref[slot, buf, hh:hh + 1, :] = jnp.max(s, axis=0, keepdims=True)

        def consume(t):
            buf = t % n_buf
            for hh in range(2):
                s = s_ref[slot, buf, hh]
                if t < 2 * i:
                    mx = mx_ref[slot, buf, hh:hh + 1, :]
                else:
                    s = jnp.where(key + (t - 2 * i) * kblk <= qry, s, -jnp.inf)
                    mx = jnp.max(s, axis=0, keepdims=True)
                m_prev = m_ref[slot, hh:hh + 1, :]
                m_new = jnp.maximum(m_prev, mx)
                alpha = jnp.exp2(m_prev - m_new)
                p = jnp.exp2(s - m_new).astype(BF16)
                m_ref[slot, hh:hh + 1, :] = m_new
                rows = slice(hh * ATT_ACC_ROWS, (hh + 1) * ATT_ACC_ROWS)
                vt_ext = jnp.concatenate([vt_ref[t, hh * V_DIM:(hh + 1) * V_DIM, :], ones], axis=0)
                acc_ref[slot, rows, :] = (alpha * acc_ref[slot, rows, :]
                                          + jnp.dot(vt_ext, p, preferred_element_type=F32))

        def finalize():
            outs = []
            for hh in range(2):
                base = hh * ATT_ACC_ROWS
                outs.append(acc_ref[slot, base:base + V_DIM, :] / acc_ref[slot, base + V_DIM:base + V_DIM + 1, :])
            o_ref[row0:row0 + blk, :] = jnp.concatenate(outs, axis=0).T

        ops = [init] + [functools.partial(scores, t) for t in range(min(ATT_AHEAD, n_keys))]
        for t in range(n_keys):
            if t + ATT_AHEAD < n_keys:
                ops.append(functools.partial(scores, t + ATT_AHEAD))
            ops.append(functools.partial(consume, t))
        return ops + [finalize]

    def run_step(j):
        queues = [stages(slot, ATT_SUB * j + slot) for slot in range(ATT_SUB)]
        for pos in range(max(len(ops) for ops in queues)):
            for ops in queues:
                if pos < len(ops):
                    ops[pos]()

    for j in range(n_steps):
        pl.when(step == j)(functools.partial(run_step, j))


def _attn(q, k, vt, batch, seq, blk, kblk):
    pair_w = 2 * HEAD_PAD
    n_buf = ATT_AHEAD + 1
    rows = ATT_SUB * blk
    return pl.pallas_call(
        functools.partial(_attn_kernel, blk=blk, kblk=kblk, n_steps=seq // rows),
        grid=(batch, N_HEADS // 2, seq // rows),
        in_specs=[pl.BlockSpec((None, rows, pair_w), lambda b, h, i: (b, i, h)),
                  pl.BlockSpec((None, seq, pair_w), lambda b, h, i: (b, 0, h)),
                  pl.BlockSpec((None, seq // kblk, LANE, kblk), lambda b, h, i: (b, 0, h, 0))],
        out_specs=pl.BlockSpec((None, rows, LANE), lambda b, h, i: (b, i, h)),
        out_shape=jax.ShapeDtypeStruct((batch, seq, GROUP_W), F32),
        scratch_shapes=[pltpu.VMEM((ATT_SUB, SUBLANES, blk), F32), pltpu.VMEM((ATT_SUB, 2 * ATT_ACC_ROWS, blk), F32),
                        pltpu.VMEM((ATT_SUB, n_buf, 2, kblk, blk), F32),
                        pltpu.VMEM((ATT_SUB, n_buf, SUBLANES, blk), F32)],
        compiler_params=_cparams(("parallel", "parallel", "arbitrary")),
        name="mla_attn",
    )(q, k, vt)


def _outproj_kernel(ya_ref, yb_ref, oc_ref, zc_ref, gc_ref, w_ref, h_ref, fg_ref, o_ref, *, final):
    oc = oc_ref[...]
    yc = oc * lax.rsqrt(jnp.mean(oc * oc, axis=-1, keepdims=True) + NORM_EPS) * gc_ref[...]
    yc = yc * _silu(zc_ref[...])
    acc = h_ref[...]
    acc = acc + jnp.dot(ya_ref[...], w_ref[0:GROUP_W, :], preferred_element_type=F32)
    acc = acc + jnp.dot(yb_ref[...], w_ref[GROUP_W:2 * GROUP_W, :], preferred_element_type=F32)
    acc = acc + _bdot(yc, w_ref[2 * GROUP_W:3 * GROUP_W, :])
    if final:
        acc = acc * lax.rsqrt(jnp.mean(acc * acc, axis=-1, keepdims=True) + NORM_EPS) * fg_ref[...]
    o_ref[...] = acc


def _outproj(ya, yb, oc, proj, h2d, mla_out_g, w_out, final_g, final, tm):
    m = h2d.shape[0]

    def rows(width, blk=0):
        return pl.BlockSpec((tm, width), lambda i, blk=blk: (i, blk))

    return pl.pallas_call(
        functools.partial(_outproj_kernel, final=final),
        grid=(m // tm,),
        in_specs=[rows(GROUP_W), rows(GROUP_W), rows(GROUP_W), rows(GROUP_W, COL_ZC // GROUP_W),
                  pl.BlockSpec((1, GROUP_W), lambda i: (0, 0)),
                  pl.BlockSpec((3 * GROUP_W, D_MODEL), lambda i: (0, 0)),
                  rows(D_MODEL),
                  pl.BlockSpec((1, D_MODEL), lambda i: (0, 0))],
        out_specs=rows(D_MODEL),
        out_shape=jax.ShapeDtypeStruct((m, D_MODEL), F32),
        compiler_params=_cparams(("parallel",)),
        name="outproj_final" if final else "outproj",
    )(ya, yb, oc, proj, mla_out_g, w_out, h2d, final_g)


def _rope_tables(seq):
    half = QK_ROPE // 2
    inv_freq = ROPE_BASE ** (-jnp.arange(half, dtype=F32) * 2.0 / QK_ROPE)
    ang = jnp.arange(seq, dtype=F32)[:, None] * inv_freq[None, :]
    cos2 = jnp.concatenate([jnp.cos(ang)] * 2, axis=1)
    sin2 = jnp.concatenate([jnp.sin(ang)] * 2, axis=1)
    zeros = lambda w: jnp.zeros((seq, w), F32)
    ones = jnp.ones((seq, QK_NOPE), F32)
    return {
        "cq": jnp.concatenate([ones, cos2, zeros(HEAD_PAD - QK_NOPE - QK_ROPE)], axis=1),
        "sq": jnp.concatenate([zeros(QK_NOPE), sin2, zeros(HEAD_PAD - QK_NOPE - QK_ROPE)], axis=1),
        "ck": jnp.concatenate([cos2, zeros(LANE - QK_ROPE)], axis=1),
        "sk": jnp.concatenate([zeros(QK_ROPE), sin2, zeros(LANE - 2 * QK_ROPE)], axis=1),
    }


def _swap_halves(w):
    half = w.shape[-1] // 2
    return jnp.concatenate([-w[..., half:], w[..., :half]], axis=-1)


def _block_diag(w):
    h, n, _ = w.shape
    eye = jnp.eye(h, dtype=w.dtype)
    return (eye[:, None, :, None] * w[:, :, None, :]).reshape(h * n, h * n)


def _layer_params(l, ln_g, w_in, rwkv_mu, rwkv_w0, rwkv_w2, rwkv_a0, rwkv_a2, rwkv_k_k, rwkv_k_a, rwkv_r_k,
                  rwkv_lnx_g, rwkv_lnx_b, lru_conv_w, lru_conv_b, lru_ga_w, lru_ga_b, lru_gx_w, lru_gx_b,
                  lru_lam, lru_out_g, mla_q_norm_g, mla_w_uq, mla_kv_norm_g, mla_w_ukv, mla_out_g, w_out):
    G = GROUP_W
    w = w_in[l]
    o_ua, o_za = 0, 3 * G + 2 * LORA
    o_ub = o_za + G
    o_zb = o_ub + G
    o_ql = o_zb + G
    o_kv = o_ql + Q_LORA
    o_kr = o_kv + KV_LORA
    o_zc = o_kr + QK_ROPE
    sl = lambda a, n: w[:, a:a + n]
    kr_w = sl(o_kr, QK_ROPE)
    w_re = jnp.concatenate([
        sl(o_ua, G), sl(o_ua + G, G), sl(o_ua + 2 * G, G), sl(o_za, G), sl(o_ub, G), sl(o_zb, G), sl(o_zc, G),
        sl(o_ql, Q_LORA), sl(o_kv, KV_LORA),
        kr_w, _swap_halves(kr_w), jnp.zeros((D_MODEL, LANE - 2 * QK_ROPE), F32),
        sl(o_ua + 3 * G, 2 * LORA)], axis=1).astype(BF16)
    row = lambda v: v.reshape(1, -1)
    mu = rwkv_mu[l]
    zl = jnp.zeros((LORA, G), F32)
    head_id = jnp.arange(RW_GROUP) // HEAD_DIM
    t_idx = jnp.arange(RW_CHUNK)
    scale = (QK_NOPE + QK_ROPE) ** -0.5 * LOG2E
    wq3 = mla_w_uq[l].reshape(Q_LORA, N_HEADS, QK_NOPE + QK_ROPE) * scale
    zq = lambda n: jnp.zeros((Q_LORA, N_HEADS, n), F32)
    wq = jnp.concatenate([wq3, zq(HEAD_PAD - QK_NOPE - QK_ROPE)], axis=2)
    wq_sw = jnp.concatenate([zq(QK_NOPE), _swap_halves(wq3[:, :, QK_NOPE:]),
                             zq(HEAD_PAD - QK_NOPE - QK_ROPE)], axis=2)
    wkv3 = mla_w_ukv[l].reshape(KV_LORA, N_HEADS, QK_NOPE + V_DIM)
    wk = jnp.concatenate([wkv3[:, :, :QK_NOPE], jnp.zeros((KV_LORA, N_HEADS, HEAD_PAD - QK_NOPE), F32)], axis=2)
    return {
        "ln_g": row(ln_g[l]), "w_in": w_re,
        "mu_rkv": row(mu[0:3 * G]),
        "mu_wa": row(mu[3 * G:3 * G + 2 * LORA]),
        "w0": row(rwkv_w0[l]), "w2p": jnp.concatenate([rwkv_w2[l], zl], axis=0).astype(BF16),
        "a0": row(rwkv_a0[l]), "a2p": jnp.concatenate([zl, rwkv_a2[l]], axis=0).astype(BF16),
        "k_k": row(rwkv_k_k[l]), "k_a": row(rwkv_k_a[l]), "r_k": row(rwkv_r_k[l]),
        "lnx_g": row(rwkv_lnx_g[l]), "lnx_b": row(rwkv_lnx_b[l]),
        "seg": (head_id[:, None] == head_id[None, :]).astype(BF16),
        "tri": (t_idx[None, :] <= t_idx[:, None]).astype(BF16),
        "conv_w": lru_conv_w[l], "conv_b": row(lru_conv_b[l]),
        "ga_w": _block_diag(lru_ga_w[l]).astype(BF16), "ga_b": row(lru_ga_b[l]),
        "gx_w": _block_diag(lru_gx_w[l]).astype(BF16), "gx_b": row(lru_gx_b[l]),
        "lam": row(lru_lam[l]), "lru_out_g": row(lru_out_g[l]),
        "q_norm_g": row(mla_q_norm_g[l]), "kv_norm_g": row(mla_kv_norm_g[l]),
        "wq": wq.reshape(Q_LORA, -1).astype(BF16), "wq_sw": wq_sw.reshape(Q_LORA, -1).astype(BF16),
        "wk": wk.reshape(KV_LORA, -1).astype(BF16),
        "wv_t": wkv3[:, :, QK_NOPE:].reshape(KV_LORA, -1).T.astype(BF16),
        "mla_out_g": row(mla_out_g[l]), "w_out": w_out[l].astype(BF16),
    }


def _tiles(seq):
    q_blk = min(ATT_Q_BLOCK, seq // ATT_SUB)
    return {"rows": min(ROW_TILE, seq),
            "out_rows": min(OUT_ROW_TILE, seq),
            "lru": min(LRU_CHUNK, seq),
            "q": q_blk, "key": q_blk // 2}


def kernel(x, ln_g, w_in, rwkv_mu, rwkv_w0, rwkv_w2, rwkv_a0, rwkv_a2, rwkv_k_k, rwkv_k_a, rwkv_r_k,
           rwkv_lnx_g, rwkv_lnx_b, lru_conv_w, lru_conv_b, lru_ga_w, lru_ga_b, lru_gx_w, lru_gx_b, lru_lam,
           lru_out_g, mla_q_norm_g, mla_w_uq, mla_kv_norm_g, mla_w_ukv, mla_out_g, w_out, final_g):
    batch, seq, _ = x.shape
    assert batch % RW_STREAMS == 0 and seq % RW_CHUNK == 0, (batch, seq)
    tiles = _tiles(seq)
    rope = _rope_tables(seq)
    fg = final_g.reshape(1, -1)
    h = x.reshape(batch * seq, D_MODEL)
    for l in range(DEPTH):
        prm = _layer_params(l, ln_g, w_in, rwkv_mu, rwkv_w0, rwkv_w2, rwkv_a0, rwkv_a2, rwkv_k_k, rwkv_k_a,
                            rwkv_r_k, rwkv_lnx_g, rwkv_lnx_b, lru_conv_w, lru_conv_b, lru_ga_w, lru_ga_b,
                            lru_gx_w, lru_gx_b, lru_lam, lru_out_g, mla_q_norm_g, mla_w_uq, mla_kv_norm_g,
                            mla_w_ukv, mla_out_g, w_out)
        proj2d = _inproj(h, prm, tiles["rows"], seq)
        proj = proj2d.reshape(batch, seq, D_PROJ)
        ya = _rwkv(proj, prm, batch, seq)
        yb = _lru(proj, prm, batch, seq, tiles["lru"])
        q, k, vt = _mla_pro(proj, prm, rope, batch, seq, tiles["q"], tiles["key"])
        oc = _attn(q, k, vt, batch, seq, tiles["q"], tiles["key"])
        h = _outproj(ya.reshape(batch * seq, GROUP_W), yb.reshape(batch * seq, GROUP_W),
                     oc.reshape(batch * seq, GROUP_W), proj2d, h, prm["mla_out_g"], prm["w_out"], fg,
                     l == DEPTH - 1, tiles["out_rows"])
    return h.reshape(batch, seq, D_MODEL)
```
